```python
import math, functools
import jax
import jax.numpy as jnp
from jax import lax
import numpy as np

D_MODEL = 1024
BATCH = 4
SEQ = 8192
DEPTH = 4

GRID_W = 64
CTX_LEN = 256
N_MIXERS = 3
ALPHA = (2 * DEPTH) ** 0.25
BETA = (8 * DEPTH) ** -0.25
LN_EPS = 1e-5
GN_EPS = 1e-5
RMS_EPS = 1e-6
LNX_EPS = 64e-5

RET_HEADS = 4
RET_DK = D_MODEL // RET_HEADS
RET_DV = 2 * RET_DK
RET_CHUNK = 128
ROPE_BASE = 10000.0

DN_QK_HEADS = 8
DN_V_HEADS = 16
DN_HEAD_DIM = 128
DN_CHUNK = 64
DN_CONV = 5

RWKV_HEAD = 64
RWKV_HEADS = D_MODEL // RWKV_HEAD
RWKV_DECAY_LORA = 64
RWKV_A_LORA = 64
RWKV_GATE_LORA = 128

FFN_DIM = 2816
N_EXPERTS = 8
TOP_K = 2
EXPERT_DIM = 3584
MOE_BLOCK = 256

N_RET = (DEPTH + 2) // 3
N_DN = (DEPTH + 1) // 3
N_RWKV = DEPTH // 3
N_DENSE = (DEPTH + 1) // 2
N_MOE = DEPTH // 2

RET_IN = 2 * RET_HEADS * RET_DK + 2 * RET_HEADS * RET_DV
DN_QK_W = DN_QK_HEADS * DN_HEAD_DIM
DN_V_W = DN_V_HEADS * DN_HEAD_DIM
DN_IN = 2 * DN_QK_W + 2 * DN_V_W + 4 * DN_V_HEADS

kernel_name = 'hybrid_retention_deltanet_rwkv7_moe_dit'


def layer_norm(x, g, b):
    xf = x.astype(jnp.float32)
    mu = jnp.mean(xf, -1, keepdims=True)
    var = jnp.mean(jnp.square(xf - mu), -1, keepdims=True)
    return ((xf - mu) * lax.rsqrt(var + LN_EPS) * g + b).astype(x.dtype)


def group_norm(x, eps):
    xf = x.astype(jnp.float32)
    mu = jnp.mean(xf, -1, keepdims=True)
    var = jnp.mean(jnp.square(xf - mu), -1, keepdims=True)
    return (xf - mu) * lax.rsqrt(var + eps)


def rms_norm(x, eps):
    xf = x.astype(jnp.float32)
    return xf * lax.rsqrt(jnp.mean(xf * xf, -1, keepdims=True) + eps)


def l2_normalize(x, eps=1e-6):
    xf = x.astype(jnp.float32)
    return (xf * lax.rsqrt(jnp.sum(xf * xf, -1, keepdims=True) + eps)).astype(x.dtype)


def rope_2d(x, row_pos, col_pos):
    half = x.shape[-1] // 2
    quarter = half // 2
    inv = ROPE_BASE ** (-jnp.arange(quarter, dtype=jnp.float32) / quarter)

    def rot(xa, pos):
        ang = pos.astype(jnp.float32)[:, None] * inv
        cos = jnp.cos(ang)[None, :, None, :]
        sin = jnp.sin(ang)[None, :, None, :]
        x1, x2 = xa[..., :quarter], xa[..., quarter:]
        return jnp.concatenate([x1 * cos - x2 * sin, x1 * sin + x2 * cos], -1)

    return jnp.concatenate([rot(x[..., :half], row_pos), rot(x[..., half:], col_pos)], -1).astype(x.dtype)


def centred_shift(x):
    p = jnp.pad(x, ((0, 0), (1, 1), (0, 0)))
    return 0.5 * (p[:, :-2] + p[:, 2:]) - x


def short_conv(x, w):
    ch = x.shape[-1]
    pad = (w.shape[0] - 1) // 2
    return lax.conv_general_dilated(x, w.astype(x.dtype)[:, None, :], (1,), [(pad, pad)],
                                    dimension_numbers=('NWC', 'WIO', 'NWC'), feature_group_count=ch)


def retention_scan(q, k, v, log_gamma, s0):
    b, h, t, _ = q.shape
    c = RET_CHUNK
    n = t // c
    pos = jnp.arange(c, dtype=jnp.float32)
    dist = pos[:, None] - pos[None, :]
    lower = dist >= 0
    inner = jnp.where(lower, jnp.exp(jnp.where(lower, dist, 0.0) * log_gamma[:, None, None]), 0.0)
    q_decay = jnp.exp((pos + 1.0) * log_gamma[:, None])[..., None]
    k_decay = jnp.exp((c - 1.0 - pos) * log_gamma[:, None])[..., None]
    chunk_decay = jnp.exp(c * log_gamma)[:, None, None]
    to_chunks = lambda a: jnp.moveaxis(a.reshape(b, h, n, c, a.shape[-1]), 2, 0)

    def step(s, inp):
        qc, kc, vc = inp
        scores = jnp.einsum('bhid,bhjd->bhij', qc, kc) * inner
        o = jnp.einsum('bhij,bhjv->bhiv', scores, vc) + jnp.einsum('bhid,bhdv->bhiv', qc * q_decay, s)
        s = s * chunk_decay + jnp.einsum('bhjd,bhjv->bhdv', kc * k_decay, vc)
        return s, o

    s, o = lax.scan(step, s0, (to_chunks(q), to_chunks(k), to_chunks(v)))
    return jnp.moveaxis(o, 0, 2).reshape(b, h, t, -1), s


def retention_mixer(h_ctx, h_lat, row_pos, col_pos, w_in, decay_logit, gn_g, w_out, with_ctx_out):
    hk = RET_HEADS * RET_DK
    hv = RET_HEADS * RET_DV

    def project(h, pos):
        b, t, _ = h.shape
        q, k, v, g = jnp.split(h @ w_in, [hk, 2 * hk, 2 * hk + hv], axis=-1)
        q = q.reshape(b, t, RET_HEADS, RET_DK)
        k = k.reshape(b, t, RET_HEADS, RET_DK) * (RET_DK ** -0.5)
        if pos is not None:
            q = rope_2d(q, *pos)
            k = rope_2d(k, *pos)
        v = v.reshape(b, t, RET_HEADS, RET_DV)
        bh = lambda a: jnp.swapaxes(a, 1, 2)
        return bh(q), bh(k), bh(v), g

    def finish(o, g):
        b, _, t, _ = o.shape
        o = group_norm(jnp.swapaxes(o, 1, 2), GN_EPS).reshape(b, t, hv) * gn_g
        return (jax.nn.silu(g) * o).astype(g.dtype) @ w_out

    log_gamma = jax.nn.log_sigmoid(decay_logit.astype(jnp.float32))
    qc, kc, vc, gc = project(h_ctx, None)
    ql, kl, vl, gl = project(h_lat, (row_pos, col_pos))
    s0 = jnp.zeros((h_lat.shape[0], RET_HEADS, RET_DK, RET_DV), jnp.float32)
    fl = lambda a: jnp.flip(a, axis=2)
    oc_f, sc_f = retention_scan(qc, kc, vc, log_gamma[0], s0)
    oc_b, sc_b = retention_scan(fl(qc), fl(kc), fl(vc), log_gamma[1], s0)
    ol_f, _ = retention_scan(ql, kl, vl, log_gamma[0], sc_f)
    ol_b, _ = retention_scan(fl(ql), fl(kl), fl(vl), log_gamma[1], sc_b)
    o_lat = finish(ol_f + fl(ol_b), gl)
    o_ctx = finish(oc_f + fl(oc_b), gc) if with_ctx_out else None
    return o_ctx, o_lat


def gdn_chunk_scan(q, k, v, g, beta, s0):
    b, h, t, _ = q.shape
    c = DN_CHUNK
    n = t // c
    f32 = jnp.float32
    ch = lambda a: a.astype(f32).reshape(b, h, n, c, *a.shape[3:])
    q, k, v, g, beta = ch(q), ch(k), ch(v), ch(g), ch(beta)
    gc = jnp.cumsum(g, axis=-1)
    idx = jnp.arange(c)
    incl = idx[:, None] >= idx[None, :]
    strict = idx[:, None] > idx[None, :]
    decay = jnp.exp(jnp.where(incl, gc[..., :, None] - gc[..., None, :], -jnp.inf))
    kb = k * beta[..., None]
    a_mat = jnp.where(strict, jnp.einsum('bhnid,bhnjd->bhnij', kb, k) * decay, 0.0) + jnp.eye(c, dtype=f32)
    u = lax.linalg.triangular_solve(a_mat, v * beta[..., None], left_side=True, lower=True, unit_diagonal=True)
    w = lax.linalg.triangular_solve(a_mat, kb * jnp.exp(gc)[..., None], left_side=True, lower=True, unit_diagonal=True)
    qk = jnp.einsum('bhnid,bhnjd->bhnij', q, k) * decay
    qg = q * jnp.exp(gc)[..., None]
    kt = k * jnp.exp(gc[..., -1:] - gc)[..., None]
    tail = jnp.exp(gc[..., -1])[..., None, None]
    sw = lambda a: jnp.moveaxis(a, 2, 0)

    def step(s, inp):
        qg_c, qk_c, u_c, w_c, kt_c, tail_c = inp
        v_new = u_c - jnp.einsum('bhcd,bhdv->bhcv', w_c, s)
        o = jnp.einsum('bhcd,bhdv->bhcv', qg_c, s) + jnp.einsum('bhij,bhjv->bhiv', qk_c, v_new)
        s = s * tail_c + jnp.einsum('bhcd,bhcv->bhdv', kt_c, v_new)
        return s, o

    s, o = lax.scan(step, s0, (sw(qg), sw(qk), sw(u), sw(w), sw(kt), sw(tail)))
    return jnp.moveaxis(o, 0, 2).reshape(b, h, t, -1), s


def deltanet_mixer(h_ctx, h_lat, w_in, conv_w, a_log, dt_bias, norm_g, w_out, with_ctx_out):
    rep = DN_V_HEADS // DN_QK_HEADS

    def project(h):
        b, t, _ = h.shape
        qkv, z, ab = jnp.split(h @ w_in, [2 * DN_QK_W + DN_V_W, 2 * DN_QK_W + 2 * DN_V_W], axis=-1)
        qkv = jax.nn.silu(short_conv(qkv, conv_w))
        q, k, v = jnp.split(qkv, [DN_QK_W, 2 * DN_QK_W], axis=-1)
        q = jnp.repeat(l2_normalize(q.reshape(b, t, DN_QK_HEADS, DN_HEAD_DIM)), rep, axis=2) * (DN_HEAD_DIM ** -0.5)
        k = jnp.repeat(l2_normalize(k.reshape(b, t, DN_QK_HEADS, DN_HEAD_DIM)), rep, axis=2)
        v = v.reshape(b, t, DN_V_HEADS, DN_HEAD_DIM)
        ab = ab.astype(jnp.float32).reshape(b, t, 2, 2, DN_V_HEADS)
        g = -jnp.exp(a_log.astype(jnp.float32)) * jax.nn.softplus(ab[:, :, :, 0] + dt_bias)
        beta = jax.nn.sigmoid(ab[:, :, :, 1])
        bh = lambda a: jnp.swapaxes(a, 1, 2)
        return bh(q), bh(k), bh(v), jnp.moveaxis(g, 1, -1), jnp.moveaxis(beta, 1, -1), z

    def finish(o, z):
        b, _, t, _ = o.shape
        o = rms_norm(jnp.swapaxes(o, 1, 2), RMS_EPS) * norm_g
        o = o * jax.nn.silu(z.reshape(b, t, DN_V_HEADS, DN_HEAD_DIM))
        return o.reshape(b, t, DN_V_W).astype(z.dtype) @ w_out

    qc, kc, vc, gcx, bcx, zc = project(h_ctx)
    ql, kl, vl, glt, blt, zl = project(h_lat)
    s0 = jnp.zeros((h_lat.shape[0], DN_V_HEADS, DN_HEAD_DIM, DN_HEAD_DIM), jnp.float32)
    fl = lambda a: jnp.flip(a, axis=2)
    oc_f, sc_f = gdn_chunk_scan(qc, kc, vc, gcx[:, 0], bcx[:, 0], s0)
    oc_b, sc_b = gdn_chunk_scan(fl(qc), fl(kc), fl(vc), fl(gcx[:, 1]), fl(bcx[:, 1]), s0)
    ol_f, _ = gdn_chunk_scan(ql, kl, vl, glt[:, 0], blt[:, 0], sc_f)
    ol_b, _ = gdn_chunk_scan(fl(ql), fl(kl), fl(vl), fl(glt[:, 1]), fl(blt[:, 1]), sc_b)
    o_lat = finish(ol_f + fl(ol_b), zl)
    o_ctx = finish(oc_f + fl(oc_b), zc) if with_ctx_out else None
    return o_ctx, o_lat


def rwkv7_scan(r, w, k, v, a, b, s0):
    tm = lambda u: jnp.moveaxis(u.astype(jnp.float32), 1, 0)

    def step(s, inp):
        r_t, w_t, k_t, v_t, a_t, b_t = inp
        sa = jnp.einsum('bhvk,bhk->bhv', s, a_t)
        s = s * w_t[:, :, None, :] + sa[..., None] * b_t[:, :, None, :] + v_t[..., None] * k_t[:, :, None, :]
        return s, jnp.einsum('bhvk,bhk->bhv', s, r_t)

    s, o = lax.scan(step, s0, (tm(r), tm(w), tm(k), tm(v), tm(a), tm(b)))
    return jnp.moveaxis(o, 0, 1), s


def rwkv7_mixer(h_ctx, h_lat, mix, w_rkv, w0, w1, w2, a0, a1, a2, g1, g2, k_k, k_a, r_k, lnx_g, w_out, with_ctx_out):
    heads = lambda u: u.reshape(*u.shape[:-1], RWKV_HEADS, RWKV_HEAD)

    def project(h):
        xx = centred_shift(h)
        r = (h + xx * mix[0]) @ w_rkv[0]
        k = (h + xx * mix[1]) @ w_rkv[1]
        v = (h + xx * mix[2]) @ w_rkv[2]
        xw, xa, xg = h + xx * mix[3], h + xx * mix[4], h + xx * mix[5]
        w_logit = w0[:, None, None, :] + jnp.einsum('zbtr,zrc->zbtc', jnp.tanh(jnp.einsum('btc,zcr->zbtr', xw, w1)), w2)
        decay = jnp.exp(-jnp.exp(-jax.nn.softplus(-w_logit.astype(jnp.float32)) - 0.5))
        a = jax.nn.sigmoid(a0[:, None, None, :] + jnp.einsum('zbtr,zrc->zbtc', jnp.einsum('btc,zcr->zbtr', xa, a1), a2))
        g = jax.nn.sigmoid(xg @ g1) @ g2
        kk = l2_normalize(heads(k * k_k))
        k_dir = heads(k[None] * (1.0 + (a - 1.0) * k_a))
        return heads(r), heads(decay), k_dir, heads(v), kk, heads(a), g

    def scan_args(r, dec, kd, v, kk, a, z):
        return r, dec[z], kd[z], v, -kk, kk * a[z]

    def finish(o, r, k_dir, v, g):
        b, t = o.shape[:2]
        bonus = jnp.sum(r[None] * k_dir * r_k, axis=(0, -1))[..., None] * v
        o = group_norm(o, LNX_EPS).reshape(b, t, D_MODEL) * lnx_g + bonus.reshape(b, t, D_MODEL)
        return (o * g).astype(g.dtype) @ w_out

    rc, dc, kdc, vc, kkc, ac, gc = project(h_ctx)
    rl, dl, kdl, vl, kkl, al, gl = project(h_lat)
    s0 = jnp.zeros((h_lat.shape[0], RWKV_HEADS, RWKV_HEAD, RWKV_HEAD), jnp.float32)
    fl = lambda u: jnp.flip(u, axis=1)
    oc_f, sc_f = rwkv7_scan(*scan_args(rc, dc, kdc, vc, kkc, ac, 0), s0)
    oc_b, sc_b = rwkv7_scan(*map(fl, scan_args(rc, dc, kdc, vc, kkc, ac, 1)), s0)
    ol_f, _ = rwkv7_scan(*scan_args(rl, dl, kdl, vl, kkl, al, 0), sc_f)
    ol_b, _ = rwkv7_scan(*map(fl, scan_args(rl, dl, kdl, vl, kkl, al, 1)), sc_b)
    o_lat = finish(ol_f + fl(ol_b), rl, kdl, vl, gl)
    o_ctx = finish(oc_f + fl(oc_b), rc, kdc, vc, gc) if with_ctx_out else None
    return o_ctx, o_lat


def swiglu(h, w_gu, w_down):
    gt, up = jnp.split(h @ w_gu, 2, axis=-1)
    return (jax.nn.silu(gt) * up) @ w_down


def moe_swiglu(h, w_router, w_gu, w_down):
    shp = h.shape
    x = h.reshape(-1, shp[-1])
    n = x.shape[0]
    logits = (x @ w_router).astype(jnp.float32)
    top_logit, top_e = lax.top_k(logits, TOP_K)
    gate = jax.nn.softmax(top_logit, axis=-1)
    flat_e = top_e.reshape(-1)
    flat_tok = jnp.repeat(jnp.arange(n, dtype=jnp.int32), TOP_K)
    flat_gate = gate.reshape(-1)
    order = jnp.argsort(flat_e)
    e_sorted = flat_e[order]
    counts = jnp.bincount(flat_e, length=N_EXPERTS)
    padded = (counts + MOE_BLOCK - 1) // MOE_BLOCK * MOE_BLOCK
    start = jnp.cumsum(counts) - counts
    pend = jnp.cumsum(padded)
    pstart = pend - padded
    slot = pstart[e_sorted] + jnp.arange(n * TOP_K) - start[e_sorted]
    n_slots = (n * TOP_K + MOE_BLOCK - 1) // MOE_BLOCK * MOE_BLOCK + N_EXPERTS * MOE_BLOCK
    n_blocks = n_slots // MOE_BLOCK
    slot_tok = jnp.full((n_slots,), n, jnp.int32).at[slot].set(flat_tok[order])
    slot_gate = jnp.zeros((n_slots,), jnp.float32).at[slot].set(flat_gate[order])
    block_e = jnp.minimum(jnp.sum(jnp.arange(n_blocks)[:, None] * MOE_BLOCK >= pend[None, :], axis=1), N_EXPERTS - 1)
    x_pad = jnp.concatenate([x, jnp.zeros((1, x.shape[-1]), x.dtype)], 0)
    xb = x_pad[slot_tok].reshape(n_blocks, MOE_BLOCK, -1)
    yb = lax.map(lambda args: swiglu(args[0], w_gu[args[1]], w_down[args[1]]), (xb, block_e))
    yb = yb.reshape(n_slots, -1)
    y = jnp.zeros((n + 1, yb.shape[-1]), yb.dtype).at[slot_tok].add(yb * slot_gate[:, None].astype(yb.dtype))
    return y[:n].reshape(shp)


def setup_inputs(seed: int = 0) -> dict:
    key = jax.random.key(seed)
    ks = iter(jax.random.split(key, 48))
    f32 = jnp.float32
    D = D_MODEL
    nrm = lambda shape, scale: scale * jax.random.normal(next(ks), shape, f32)
    uni = lambda shape, lo, hi: jax.random.uniform(next(ks), shape, f32, minval=lo, maxval=hi)

    ret_gamma = 1.0 - 2.0 ** (-5.0 - jnp.arange(RET_HEADS, dtype=f32))
    ret_logit = jnp.log(ret_gamma) - jnp.log1p(-ret_gamma)
    dt = jnp.exp(uni((N_DN, 2, DN_V_HEADS), math.log(1e-3), math.log(1e-1)))
    n_idx = jnp.arange(D, dtype=f32) / (D - 1)
    decay_speed = -7.0 + 5.0 * n_idx ** (0.85 + 0.5 ** 0.5)

    return {
        'x': nrm((BATCH, SEQ, D), 1.0),
        'c': nrm((BATCH, D), 1.0),
        'ctx': nrm((BATCH, CTX_LEN, D), 1.0),
        'c_ctx': nrm((D,), 1.0),
        'mod_w': nrm((DEPTH, D, 6 * D), 0.5 * D ** -0.5),
        'mod_b': nrm((DEPTH, 6 * D), 0.02),
        'ln_g': 1.0 + nrm((DEPTH, 2, D), 0.02),
        'ln_b': nrm((DEPTH, 2, D), 0.02),
        'ret_w_in': nrm((N_RET, D, RET_IN), D ** -0.5),
        'ret_decay': ret_logit + nrm((N_RET, 2, RET_HEADS), 0.01),
        'ret_gn_g': 1.0 + nrm((N_RET, RET_HEADS * RET_DV), 0.02),
        'ret_w_out': nrm((N_RET, RET_HEADS * RET_DV, D), BETA * (RET_HEADS * RET_DV) ** -0.5),
        'dn_w_in': nrm((N_DN, D, DN_IN), D ** -0.5),
        'dn_conv_w': nrm((N_DN, DN_CONV, 2 * DN_QK_W + DN_V_W), DN_CONV ** -0.5),
        'dn_a_log': jnp.log(uni((N_DN, 2, DN_V_HEADS), 1.0, 16.0)),
        'dn_dt_bias': dt + jnp.log(-jnp.expm1(-dt)),
        'dn_norm_g': 1.0 + nrm((N_DN, DN_HEAD_DIM), 0.02),
        'dn_w_out': nrm((N_DN, DN_V_W, D), BETA * DN_V_W ** -0.5),
        'rk_mix': uni((N_RWKV, 6, D), 0.0, 1.0),
        'rk_w_rkv': nrm((N_RWKV, 3, D, D), D ** -0.5),
        'rk_w0': decay_speed + 0.5 + nrm((N_RWKV, 2, D), 0.1),
        'rk_w1': nrm((N_RWKV, 2, D, RWKV_DECAY_LORA), D ** -0.5),
        'rk_w2': nrm((N_RWKV, 2, RWKV_DECAY_LORA, D), 0.1 * RWKV_DECAY_LORA ** -0.5),
        'rk_a0': nrm((N_RWKV, 2, D), 0.1),
        'rk_a1': nrm((N_RWKV, 2, D, RWKV_A_LORA), D ** -0.5),
        'rk_a2': nrm((N_RWKV, 2, RWKV_A_LORA, D), 0.3 * RWKV_A_LORA ** -0.5),
        'rk_g1': nrm((N_RWKV, D, RWKV_GATE_LORA), D ** -0.5),
        'rk_g2': nrm((N_RWKV, RWKV_GATE_LORA, D), RWKV_GATE_LORA ** -0.5),
        'rk_k_k': 0.85 + nrm((N_RWKV, D), 0.02),
        'rk_k_a': 1.0 + nrm((N_RWKV, D), 0.02),
        'rk_r_k': nrm((N_RWKV, RWKV_HEADS, RWKV_HEAD), 0.1),
        'rk_lnx_g': 1.0 + nrm((N_RWKV, D), 0.02),
        'rk_w_out': nrm((N_RWKV, D, D), BETA * D ** -0.5),
        'ffn_w_gu': nrm((N_DENSE, D, 2 * FFN_DIM), D ** -0.5),
        'ffn_w_down': nrm((N_DENSE, FFN_DIM, D), BETA * FFN_DIM ** -0.5),
        'moe_router': nrm((N_MOE, D, N_EXPERTS), D ** -0.5),
        'moe_w_gu': nrm((N_MOE, N_EXPERTS, D, 2 * EXPERT_DIM), D ** -0.5),
        'moe_w_down': nrm((N_MOE, N_EXPERTS, EXPERT_DIM, D), BETA * EXPERT_DIM ** -0.5),
    }


def reference(x, c, ctx, c_ctx, mod_w, mod_b, ln_g, ln_b,
              ret_w_in, ret_decay, ret_gn_g, ret_w_out,
              dn_w_in, dn_conv_w, dn_a_log, dn_dt_bias, dn_norm_g, dn_w_out,
              rk_mix, rk_w_rkv, rk_w0, rk_w1, rk_w2, rk_a0, rk_a1, rk_a2, rk_g1, rk_g2,
              rk_k_k, rk_k_a, rk_r_k, rk_lnx_g, rk_w_out,
              ffn_w_gu, ffn_w_down, moe_router, moe_w_gu, moe_w_down):
    t = x.shape[1]
    n_ctx = ctx.shape[1]
    rows = t // GRID_W
    row_pos = jnp.repeat(jnp.arange(rows), GRID_W)
    col_pos = jnp.tile(jnp.arange(GRID_W), rows)
    s_lat = jax.nn.silu(c)
    s_ctx = jax.nn.silu(c_ctx)
    lat, cx = x, ctx
    for i in range(DEPTH):
        last = i == DEPTH - 1
        m_lat = (s_lat @ mod_w[i] + mod_b[i])[:, None, :]
        m_ctx = (s_ctx @ mod_w[i] + mod_b[i])[None, None, :]
        sh1, sc1, ga1, sh2, sc2, ga2 = jnp.split(m_lat, 6, axis=-1)
        csh1, csc1, cga1, csh2, csc2, cga2 = jnp.split(m_ctx, 6, axis=-1)
        h_lat = lat * (1.0 + sc1) + sh1
        h_ctx = cx * (1.0 + csc1) + csh1
        kind, j = i % N_MIXERS, i // N_MIXERS
        if kind == 0:
            o_ctx, o_lat = retention_mixer(h_ctx, h_lat, row_pos, col_pos, ret_w_in[j], ret_decay[j],
                                           ret_gn_g[j], ret_w_out[j], not last)
        elif kind == 1:
            o_ctx, o_lat = deltanet_mixer(h_ctx, h_lat, dn_w_in[j], dn_conv_w[j], dn_a_log[j], dn_dt_bias[j],
                                          dn_norm_g[j], dn_w_out[j], not last)
        else:
            o_ctx, o_lat = rwkv7_mixer(h_ctx, h_lat, rk_mix[j], rk_w_rkv[j], rk_w0[j], rk_w1[j], rk_w2[j],
                                       rk_a0[j], rk_a1[j], rk_a2[j], rk_g1[j], rk_g2[j], rk_k_k[j], rk_k_a[j],
                                       rk_r_k[j], rk_lnx_g[j], rk_w_out[j], not last)
        if i % 2 == 0:
            channel = functools.partial(swiglu, w_gu=ffn_w_gu[i // 2], w_down=ffn_w_down[i // 2])
        else:
            channel = functools.partial(moe_swiglu, w_router=moe_router[i // 2], w_gu=moe_w_gu[i // 2],
                                        w_down=moe_w_down[i // 2])
        lat = layer_norm(ALPHA * lat + (1.0 + ga1) * o_lat, ln_g[i, 0], ln_b[i, 0])
        h_lat = lat * (1.0 + sc2) + sh2
        if last:
            lat = layer_norm(ALPHA * lat + (1.0 + ga2) * channel(h_lat), ln_g[i, 1], ln_b[i, 1])
        else:
            cx = layer_norm(ALPHA * cx + (1.0 + cga1) * o_ctx, ln_g[i, 0], ln_b[i, 0])
            h_ctx = cx * (1.0 + csc2) + csh2
            f = channel(jnp.concatenate([h_ctx, h_lat], axis=1))
            cx = layer_norm(ALPHA * cx + (1.0 + cga2) * f[:, :n_ctx], ln_g[i, 1], ln_b[i, 1])
            lat = layer_norm(ALPHA * lat + (1.0 + ga2) * f[:, n_ctx:], ln_g[i, 1], ln_b[i, 1])
    return lat
```

```python
import math, functools
import jax
import jax.numpy as jnp
from jax import lax
import numpy as np
from jax.experimental import pallas as pl
from jax.experimental.pallas import tpu as pltpu

D_MODEL = 1024
DEPTH = 4
GRID_W = 64
N_MIXERS = 3
ALPHA = (2 * DEPTH) ** 0.25
LN_EPS = 1e-5
GN_EPS = 1e-5
RMS_EPS = 1e-6
LNX_EPS = 64e-5

RET_HEADS = 4
RET_DK = D_MODEL // RET_HEADS
RET_DV = 2 * RET_DK
RET_CHUNK = 128
ROPE_BASE = 10000.0

DN_QK_HEADS = 8
DN_V_HEADS = 16
DN_HEAD_DIM = 128
DN_CHUNK = 64
DN_QK_W = DN_QK_HEADS * DN_HEAD_DIM
DN_V_W = DN_V_HEADS * DN_HEAD_DIM

RWKV_HEAD = 64
RWKV_HEADS = D_MODEL // RWKV_HEAD

N_EXPERTS = 8
TOP_K = 2
MOE_BLOCK = 256


def group_norm(x, eps):
    xf = x.astype(jnp.float32)
    mu = jnp.mean(xf, -1, keepdims=True)
    var = jnp.mean(jnp.square(xf - mu), -1, keepdims=True)
    return (xf - mu) * lax.rsqrt(var + eps)


def rms_norm(x, eps):
    xf = x.astype(jnp.float32)
    return xf * lax.rsqrt(jnp.mean(xf * xf, -1, keepdims=True) + eps)


def l2_normalize(x, eps=1e-6):
    xf = x.astype(jnp.float32)
    return (xf * lax.rsqrt(jnp.sum(xf * xf, -1, keepdims=True) + eps)).astype(x.dtype)


def rope_2d(x, row_pos, col_pos):
    half = x.shape[-1] // 2
    quarter = half // 2
    inv = ROPE_BASE ** (-jnp.arange(quarter, dtype=jnp.float32) / quarter)

    def rot(xa, pos):
        ang = pos.astype(jnp.float32)[:, None] * inv
        cos = jnp.cos(ang)[None, :, None, :]
        sin = jnp.sin(ang)[None, :, None, :]
        x1, x2 = xa[..., :quarter], xa[..., quarter:]
        return jnp.concatenate([x1 * cos - x2 * sin, x1 * sin + x2 * cos], -1)

    return jnp.concatenate([rot(x[..., :half], row_pos), rot(x[..., half:], col_pos)], -1).astype(x.dtype)


def centred_shift(x):
    p = jnp.pad(x, ((0, 0), (1, 1), (0, 0)))
    return 0.5 * (p[:, :-2] + p[:, 2:]) - x


def short_conv(x, w):
    ch = x.shape[-1]
    pad = (w.shape[0] - 1) // 2
    return lax.conv_general_dilated(x, w.astype(x.dtype)[:, None, :], (1,), [(pad, pad)],
                                    dimension_numbers=('NWC', 'WIO', 'NWC'), feature_group_count=ch)


def retention_scan(q, k, v, log_gamma, s0):
    b, h, t, _ = q.shape
    c = RET_CHUNK
    n = t // c
    pos = jnp.arange(c, dtype=jnp.float32)
    dist = pos[:, None] - pos[None, :]
    lower = dist >= 0
    inner = jnp.where(lower, jnp.exp(jnp.where(lower, dist, 0.0) * log_gamma[:, None, None]), 0.0)
    q_decay = jnp.exp((pos + 1.0) * log_gamma[:, None])[..., None]
    k_decay = jnp.exp((c - 1.0 - pos) * log_gamma[:, None])[..., None]
    chunk_decay = jnp.exp(c * log_gamma)[:, None, None]
    to_chunks = lambda a: jnp.moveaxis(a.reshape(b, h, n, c, a.shape[-1]), 2, 0)

    def step(s, inp):
        qc, kc, vc = inp
        scores = jnp.einsum('bhid,bhjd->bhij', qc, kc) * inner
        o = jnp.einsum('bhij,bhjv->bhiv', scores, vc) + jnp.einsum('bhid,bhdv->bhiv', qc * q_decay, s)
        s = s * chunk_decay + jnp.einsum('bhjd,bhjv->bhdv', kc * k_decay, vc)
        return s, o

    s, o = lax.scan(step, s0, (to_chunks(q), to_chunks(k), to_chunks(v)))
    return jnp.moveaxis(o, 0, 2).reshape(b, h, t, -1), s


def retention_mixer(h_ctx, h_lat, row_pos, col_pos, w_in, decay_logit, gn_g, w_out, with_ctx_out):
    hk = RET_HEADS * RET_DK
    hv = RET_HEADS * RET_DV

    def project(h, pos):
        b, t, _ = h.shape
        q, k, v, g = jnp.split(h @ w_in, [hk, 2 * hk, 2 * hk + hv], axis=-1)
        q = q.reshape(b, t, RET_HEADS, RET_DK)
        k = k.reshape(b, t, RET_HEADS, RET_DK) * (RET_DK ** -0.5)
        if pos is not None:
            q = rope_2d(q, *pos)
            k = rope_2d(k, *pos)
        v = v.reshape(b, t, RET_HEADS, RET_DV)
        bh = lambda a: jnp.swapaxes(a, 1, 2)
        return bh(q), bh(k), bh(v), g

    def finish(o, g):
        b, _, t, _ = o.shape
        o = group_norm(jnp.swapaxes(o, 1, 2), GN_EPS).reshape(b, t, hv) * gn_g
        return (jax.nn.silu(g) * o).astype(g.dtype) @ w_out

    log_gamma = jax.nn.log_sigmoid(decay_logit.astype(jnp.float32))
    qc, kc, vc, gc = project(h_ctx, None)
    ql, kl, vl, gl = project(h_lat, (row_pos, col_pos))
    s0 = jnp.zeros((h_lat.shape[0], RET_HEADS, RET_DK, RET_DV), jnp.float32)
    fl = lambda a: jnp.flip(a, axis=2)
    oc_f, sc_f = retention_scan(qc, kc, vc, log_gamma[0], s0)
    oc_b, sc_b = retention_scan(fl(qc), fl(kc), fl(vc), log_gamma[1], s0)
    ol_f, _ = retention_scan(ql, kl, vl, log_gamma[0], sc_f)
    ol_b, _ = retention_scan(fl(ql), fl(kl), fl(vl), log_gamma[1], sc_b)
    o_lat = finish(ol_f + fl(ol_b), gl)
    o_ctx = finish(oc_f + fl(oc_b), gc) if with_ctx_out else None
    return o_ctx, o_lat


def gdn_chunk_scan(q, k, v, g, beta, s0):
    b, h, t, _ = q.shape
    c = DN_CHUNK
    n = t // c
    f32 = jnp.float32
    ch = lambda a: a.astype(f32).reshape(b, h, n, c, *a.shape[3:])
    q, k, v, g, beta = ch(q), ch(k), ch(v), ch(g), ch(beta)
    gc = jnp.cumsum(g, axis=-1)
    idx = jnp.arange(c)
    incl = idx[:, None] >= idx[None, :]
    strict = idx[:, None] > idx[None, :]
    decay = jnp.exp(jnp.where(incl, gc[..., :, None] - gc[..., None, :], -jnp.inf))
    kb = k * beta[..., None]
    a_mat = jnp.where(strict, jnp.einsum('bhnid,bhnjd->bhnij', kb, k) * decay, 0.0) + jnp.eye(c, dtype=f32)
    u = lax.linalg.triangular_solve(a_mat, v * beta[..., None], left_side=True, lower=True, unit_diagonal=True)
    w = lax.linalg.triangular_solve(a_mat, kb * jnp.exp(gc)[..., None], left_side=True, lower=True, unit_diagonal=True)
    qk = jnp.einsum('bhnid,bhnjd->bhnij', q, k) * decay
    qg = q * jnp.exp(gc)[..., None]
    kt = k * jnp.exp(gc[..., -1:] - gc)[..., None]
    tail = jnp.exp(gc[..., -1])[..., None, None]
    sw = lambda a: jnp.moveaxis(a, 2, 0)

    def step(s, inp):
        qg_c, qk_c, u_c, w_c, kt_c, tail_c = inp
        v_new = u_c - jnp.einsum('bhcd,bhdv->bhcv', w_c, s)
        o = jnp.einsum('bhcd,bhdv->bhcv', qg_c, s) + jnp.einsum('bhij,bhjv->bhiv', qk_c, v_new)
        s = s * tail_c + jnp.einsum('bhcd,bhcv->bhdv', kt_c, v_new)
        return s, o

    s, o = lax.scan(step, s0, (sw(qg), sw(qk), sw(u), sw(w), sw(kt), sw(tail)))
    return jnp.moveaxis(o, 0, 2).reshape(b, h, t, -1), s


def deltanet_mixer(h_ctx, h_lat, w_in, conv_w, a_log, dt_bias, norm_g, w_out, with_ctx_out):
    rep = DN_V_HEADS // DN_QK_HEADS

    def project(h):
        b, t, _ = h.shape
        qkv, z, ab = jnp.split(h @ w_in, [2 * DN_QK_W + DN_V_W, 2 * DN_QK_W + 2 * DN_V_W], axis=-1)
        qkv = jax.nn.silu(short_conv(qkv, conv_w))
        q, k, v = jnp.split(qkv, [DN_QK_W, 2 * DN_QK_W], axis=-1)
        q = jnp.repeat(l2_normalize(q.reshape(b, t, DN_QK_HEADS, DN_HEAD_DIM)), rep, axis=2) * (DN_HEAD_DIM ** -0.5)
        k = jnp.repeat(l2_normalize(k.reshape(b, t, DN_QK_HEADS, DN_HEAD_DIM)), rep, axis=2)
        v = v.reshape(b, t, DN_V_HEADS, DN_HEAD_DIM)
        ab = ab.astype(jnp.float32).reshape(b, t, 2, 2, DN_V_HEADS)
        g = -jnp.exp(a_log.astype(jnp.float32)) * jax.nn.softplus(ab[:, :, :, 0] + dt_bias)
        beta = jax.nn.sigmoid(ab[:, :, :, 1])
        bh = lambda a: jnp.swapaxes(a, 1, 2)
        return bh(q), bh(k), bh(v), jnp.moveaxis(g, 1, -1), jnp.moveaxis(beta, 1, -1), z

    def finish(o, z):
        b, _, t, _ = o.shape
        o = rms_norm(jnp.swapaxes(o, 1, 2), RMS_EPS) * norm_g
        o = o * jax.nn.silu(z.reshape(b, t, DN_V_HEADS, DN_HEAD_DIM))
        return o.reshape(b, t, DN_V_W).astype(z.dtype) @ w_out

    qc, kc, vc, gcx, bcx, zc = project(h_ctx)
    ql, kl, vl, glt, blt, zl = project(h_lat)
    s0 = jnp.zeros((h_lat.shape[0], DN_V_HEADS, DN_HEAD_DIM, DN_HEAD_DIM), jnp.float32)
    fl = lambda a: jnp.flip(a, axis=2)
    oc_f, sc_f = gdn_chunk_scan(qc, kc, vc, gcx[:, 0], bcx[:, 0], s0)
    oc_b, sc_b = gdn_chunk_scan(fl(qc), fl(kc), fl(vc), fl(gcx[:, 1]), fl(bcx[:, 1]), s0)
    ol_f, _ = gdn_chunk_scan(ql, kl, vl, glt[:, 0], blt[:, 0], sc_f)
    ol_b, _ = gdn_chunk_scan(fl(ql), fl(kl), fl(vl), fl(glt[:, 1]), fl(blt[:, 1]), sc_b)
    o_lat = finish(ol_f + fl(ol_b), zl)
    o_ctx = finish(oc_f + fl(oc_b), zc) if with_ctx_out else None
    return o_ctx, o_lat


def rwkv7_scan(r, w, k, v, a, b, s0):
    tm = lambda u: jnp.moveaxis(u.astype(jnp.float32), 1, 0)

    def step(s, inp):
        r_t, w_t, k_t, v_t, a_t, b_t = inp
        sa = jnp.einsum('bhvk,bhk->bhv', s, a_t)
        s = s * w_t[:, :, None, :] + sa[..., None] * b_t[:, :, None, :] + v_t[..., None] * k_t[:, :, None, :]
        return s, jnp.einsum('bhvk,bhk->bhv', s, r_t)

    s, o = lax.scan(step, s0, (tm(r), tm(w), tm(k), tm(v), tm(a), tm(b)))
    return jnp.moveaxis(o, 0, 1), s


def rwkv7_mixer(h_ctx, h_lat, mix, w_rkv, w0, w1, w2, a0, a1, a2, g1, g2, k_k, k_a, r_k, lnx_g, w_out, with_ctx_out):
    heads = lambda u: u.reshape(*u.shape[:-1], RWKV_HEADS, RWKV_HEAD)

    def project(h):
        xx = centred_shift(h)
        r = (h + xx * mix[0]) @ w_rkv[0]
        k = (h + xx * mix[1]) @ w_rkv[1]
        v = (h + xx * mix[2]) @ w_rkv[2]
        xw, xa, xg = h + xx * mix[3], h + xx * mix[4], h + xx * mix[5]
        w_logit = w0[:, None, None, :] + jnp.einsum('zbtr,zrc->zbtc', jnp.tanh(jnp.einsum('btc,zcr->zbtr', xw, w1)), w2)
        decay = jnp.exp(-jnp.exp(-jax.nn.softplus(-w_logit.astype(jnp.float32)) - 0.5))
        a = jax.nn.sigmoid(a0[:, None, None, :] + jnp.einsum('zbtr,zrc->zbtc', jnp.einsum('btc,zcr->zbtr', xa, a1), a2))
        g = jax.nn.sigmoid(xg @ g1) @ g2
        kk = l2_normalize(heads(k * k_k))
        k_dir = heads(k[None] * (1.0 + (a - 1.0) * k_a))
        return heads(r), heads(decay), k_dir, heads(v), kk, heads(a), g

    def scan_args(r, dec, kd, v, kk, a, z):
        return r, dec[z], kd[z], v, -kk, kk * a[z]

    def finish(o, r, k_dir, v, g):
        b, t = o.shape[:2]
        bonus = jnp.sum(r[None] * k_dir * r_k, axis=(0, -1))[..., None] * v
        o = group_norm(o, LNX_EPS).reshape(b, t, D_MODEL) * lnx_g + bonus.reshape(b, t, D_MODEL)
        return (o * g).astype(g.dtype) @ w_out

    rc, dc, kdc, vc, kkc, ac, gc = project(h_ctx)
    rl, dl, kdl, vl, kkl, al, gl = project(h_lat)
    s0 = jnp.zeros((h_lat.shape[0], RWKV_HEADS, RWKV_HEAD, RWKV_HEAD), jnp.float32)
    fl = lambda u: jnp.flip(u, axis=1)
    oc_f, sc_f = rwkv7_scan(*scan_args(rc, dc, kdc, vc, kkc, ac, 0), s0)
    oc_b, sc_b = rwkv7_scan(*map(fl, scan_args(rc, dc, kdc, vc, kkc, ac, 1)), s0)
    ol_f, _ = rwkv7_scan(*scan_args(rl, dl, kdl, vl, kkl, al, 0), sc_f)
    ol_b, _ = rwkv7_scan(*map(fl, scan_args(rl, dl, kdl, vl, kkl, al, 1)), sc_b)
    o_lat = finish(ol_f + fl(ol_b), rl, kdl, vl, gl)
    o_ctx = finish(oc_f + fl(oc_b), rc, kdc, vc, gc) if with_ctx_out else None
    return o_ctx, o_lat


def swiglu(h, w_gu, w_down):
    gt, up = jnp.split(h @ w_gu, 2, axis=-1)
    return (jax.nn.silu(gt) * up) @ w_down


def moe_swiglu(h, w_router, w_gu, w_down):
    shp = h.shape
    x = h.reshape(-1, shp[-1])
    n = x.shape[0]
    logits = (x @ w_router).astype(jnp.float32)
    top_logit, top_e = lax.top_k(logits, TOP_K)
    gate = jax.nn.softmax(top_logit, axis=-1)
    flat_e = top_e.reshape(-1)
    flat_tok = jnp.repeat(jnp.arange(n, dtype=jnp.int32), TOP_K)
    flat_gate = gate.reshape(-1)
    order = jnp.argsort(flat_e)
    e_sorted = flat_e[order]
    counts = jnp.bincount(flat_e, length=N_EXPERTS)
    padded = (counts + MOE_BLOCK - 1) // MOE_BLOCK * MOE_BLOCK
    start = jnp.cumsum(counts) - counts
    pend = jnp.cumsum(padded)
    pstart = pend - padded
    slot = pstart[e_sorted] + jnp.arange(n * TOP_K) - start[e_sorted]
    n_slots = (n * TOP_K + MOE_BLOCK - 1) // MOE_BLOCK * MOE_BLOCK + N_EXPERTS * MOE_BLOCK
    n_blocks = n_slots // MOE_BLOCK
    slot_tok = jnp.full((n_slots,), n, jnp.int32).at[slot].set(flat_tok[order])
    slot_gate = jnp.zeros((n_slots,), jnp.float32).at[slot].set(flat_gate[order])
    block_e = jnp.minimum(jnp.sum(jnp.arange(n_blocks)[:, None] * MOE_BLOCK >= pend[None, :], axis=1), N_EXPERTS - 1)
    x_pad = jnp.concatenate([x, jnp.zeros((1, x.shape[-1]), x.dtype)], 0)
    xb = x_pad[slot_tok].reshape(n_blocks, MOE_BLOCK, -1)
    yb = lax.map(lambda args: swiglu(args[0], w_gu[args[1]], w_down[args[1]]), (xb, block_e))
    yb = yb.reshape(n_slots, -1)
    y = jnp.zeros((n + 1, yb.shape[-1]), yb.dtype).at[slot_tok].add(yb * slot_gate[:, None].astype(yb.dtype))
    return y[:n].reshape(shp)


LN_ROWS = 512


def _resid_ln_kernel(x_ref, f_ref, ga_ref, g_ref, b_ref, o_ref):
    y = ALPHA * x_ref[...] + (1.0 + ga_ref[...]) * f_ref[...]
    mu = jnp.mean(y, -1, keepdims=True)
    yc = y - mu
    var = jnp.mean(yc * yc, -1, keepdims=True)
    o_ref[...] = yc * lax.rsqrt(var + LN_EPS) * g_ref[...] + b_ref[...]


def resid_ln(x, f, gate, g, b):
    bsz, t, d = x.shape
    rows = min(LN_ROWS, t)
    gate = jnp.broadcast_to(gate, (bsz, 1, d))
    tok = pl.BlockSpec((None, rows, d), lambda i, j: (i, j, 0))
    vec = pl.BlockSpec((1, d), lambda i, j: (0, 0))
    return pl.pallas_call(
        _resid_ln_kernel,
        grid=(bsz, t // rows),
        in_specs=[tok, tok, pl.BlockSpec((None, 1, d), lambda i, j: (i, 0, 0)), vec, vec],
        out_specs=tok,
        out_shape=jax.ShapeDtypeStruct(x.shape, x.dtype),
        name="resid_ln",
    )(x, f, gate, g.reshape(1, d), b.reshape(1, d))


def kernel(x, c, ctx, c_ctx, mod_w, mod_b, ln_g, ln_b,
           ret_w_in, ret_decay, ret_gn_g, ret_w_out,
           dn_w_in, dn_conv_w, dn_a_log, dn_dt_bias, dn_norm_g, dn_w_out,
           rk_mix, rk_w_rkv, rk_w0, rk_w1, rk_w2, rk_a0, rk_a1, rk_a2, rk_g1, rk_g2,
           rk_k_k, rk_k_a, rk_r_k, rk_lnx_g, rk_w_out,
           ffn_w_gu, ffn_w_down, moe_router, moe_w_gu, moe_w_down):
    t = x.shape[1]
    n_ctx = ctx.shape[1]
    rows = t // GRID_W
    row_pos = jnp.repeat(jnp.arange(rows), GRID_W)
    col_pos = jnp.tile(jnp.arange(GRID_W), rows)
    s_lat = jax.nn.silu(c)
    s_ctx = jax.nn.silu(c_ctx)
    lat, cx = x, ctx
    for i in range(DEPTH):
        last = i == DEPTH - 1
        m_lat = (s_lat @ mod_w[i] + mod_b[i])[:, None, :]
        m_ctx = (s_ctx @ mod_w[i] + mod_b[i])[None, None, :]
        sh1, sc1, ga1, sh2, sc2, ga2 = jnp.split(m_lat, 6, axis=-1)
        csh1, csc1, cga1, csh2, csc2, cga2 = jnp.split(m_ctx, 6, axis=-1)
        h_lat = lat * (1.0 + sc1) + sh1
        h_ctx = cx * (1.0 + csc1) + csh1
        kind, j = i % N_MIXERS, i // N_MIXERS
        if kind == 0:
            o_ctx, o_lat = retention_mixer(h_ctx, h_lat, row_pos, col_pos, ret_w_in[j], ret_decay[j],
                                           ret_gn_g[j], ret_w_out[j], not last)
        elif kind == 1:
            o_ctx, o_lat = deltanet_mixer(h_ctx, h_lat, dn_w_in[j], dn_conv_w[j], dn_a_log[j], dn_dt_bias[j],
                                          dn_norm_g[j], dn_w_out[j], not last)
        else:
            o_ctx, o_lat = rwkv7_mixer(h_ctx, h_lat, rk_mix[j], rk_w_rkv[j], rk_w0[j], rk_w1[j], rk_w2[j],
                                       rk_a0[j], rk_a1[j], rk_a2[j], rk_g1[j], rk_g2[j], rk_k_k[j], rk_k_a[j],
                                       rk_r_k[j], rk_lnx_g[j], rk_w_out[j], not last)
        if i % 2 == 0:
            channel = functools.partial(swiglu, w_gu=ffn_w_gu[i // 2], w_down=ffn_w_down[i // 2])
        else:
            channel = functools.partial(moe_swiglu, w_router=moe_router[i // 2], w_gu=moe_w_gu[i // 2],
                                        w_down=moe_w_down[i // 2])
        lat = resid_ln(lat, o_lat, ga1, ln_g[i, 0], ln_b[i, 0])
        h_lat = lat * (1.0 + sc2) + sh2
        if last:
            lat = resid_ln(lat, channel(h_lat), ga2, ln_g[i, 1], ln_b[i, 1])
        else:
            cx = resid_ln(cx, o_ctx, cga1, ln_g[i, 0], ln_b[i, 0])
            h_ctx = cx * (1.0 + csc2) + csh2
            f = channel(jnp.concatenate([h_ctx, h_lat], axis=1))
            cx = resid_ln(cx, f[:, :n_ctx], cga2, ln_g[i, 1], ln_b[i, 1])
            lat = resid_ln(lat, f[:, n_ctx:], ga2, ln_g[i, 1], ln_b[i, 1])
    return lat
```

```python
import math, functools
import jax
import jax.numpy as jnp
from jax import lax
import numpy as np
from jax.experimental import pallas as pl
from jax.experimental.pallas import tpu as pltpu

D_MODEL = 1024
DEPTH = 4
GRID_W = 64
N_MIXERS = 3
ALPHA = (2 * DEPTH) ** 0.25
LN_EPS = 1e-5
GN_EPS = 1e-5
RMS_EPS = 1e-6
LNX_EPS = 64e-5

RET_HEADS = 4
RET_DK = D_MODEL // RET_HEADS
RET_DV = 2 * RET_DK
RET_CHUNK = 128
ROPE_BASE = 10000.0

DN_QK_HEADS = 8
DN_V_HEADS = 16
DN_HEAD_DIM = 128
DN_CHUNK = 64
DN_QK_W = DN_QK_HEADS * DN_HEAD_DIM
DN_V_W = DN_V_HEADS * DN_HEAD_DIM

RWKV_HEAD = 64
RWKV_HEADS = D_MODEL // RWKV_HEAD

N_EXPERTS = 8
TOP_K = 2
MOE_BLOCK = 256


def group_norm(x, eps):
    xf = x.astype(jnp.float32)
    mu = jnp.mean(xf, -1, keepdims=True)
    var = jnp.mean(jnp.square(xf - mu), -1, keepdims=True)
    return (xf - mu) * lax.rsqrt(var + eps)


def rms_norm(x, eps):
    xf = x.astype(jnp.float32)
    return xf * lax.rsqrt(jnp.mean(xf * xf, -1, keepdims=True) + eps)


def l2_normalize(x, eps=1e-6):
    xf = x.astype(jnp.float32)
    return (xf * lax.rsqrt(jnp.sum(xf * xf, -1, keepdims=True) + eps)).astype(x.dtype)


def rope_2d(x, row_pos, col_pos):
    half = x.shape[-1] // 2
    quarter = half // 2
    inv = ROPE_BASE ** (-jnp.arange(quarter, dtype=jnp.float32) / quarter)

    def rot(xa, pos):
        ang = pos.astype(jnp.float32)[:, None] * inv
        cos = jnp.cos(ang)[None, :, None, :]
        sin = jnp.sin(ang)[None, :, None, :]
        x1, x2 = xa[..., :quarter], xa[..., quarter:]
        return jnp.concatenate([x1 * cos - x2 * sin, x1 * sin + x2 * cos], -1)

    return jnp.concatenate([rot(x[..., :half], row_pos), rot(x[..., half:], col_pos)], -1).astype(x.dtype)


def centred_shift(x):
    p = jnp.pad(x, ((0, 0), (1, 1), (0, 0)))
    return 0.5 * (p[:, :-2] + p[:, 2:]) - x


def short_conv(x, w):
    ch = x.shape[-1]
    pad = (w.shape[0] - 1) // 2
    return lax.conv_general_dilated(x, w.astype(x.dtype)[:, None, :], (1,), [(pad, pad)],
                                    dimension_numbers=('NWC', 'WIO', 'NWC'), feature_group_count=ch)


def retention_scan(q, k, v, log_gamma, s0):
    b, h, t, _ = q.shape
    c = RET_CHUNK
    n = t // c
    pos = jnp.arange(c, dtype=jnp.float32)
    dist = pos[:, None] - pos[None, :]
    lower = dist >= 0
    inner = jnp.where(lower, jnp.exp(jnp.where(lower, dist, 0.0) * log_gamma[:, None, None]), 0.0)
    q_decay = jnp.exp((pos + 1.0) * log_gamma[:, None])[..., None]
    k_decay = jnp.exp((c - 1.0 - pos) * log_gamma[:, None])[..., None]
    chunk_decay = jnp.exp(c * log_gamma)[:, None, None]
    to_chunks = lambda a: jnp.moveaxis(a.reshape(b, h, n, c, a.shape[-1]), 2, 0)

    def step(s, inp):
        qc, kc, vc = inp
        scores = jnp.einsum('bhid,bhjd->bhij', qc, kc) * inner
        o = jnp.einsum('bhij,bhjv->bhiv', scores, vc) + jnp.einsum('bhid,bhdv->bhiv', qc * q_decay, s)
        s = s * chunk_decay + jnp.einsum('bhjd,bhjv->bhdv', kc * k_decay, vc)
        return s, o

    s, o = lax.scan(step, s0, (to_chunks(q), to_chunks(k), to_chunks(v)))
    return jnp.moveaxis(o, 0, 2).reshape(b, h, t, -1), s


def retention_mixer(h_ctx, h_lat, row_pos, col_pos, w_in, decay_logit, gn_g, w_out, with_ctx_out):
    hk = RET_HEADS * RET_DK
    hv = RET_HEADS * RET_DV

    def project(h, pos):
        b, t, _ = h.shape
        q, k, v, g = jnp.split(h @ w_in, [hk, 2 * hk, 2 * hk + hv], axis=-1)
        q = q.reshape(b, t, RET_HEADS, RET_DK)
        k = k.reshape(b, t, RET_HEADS, RET_DK) * (RET_DK ** -0.5)
        if pos is not None:
            q = rope_2d(q, *pos)
            k = rope_2d(k, *pos)
        v = v.reshape(b, t, RET_HEADS, RET_DV)
        bh = lambda a: jnp.swapaxes(a, 1, 2)
        return bh(q), bh(k), bh(v), g

    def finish(o, g):
        b, _, t, _ = o.shape
        o = group_norm(jnp.swapaxes(o, 1, 2), GN_EPS).reshape(b, t, hv) * gn_g
        return (jax.nn.silu(g) * o).astype(g.dtype) @ w_out

    log_gamma = jax.nn.log_sigmoid(decay_logit.astype(jnp.float32))
    qc, kc, vc, gc = project(h_ctx, None)
    ql, kl, vl, gl = project(h_lat, (row_pos, col_pos))
    s0 = jnp.zeros((h_lat.shape[0], RET_HEADS, RET_DK, RET_DV), jnp.float32)
    fl = lambda a: jnp.flip(a, axis=2)
    oc_f, sc_f = retention_scan(qc, kc, vc, log_gamma[0], s0)
    oc_b, sc_b = retention_scan(fl(qc), fl(kc), fl(vc), log_gamma[1], s0)
    ol_f, _ = retention_scan(ql, kl, vl, log_gamma[0], sc_f)
    ol_b, _ = retention_scan(fl(ql), fl(kl), fl(vl), log_gamma[1], sc_b)
    o_lat = finish(ol_f + fl(ol_b), gl)
    o_ctx = finish(oc_f + fl(oc_b), gc) if with_ctx_out else None
    return o_ctx, o_lat


def gdn_chunk_scan(q, k, v, g, beta, s0):
    b, h, t, _ = q.shape
    c = DN_CHUNK
    n = t // c
    f32 = jnp.float32
    ch = lambda a: a.astype(f32).reshape(b, h, n, c, *a.shape[3:])
    q, k, v, g, beta = ch(q), ch(k), ch(v), ch(g), ch(beta)
    gc = jnp.cumsum(g, axis=-1)
    idx = jnp.arange(c)
    incl = idx[:, None] >= idx[None, :]
    strict = idx[:, None] > idx[None, :]
    decay = jnp.exp(jnp.where(incl, gc[..., :, None] - gc[..., None, :], -jnp.inf))
    kb = k * beta[..., None]
    a_mat = jnp.where(strict, jnp.einsum('bhnid,bhnjd->bhnij', kb, k) * decay, 0.0) + jnp.eye(c, dtype=f32)
    u = lax.linalg.triangular_solve(a_mat, v * beta[..., None], left_side=True, lower=True, unit_diagonal=True)
    w = lax.linalg.triangular_solve(a_mat, kb * jnp.exp(gc)[..., None], left_side=True, lower=True, unit_diagonal=True)
    qk = jnp.einsum('bhnid,bhnjd->bhnij', q, k) * decay
    qg = q * jnp.exp(gc)[..., None]
    kt = k * jnp.exp(gc[..., -1:] - gc)[..., None]
    tail = jnp.exp(gc[..., -1])[..., None, None]
    sw = lambda a: jnp.moveaxis(a, 2, 0)

    def step(s, inp):
        qg_c, qk_c, u_c, w_c, kt_c, tail_c = inp
        v_new = u_c - jnp.einsum('bhcd,bhdv->bhcv', w_c, s)
        o = jnp.einsum('bhcd,bhdv->bhcv', qg_c, s) + jnp.einsum('bhij,bhjv->bhiv', qk_c, v_new)
        s = s * tail_c + jnp.einsum('bhcd,bhcv->bhdv', kt_c, v_new)
        return s, o

    s, o = lax.scan(step, s0, (sw(qg), sw(qk), sw(u), sw(w), sw(kt), sw(tail)))
    return jnp.moveaxis(o, 0, 2).reshape(b, h, t, -1), s


def deltanet_mixer(h_ctx, h_lat, w_in, conv_w, a_log, dt_bias, norm_g, w_out, with_ctx_out):
    rep = DN_V_HEADS // DN_QK_HEADS

    def project(h):
        b, t, _ = h.shape
        qkv, z, ab = jnp.split(h @ w_in, [2 * DN_QK_W + DN_V_W, 2 * DN_QK_W + 2 * DN_V_W], axis=-1)
        qkv = jax.nn.silu(short_conv(qkv, conv_w))
        q, k, v = jnp.split(qkv, [DN_QK_W, 2 * DN_QK_W], axis=-1)
        q = jnp.repeat(l2_normalize(q.reshape(b, t, DN_QK_HEADS, DN_HEAD_DIM)), rep, axis=2) * (DN_HEAD_DIM ** -0.5)
        k = jnp.repeat(l2_normalize(k.reshape(b, t, DN_QK_HEADS, DN_HEAD_DIM)), rep, axis=2)
        v = v.reshape(b, t, DN_V_HEADS, DN_HEAD_DIM)
        ab = ab.astype(jnp.float32).reshape(b, t, 2, 2, DN_V_HEADS)
        g = -jnp.exp(a_log.astype(jnp.float32)) * jax.nn.softplus(ab[:, :, :, 0] + dt_bias)
        beta = jax.nn.sigmoid(ab[:, :, :, 1])
        bh = lambda a: jnp.swapaxes(a, 1, 2)
        return bh(q), bh(k), bh(v), jnp.moveaxis(g, 1, -1), jnp.moveaxis(beta, 1, -1), z

    def finish(o, z):
        b, _, t, _ = o.shape
        o = rms_norm(jnp.swapaxes(o, 1, 2), RMS_EPS) * norm_g
        o = o * jax.nn.silu(z.reshape(b, t, DN_V_HEADS, DN_HEAD_DIM))
        return o.reshape(b, t, DN_V_W).astype(z.dtype) @ w_out

    qc, kc, vc, gcx, bcx, zc = project(h_ctx)
    ql, kl, vl, glt, blt, zl = project(h_lat)
    s0 = jnp.zeros((h_lat.shape[0], DN_V_HEADS, DN_HEAD_DIM, DN_HEAD_DIM), jnp.float32)
    fl = lambda a: jnp.flip(a, axis=2)
    oc_f, sc_f = gdn_chunk_scan(qc, kc, vc, gcx[:, 0], bcx[:, 0], s0)
    oc_b, sc_b = gdn_chunk_scan(fl(qc), fl(kc), fl(vc), fl(gcx[:, 1]), fl(bcx[:, 1]), s0)
    ol_f, _ = gdn_chunk_scan(ql, kl, vl, glt[:, 0], blt[:, 0], sc_f)
    ol_b, _ = gdn_chunk_scan(fl(ql), fl(kl), fl(vl), fl(glt[:, 1]), fl(blt[:, 1]), sc_b)
    o_lat = finish(ol_f + fl(ol_b), zl)
    o_ctx = finish(oc_f + fl(oc_b), zc) if with_ctx_out else None
    return o_ctx, o_lat


RWKV_CHUNK = 64
RWKV_NEUMANN_STEPS = int(math.log2(RWKV_CHUNK)) - 1

_NT = (((1,), (1,)), ((), ()))
_TN = (((0,), (0,)), ((), ()))


def _bdot(x, y, dims=None):
    x = x.astype(jnp.bfloat16)
    y = y.astype(jnp.bfloat16)
    if dims is None:
        return jnp.dot(x, y, preferred_element_type=jnp.float32)
    return lax.dot_general(x, y, dims, preferred_element_type=jnp.float32)


def _rwkv_chunk_kernel(r_ref, v_ref, kk_ref, wl_ref, kd_ref, a_ref, s0_ref, o_ref, s_ref,
                       at_s, bt_s, kt_s, rt_s, be_s, ke_s, vb_s, gc_s):
    z = pl.program_id(0)
    i = pl.program_id(2)
    c = r_ref.shape[0]

    @pl.when(i == 0)
    def _():
        s_ref[...] = s0_ref[...]

    row = lax.broadcasted_iota(jnp.int32, (c, c), 0)
    col = lax.broadcasted_iota(jnp.int32, (c, c), 1)
    lag = (row - col) * (1 - 2 * z)
    incl = lag >= 0
    strict = lag > 0

    wl = wl_ref[...]
    softplus_neg = jnp.maximum(-wl, 0.0) + jnp.log(1.0 + jnp.exp(-jnp.abs(wl)))
    logw = -jnp.exp(-softplus_neg - 0.5)
    cum = jnp.dot(incl.astype(jnp.float32), logw, precision=lax.Precision.HIGHEST,
                  preferred_element_type=jnp.float32)
    c_last = jnp.sum(logw, axis=0, keepdims=True)
    kk = kk_ref[...]
    kb = kk * a_ref[...]
    kd = kd_ref[...]
    g_inv = jnp.exp(-cum)
    e_end = jnp.exp(c_last - cum)
    at_s[...] = (-kk * jnp.exp(cum - logw)).astype(jnp.bfloat16)
    bt_s[...] = (kb * g_inv).astype(jnp.bfloat16)
    kt_s[...] = (kd * g_inv).astype(jnp.bfloat16)
    rt_s[...] = (r_ref[...] * jnp.exp(cum)).astype(jnp.bfloat16)
    be_s[...] = (kb * e_end).astype(jnp.bfloat16)
    ke_s[...] = (kd * e_end).astype(jnp.bfloat16)
    vb_s[...] = v_ref[...].astype(jnp.bfloat16)
    gc_s[...] = jnp.exp(c_last)

    eye = (row == col).astype(jnp.float32)
    for h in range(RWKV_HEADS):
        hs = slice(h * RWKV_HEAD, (h + 1) * RWKV_HEAD)
        at, bt, kt, rt, vv = at_s[:, hs], bt_s[:, hs], kt_s[:, hs], rt_s[:, hs], vb_s[:, hs]
        s0 = s_ref[h]
        a_ab = jnp.where(strict, _bdot(at, bt, _NT), 0.0)
        a_ak = jnp.where(strict, _bdot(at, kt, _NT), 0.0)
        m_rb = jnp.where(incl, _bdot(rt, bt, _NT), 0.0)
        m_rk = jnp.where(incl, _bdot(rt, kt, _NT), 0.0)
        tm = eye + a_ab
        p = a_ab
        for _ in range(RWKV_NEUMANN_STEPS):
            p = _bdot(p, p)
            tm = tm + _bdot(tm, p)
        u = _bdot(tm, _bdot(at, s0, _NT) + _bdot(a_ak, vv))
        o_ref[:, hs] = _bdot(rt, s0, _NT) + _bdot(m_rb, u) + _bdot(m_rk, vv)
        s_ref[h] = s0 * gc_s[:, hs] + _bdot(u, be_s[:, hs], _TN) + _bdot(vv, ke_s[:, hs], _TN)


def rwkv_scan(r, v, kk, wl, kd, a, s0):
    b, t, d = r.shape
    c = RWKV_CHUNK
    n = t // c
    chunk = lambda z, i: i + z * (n - 1 - 2 * i)
    shared = pl.BlockSpec((None, c, d), lambda z, bi, i: (bi, chunk(z, i), 0))
    perdir = pl.BlockSpec((None, None, c, d), lambda z, bi, i: (z, bi, chunk(z, i), 0))
    state = pl.BlockSpec((None, None, RWKV_HEADS, RWKV_HEAD, RWKV_HEAD), lambda z, bi, i: (z, bi, 0, 0, 0))
    bf = lambda: pltpu.VMEM((c, d), jnp.bfloat16)
    return pl.pallas_call(
        _rwkv_chunk_kernel,
        grid=(2, b, n),
        in_specs=[shared, shared, shared, perdir, perdir, perdir, state],
        out_specs=[perdir, state],
        out_shape=[jax.ShapeDtypeStruct((2, b, t, d), jnp.float32),
                   jax.ShapeDtypeStruct(s0.shape, jnp.float32)],
        scratch_shapes=[bf(), bf(), bf(), bf(), bf(), bf(), bf(), pltpu.VMEM((1, d), jnp.float32)],
        compiler_params=pltpu.CompilerParams(dimension_semantics=("arbitrary", "arbitrary", "arbitrary")),
        name="rwkv_scan",
    )(r, v, kk, wl, kd, a, s0)


def rwkv7_mixer(h_ctx, h_lat, mix, w_rkv, w0, w1, w2, a0, a1, a2, g1, g2, k_k, k_a, r_k, lnx_g, w_out, with_ctx_out):
    heads = lambda u: u.reshape(*u.shape[:-1], RWKV_HEADS, RWKV_HEAD)

    def project(h):
        xx = centred_shift(h)
        r = (h + xx * mix[0]) @ w_rkv[0]
        k = (h + xx * mix[1]) @ w_rkv[1]
        v = (h + xx * mix[2]) @ w_rkv[2]
        xw, xa, xg = h + xx * mix[3], h + xx * mix[4], h + xx * mix[5]
        w_logit = w0[:, None, None, :] + jnp.einsum('zbtr,zrc->zbtc', jnp.tanh(jnp.einsum('btc,zcr->zbtr', xw, w1)), w2)
        a = jax.nn.sigmoid(a0[:, None, None, :] + jnp.einsum('zbtr,zrc->zbtc', jnp.einsum('btc,zcr->zbtr', xa, a1), a2))
        g = jax.nn.sigmoid(xg @ g1) @ g2
        kk = l2_normalize(heads(k * k_k)).reshape(k.shape)
        k_dir = k[None] * (1.0 + (a - 1.0) * k_a)
        return r, w_logit.astype(jnp.float32), k_dir, v, kk, a, g

    def finish(o, r, k_dir, v, g):
        b, t = r.shape[:2]
        bonus = jnp.sum(heads(r)[None] * heads(k_dir) * r_k, axis=(0, -1))[..., None] * heads(v)
        o = group_norm(heads(o[0] + o[1]), LNX_EPS).reshape(b, t, D_MODEL) * lnx_g + bonus.reshape(b, t, D_MODEL)
        return (o * g).astype(g.dtype) @ w_out

    rc, wlc, kdc, vc, kkc, ac, gc = project(h_ctx)
    rl, wll, kdl, vl, kkl, al, gl = project(h_lat)
    s0 = jnp.zeros((2, h_lat.shape[0], RWKV_HEADS, RWKV_HEAD, RWKV_HEAD), jnp.float32)
    oc, sc = rwkv_scan(rc, vc, kkc, wlc, kdc, ac, s0)
    ol, _ = rwkv_scan(rl, vl, kkl, wll, kdl, al, sc)
    o_lat = finish(ol, rl, kdl, vl, gl)
    o_ctx = finish(oc, rc, kdc, vc, gc) if with_ctx_out else None
    return o_ctx, o_lat


def swiglu(h, w_gu, w_down):
    gt, up = jnp.split(h @ w_gu, 2, axis=-1)
    return (jax.nn.silu(gt) * up) @ w_down


def moe_swiglu(h, w_router, w_gu, w_down):
    shp = h.shape
    x = h.reshape(-1, shp[-1])
    n = x.shape[0]
    logits = (x @ w_router).astype(jnp.float32)
    top_logit, top_e = lax.top_k(logits, TOP_K)
    gate = jax.nn.softmax(top_logit, axis=-1)
    flat_e = top_e.reshape(-1)
    flat_tok = jnp.repeat(jnp.arange(n, dtype=jnp.int32), TOP_K)
    flat_gate = gate.reshape(-1)
    order = jnp.argsort(flat_e)
    e_sorted = flat_e[order]
    counts = jnp.bincount(flat_e, length=N_EXPERTS)
    padded = (counts + MOE_BLOCK - 1) // MOE_BLOCK * MOE_BLOCK
    start = jnp.cumsum(counts) - counts
    pend = jnp.cumsum(padded)
    pstart = pend - padded
    slot = pstart[e_sorted] + jnp.arange(n * TOP_K) - start[e_sorted]
    n_slots = (n * TOP_K + MOE_BLOCK - 1) // MOE_BLOCK * MOE_BLOCK + N_EXPERTS * MOE_BLOCK
    n_blocks = n_slots // MOE_BLOCK
    slot_tok = jnp.full((n_slots,), n, jnp.int32).at[slot].set(flat_tok[order])
    slot_gate = jnp.zeros((n_slots,), jnp.float32).at[slot].set(flat_gate[order])
    block_e = jnp.minimum(jnp.sum(jnp.arange(n_blocks)[:, None] * MOE_BLOCK >= pend[None, :], axis=1), N_EXPERTS - 1)
    x_pad = jnp.concatenate([x, jnp.zeros((1, x.shape[-1]), x.dtype)], 0)
    xb = x_pad[slot_tok].reshape(n_blocks, MOE_BLOCK, -1)
    yb = lax.map(lambda args: swiglu(args[0], w_gu[args[1]], w_down[args[1]]), (xb, block_e))
    yb = yb.reshape(n_slots, -1)
    y = jnp.zeros((n + 1, yb.shape[-1]), yb.dtype).at[slot_tok].add(yb * slot_gate[:, None].astype(yb.dtype))
    return y[:n].reshape(shp)


LN_ROWS = 512


def _resid_ln_kernel(x_ref, f_ref, ga_ref, g_ref, b_ref, o_ref):
    y = ALPHA * x_ref[...] + (1.0 + ga_ref[...]) * f_ref[...]
    mu = jnp.mean(y, -1, keepdims=True)
    yc = y - mu
    var = jnp.mean(yc * yc, -1, keepdims=True)
    o_ref[...] = yc * lax.rsqrt(var + LN_EPS) * g_ref[...] + b_ref[...]


def resid_ln(x, f, gate, g, b):
    bsz, t, d = x.shape
    rows = min(LN_ROWS, t)
    gate = jnp.broadcast_to(gate, (bsz, 1, d))
    tok = pl.BlockSpec((None, rows, d), lambda i, j: (i, j, 0))
    vec = pl.BlockSpec((1, d), lambda i, j: (0, 0))
    return pl.pallas_call(
        _resid_ln_kernel,
        grid=(bsz, t // rows),
        in_specs=[tok, tok, pl.BlockSpec((None, 1, d), lambda i, j: (i, 0, 0)), vec, vec],
        out_specs=tok,
        out_shape=jax.ShapeDtypeStruct(x.shape, x.dtype),
        name="resid_ln",
    )(x, f, gate, g.reshape(1, d), b.reshape(1, d))


def kernel(x, c, ctx, c_ctx, mod_w, mod_b, ln_g, ln_b,
           ret_w_in, ret_decay, ret_gn_g, ret_w_out,
           dn_w_in, dn_conv_w, dn_a_log, dn_dt_bias, dn_norm_g, dn_w_out,
           rk_mix, rk_w_rkv, rk_w0, rk_w1, rk_w2, rk_a0, rk_a1, rk_a2, rk_g1, rk_g2,
           rk_k_k, rk_k_a, rk_r_k, rk_lnx_g, rk_w_out,
           ffn_w_gu, ffn_w_down, moe_router, moe_w_gu, moe_w_down):
    t = x.shape[1]
    n_ctx = ctx.shape[1]
    rows = t // GRID_W
    row_pos = jnp.repeat(jnp.arange(rows), GRID_W)
    col_pos = jnp.tile(jnp.arange(GRID_W), rows)
    s_lat = jax.nn.silu(c)
    s_ctx = jax.nn.silu(c_ctx)
    lat, cx = x, ctx
    for i in range(DEPTH):
        last = i == DEPTH - 1
        m_lat = (s_lat @ mod_w[i] + mod_b[i])[:, None, :]
        m_ctx = (s_ctx @ mod_w[i] + mod_b[i])[None, None, :]
        sh1, sc1, ga1, sh2, sc2, ga2 = jnp.split(m_lat, 6, axis=-1)
        csh1, csc1, cga1, csh2, csc2, cga2 = jnp.split(m_ctx, 6, axis=-1)
        h_lat = lat * (1.0 + sc1) + sh1
        h_ctx = cx * (1.0 + csc1) + csh1
        kind, j = i % N_MIXERS, i // N_MIXERS
        if kind == 0:
            o_ctx, o_lat = retention_mixer(h_ctx, h_lat, row_pos, col_pos, ret_w_in[j], ret_decay[j],
                                           ret_gn_g[j], ret_w_out[j], not last)
        elif kind == 1:
            o_ctx, o_lat = deltanet_mixer(h_ctx, h_lat, dn_w_in[j], dn_conv_w[j], dn_a_log[j], dn_dt_bias[j],
                                          dn_norm_g[j], dn_w_out[j], not last)
        else:
            o_ctx, o_lat = rwkv7_mixer(h_ctx, h_lat, rk_mix[j], rk_w_rkv[j], rk_w0[j], rk_w1[j], rk_w2[j],
                                       rk_a0[j], rk_a1[j], rk_a2[j], rk_g1[j], rk_g2[j], rk_k_k[j], rk_k_a[j],
                                       rk_r_k[j], rk_lnx_g[j], rk_w_out[j], not last)
        if i % 2 == 0:
            channel = functools.partial(swiglu, w_gu=ffn_w_gu[i // 2], w_down=ffn_w_down[i // 2])
        else:
            channel = functools.partial(moe_swiglu, w_router=moe_router[i // 2], w_gu=moe_w_gu[i // 2],
                                        w_down=moe_w_down[i // 2])
        lat = resid_ln(lat, o_lat, ga1, ln_g[i, 0], ln_b[i, 0])
        h_lat = lat * (1.0 + sc2) + sh2
        if last:
            lat = resid_ln(lat, channel(h_lat), ga2, ln_g[i, 1], ln_b[i, 1])
        else:
            cx = resid_ln(cx, o_ctx, cga1, ln_g[i, 0], ln_b[i, 0])
            h_ctx = cx * (1.0 + csc2) + csh2
            f = channel(jnp.concatenate([h_ctx, h_lat], axis=1))
            cx = resid_ln(cx, f[:, :n_ctx], cga2, ln_g[i, 1], ln_b[i, 1])
            lat = resid_ln(lat, f[:, n_ctx:], ga2, ln_g[i, 1], ln_b[i, 1])
    return lat
```

```python
import math, functools
import jax
import jax.numpy as jnp
from jax import lax
import numpy as np
from jax.experimental import pallas as pl
from jax.experimental.pallas import tpu as pltpu

D_MODEL = 1024
DEPTH = 4
GRID_W = 64
N_MIXERS = 3
ALPHA = (2 * DEPTH) ** 0.25
LN_EPS = 1e-5
GN_EPS = 1e-5
RMS_EPS = 1e-6
LNX_EPS = 64e-5
L2_EPS = 1e-6

RET_HEADS = 4
RET_DK = D_MODEL // RET_HEADS
RET_DV = 2 * RET_DK
RET_CHUNK = 128
ROPE_BASE = 10000.0

DN_QK_HEADS = 8
DN_V_HEADS = 16
DN_HEAD_DIM = 128
DN_CHUNK = 64
DN_CONV = 5
DN_QK_W = DN_QK_HEADS * DN_HEAD_DIM
DN_V_W = DN_V_HEADS * DN_HEAD_DIM
DN_HALO = 16

RWKV_HEAD = 64
RWKV_HEADS = D_MODEL // RWKV_HEAD
RWKV_CHUNK = 64

FFN_DIM = 2816
N_EXPERTS = 8
TOP_K = 2
EXPERT_DIM = 3584
MOE_ROWS = 512
MOE_F_CHUNKS = 4

ROW_TILE = 512
VMEM_LIMIT = 48 * 1024 * 1024

BF16 = jnp.bfloat16
F32 = jnp.float32
_NT = (((1,), (1,)), ((), ()))
_TN = (((0,), (0,)), ((), ()))


def _bdot(x, y, dims=None):
    x = x.astype(BF16)
    y = y.astype(BF16)
    if dims is None:
        return jnp.dot(x, y, preferred_element_type=F32)
    return lax.dot_general(x, y, dims, preferred_element_type=F32)


def _hdot(x, y):
    return jnp.dot(x, y, precision=lax.Precision.HIGHEST, preferred_element_type=F32)


def _sigmoid(x):
    return 1.0 / (1.0 + jnp.exp(-x))


def _softplus(x):
    return jnp.maximum(x, 0.0) + jnp.log(1.0 + jnp.exp(-jnp.abs(x)))


def _params(*sem):
    return pltpu.CompilerParams(dimension_semantics=sem, vmem_limit_bytes=VMEM_LIMIT)


class Layout:
    def __init__(self, batch, n_ctx, seq):
        self.batch, self.n_ctx, self.seq = batch, n_ctx, seq
        self.ctx_tok = batch * n_ctx
        self.n_tok = self.ctx_tok + batch * seq
        self.row_tile = math.gcd(ROW_TILE, n_ctx * batch, seq)

    def mod_index(self, tile, j):
        r0 = j * tile
        return jnp.where(r0 < self.ctx_tok, 0, 1 + (r0 - self.ctx_tok) // self.seq)

    def seq_chunk(self, chunk, backward, i):
        nc, nl = self.n_ctx // chunk, self.seq // chunk
        if not backward:
            return i
        return jnp.where(i < nc, nc - 1 - i, 2 * nc + nl - 1 - i)

    def row_block(self, chunk, b, sc):
        nc, nl = self.n_ctx // chunk, self.seq // chunk
        return jnp.where(sc < nc, b * nc + sc, self.batch * nc + b * nl + sc - nc)

    def n_chunks(self, chunk):
        return (self.n_ctx + self.seq) // chunk


def _proj_kernel(h_ref, w_ref, o_ref):
    o_ref[...] = _bdot(h_ref[...], w_ref[...]).astype(o_ref.dtype)


def proj(h, w, out_dtype, tn):
    n_tok, k = h.shape
    n = w.shape[1]
    tm = math.gcd(1024, n_tok)
    return pl.pallas_call(
        _proj_kernel,
        grid=(n // tn, n_tok // tm),
        in_specs=[pl.BlockSpec((tm, k), lambda c, j: (j, 0)), pl.BlockSpec((k, tn), lambda c, j: (0, c))],
        out_specs=pl.BlockSpec((tm, tn), lambda c, j: (j, c)),
        out_shape=jax.ShapeDtypeStruct((n_tok, n), out_dtype),
        compiler_params=_params("arbitrary", "arbitrary"),
        name="proj",
    )(h, w)


def _swiglu_in_kernel(h_ref, wg_ref, wu_ref, o_ref):
    h = h_ref[...]
    g = _bdot(h, wg_ref[...])
    u = _bdot(h, wu_ref[...])
    o_ref[...] = (g * _sigmoid(g) * u).astype(o_ref.dtype)


def swiglu_in(h, w_gu, tn):
    n_tok, k = h.shape
    f = w_gu.shape[1] // 2
    tm = math.gcd(1024, n_tok)
    nf = f // tn
    return pl.pallas_call(
        _swiglu_in_kernel,
        grid=(nf, n_tok // tm),
        in_specs=[pl.BlockSpec((tm, k), lambda c, j: (j, 0)),
                  pl.BlockSpec((k, tn), lambda c, j: (0, c)),
                  pl.BlockSpec((k, tn), lambda c, j: (0, c + nf))],
        out_specs=pl.BlockSpec((tm, tn), lambda c, j: (j, c)),
        out_shape=jax.ShapeDtypeStruct((n_tok, f), BF16),
        compiler_params=_params("arbitrary", "arbitrary"),
        name="swiglu_in",
    )(h, w_gu, w_gu)


def _deepnorm_epilogue(x, f, ga_ref, g_ref, b_ref, sc_ref, sh_ref, x_out, h_out):
    y = ALPHA * x + (1.0 + ga_ref[...]) * f
    mu = jnp.mean(y, -1, keepdims=True)
    yc = y - mu
    var = jnp.mean(yc * yc, -1, keepdims=True)
    xn = yc * lax.rsqrt(var + LN_EPS) * g_ref[...] + b_ref[...]
    x_out[...] = xn
    h_out[...] = (xn * (1.0 + sc_ref[...]) + sh_ref[...]).astype(h_out.dtype)


def _out_ln_kernel(a_ref, w_ref, x_ref, ga_ref, g_ref, b_ref, sc_ref, sh_ref, x_out, h_out):
    _deepnorm_epilogue(x_ref[...], _bdot(a_ref[...], w_ref[...]), ga_ref, g_ref, b_ref, sc_ref, sh_ref, x_out, h_out)


def _resid_ln_kernel(f_ref, x_ref, ga_ref, g_ref, b_ref, sc_ref, sh_ref, x_out, h_out):
    _deepnorm_epilogue(x_ref[...], f_ref[...], ga_ref, g_ref, b_ref, sc_ref, sh_ref, x_out, h_out)


def out_ln(lay, a, w, x, gate, ln_g, ln_b, sc_next, sh_next):
    n_tok, d = x.shape
    tm = lay.row_tile
    row = lambda j: (j, 0)
    mod = pl.BlockSpec((None, 1, d), lambda j: (lay.mod_index(tm, j), 0, 0))
    vec = pl.BlockSpec((1, d), lambda j: (0, 0))
    tok = pl.BlockSpec((tm, d), row)
    if w is None:
        body, lhs, lhs_specs = _resid_ln_kernel, (a,), [tok]
    else:
        k = a.shape[1]
        body, lhs = _out_ln_kernel, (a, w)
        lhs_specs = [pl.BlockSpec((tm, k), row), pl.BlockSpec((k, d), lambda j: (0, 0))]
    return pl.pallas_call(
        body,
        grid=(n_tok // tm,),
        in_specs=lhs_specs + [tok, mod, vec, vec, mod, mod],
        out_specs=[tok, tok],
        out_shape=[jax.ShapeDtypeStruct((n_tok, d), F32), jax.ShapeDtypeStruct((n_tok, d), BF16)],
        compiler_params=_params("arbitrary"),
        name="out_ln",
    )(*lhs, x, gate, ln_g.reshape(1, d), ln_b.reshape(1, d), sc_next, sh_next)


def _modulate_kernel(x_ref, sc_ref, sh_ref, h_out):
    h_out[...] = (x_ref[...] * (1.0 + sc_ref[...]) + sh_ref[...]).astype(h_out.dtype)


def modulate(lay, x, sc, sh):
    n_tok, d = x.shape
    tm = lay.row_tile
    mod = pl.BlockSpec((None, 1, d), lambda j: (lay.mod_index(tm, j), 0, 0))
    tok = pl.BlockSpec((tm, d), lambda j: (j, 0))
    return pl.pallas_call(
        _modulate_kernel, grid=(n_tok // tm,), in_specs=[tok, mod, mod], out_specs=tok,
        out_shape=jax.ShapeDtypeStruct((n_tok, d), BF16), compiler_params=_params("arbitrary"), name="modulate",
    )(x, sc, sh)


def _mod_kernel(s_ref, w_ref, b_ref, o_ref):
    o_ref[...] = _bdot(s_ref[...], w_ref[...]) + b_ref[...]


def modulation_rows(s, mod_w, mod_b):
    r, d = s.shape
    depth, _, n = mod_w.shape
    tn = 1024
    return pl.pallas_call(
        _mod_kernel,
        grid=(depth, n // tn),
        in_specs=[pl.BlockSpec((r, d), lambda i, c: (0, 0)),
                  pl.BlockSpec((None, d, tn), lambda i, c: (i, 0, c)),
                  pl.BlockSpec((None, 1, tn), lambda i, c: (i, 0, c))],
        out_specs=pl.BlockSpec((None, r, tn), lambda i, c: (i, 0, c)),
        out_shape=jax.ShapeDtypeStruct((depth, r, n), F32),
        compiler_params=_params("arbitrary", "arbitrary"),
        name="modulation_rows",
    )(s, mod_w, mod_b.reshape(depth, 1, n))


def _rope(x, cos, sin):
    half = x.shape[1] // 2
    parts = []
    for p in range(2):
        xs = x[:, p * half:(p + 1) * half]
        parts.append(xs * cos[:, p * half:(p + 1) * half]
                     + pltpu.roll(xs, half // 2, axis=1) * sin[:, p * half:(p + 1) * half])
    return jnp.concatenate(parts, axis=1)


def _ret_fwd_kernel(lg_ref, q_ref, k_ref, v_ref, cos_ref, sin_ref, o_ref, s_ref):
    h = pl.program_id(1)
    i = pl.program_id(2)
    c = q_ref.shape[0]

    @pl.when(i == 0)
    def _():
        s_ref[...] = jnp.zeros_like(s_ref)

    lg_f = lg_ref[0, h]
    lg_b = lg_ref[1, h]
    cos, sin = cos_ref[...], sin_ref[...]
    q = _rope(q_ref[...].astype(F32), cos, sin)
    k = _rope(k_ref[...].astype(F32), cos, sin) * (RET_DK ** -0.5)
    v = v_ref[...]
    row = lax.broadcasted_iota(jnp.int32, (c, c), 0)
    col = lax.broadcasted_iota(jnp.int32, (c, c), 1)
    lag = (row - col).astype(F32)
    decay = (jnp.where(lag >= 0, jnp.exp(jnp.maximum(lag, 0.0) * lg_f), 0.0)
             + jnp.where(lag <= 0, jnp.exp(jnp.maximum(-lag, 0.0) * lg_b), 0.0))
    pos = lax.broadcasted_iota(jnp.int32, q.shape, 0).astype(F32)
    s = s_ref[...]
    scores = _bdot(q, k, _NT) * decay
    o_ref[...] = _bdot(scores, v) + _bdot(q * jnp.exp((pos + 1.0) * lg_f), s)
    chunk_decay = jnp.exp(jnp.full((1, 1), c, F32) * lg_f)
    s_ref[...] = s * chunk_decay + _bdot(k * jnp.exp((c - 1.0 - pos) * lg_f), v, _TN)


def _ret_bwd_kernel(lg_ref, q_ref, k_ref, v_ref, g_ref, cos_ref, sin_ref, op_ref, gn_ref, o_ref, s_ref):
    h = pl.program_id(1)
    i = pl.program_id(2)
    c = q_ref.shape[0]

    @pl.when(i == 0)
    def _():
        s_ref[...] = jnp.zeros_like(s_ref)

    lg_b = lg_ref[1, h]
    cos, sin = cos_ref[...], sin_ref[...]
    q = _rope(q_ref[...].astype(F32), cos, sin)
    k = _rope(k_ref[...].astype(F32), cos, sin) * (RET_DK ** -0.5)
    v = v_ref[...]
    pos = lax.broadcasted_iota(jnp.int32, q.shape, 0).astype(F32)
    s = s_ref[...]
    o = op_ref[...] + _bdot(q * jnp.exp((c - pos) * lg_b), s)
    chunk_decay = jnp.exp(jnp.full((1, 1), c, F32) * lg_b)
    s_ref[...] = s * chunk_decay + _bdot(k * jnp.exp(pos * lg_b), v, _TN)
    mu = jnp.mean(o, -1, keepdims=True)
    oc = o - mu
    var = jnp.mean(oc * oc, -1, keepdims=True)
    g = g_ref[...].astype(F32)
    o_ref[...] = (g * _sigmoid(g) * (oc * lax.rsqrt(var + GN_EPS) * gn_ref[...])).astype(o_ref.dtype)


def retention_mix(lay, p, log_gamma, gn_g, cos_t, sin_t):
    c = RET_CHUNK
    n = lay.n_chunks(c)
    n_tok = p.shape[0]
    hv = RET_HEADS * RET_DV
    k_off, v_off, g_off = RET_HEADS, (2 * RET_HEADS * RET_DK) // RET_DV, (2 * RET_HEADS * RET_DK) // RET_DV + RET_HEADS

    def specs(backward):
        sc = lambda i: lay.seq_chunk(c, backward, i)
        rb = lambda b, i: lay.row_block(c, b, sc(i))
        blk = lambda w, off: pl.BlockSpec((c, w), lambda b, h, i: (rb(b, i), off + h))
        tab = pl.BlockSpec((c, RET_DK), lambda b, h, i: (sc(i), 0))
        return blk, tab

    smem = pl.BlockSpec(memory_space=pltpu.SMEM)
    grid = (lay.batch, RET_HEADS, n)
    state = [pltpu.VMEM((RET_DK, RET_DV), F32)]
    blk, tab = specs(False)
    o_part = pl.pallas_call(
        _ret_fwd_kernel, grid=grid,
        in_specs=[smem, blk(RET_DK, 0), blk(RET_DK, k_off), blk(RET_DV, v_off), tab, tab],
        out_specs=blk(RET_DV, 0),
        out_shape=jax.ShapeDtypeStruct((n_tok, hv), F32),
        scratch_shapes=state, compiler_params=_params("arbitrary", "arbitrary", "arbitrary"), name="ret_fwd",
    )(log_gamma, p, p, p, cos_t, sin_t)
    blk, tab = specs(True)
    return pl.pallas_call(
        _ret_bwd_kernel, grid=grid,
        in_specs=[smem, blk(RET_DK, 0), blk(RET_DK, k_off), blk(RET_DV, v_off), blk(RET_DV, g_off), tab, tab,
                  blk(RET_DV, 0), pl.BlockSpec((1, RET_DV), lambda b, h, i: (0, h))],
        out_specs=blk(RET_DV, 0),
        out_shape=jax.ShapeDtypeStruct((n_tok, hv), BF16),
        scratch_shapes=state, compiler_params=_params("arbitrary", "arbitrary", "arbitrary"), name="ret_bwd",
    )(log_gamma, p, p, p, p, cos_t, sin_t, o_part, gn_g.reshape(1, hv))


def rope_tables(lay):
    quarter = RET_DK // 4
    t = jnp.arange(lay.seq)
    inv = ROPE_BASE ** (-jnp.arange(quarter, dtype=F32) / quarter)
    ang_r = (t // GRID_W).astype(F32)[:, None] * inv
    ang_c = (t % GRID_W).astype(F32)[:, None] * inv
    cos = jnp.concatenate([jnp.cos(ang_r)] * 2 + [jnp.cos(ang_c)] * 2, -1)
    sin = jnp.concatenate([-jnp.sin(ang_r), jnp.sin(ang_r), -jnp.sin(ang_c), jnp.sin(ang_c)], -1)
    cos = jnp.concatenate([jnp.ones((lay.n_ctx, RET_DK), F32), cos], 0)
    sin = jnp.concatenate([jnp.zeros((lay.n_ctx, RET_DK), F32), sin], 0)
    return cos, sin


def _dn_prep_kernel(lay, tm, prev_ref, x_ref, next_ref, w_ref, o_ref, xe_s):
    j = pl.program_id(0)
    ct = pl.program_id(1)
    r0 = j * tm
    in_ctx = r0 < lay.ctx_tok
    seq_len = jnp.where(in_ctx, lay.n_ctx, lay.seq)
    off = jnp.where(in_ctx, r0, r0 - lay.ctx_tok) % seq_len
    first = off == 0
    last = off + tm == seq_len
    hal = DN_HALO
    xe_s[0:hal, :] = jnp.where(first, 0.0, prev_ref[...].astype(F32))
    xe_s[hal:hal + tm, :] = x_ref[...].astype(F32)
    xe_s[hal + tm:, :] = jnp.where(last, 0.0, next_ref[...].astype(F32))
    pad = (DN_CONV - 1) // 2
    acc = xe_s[pl.ds(hal - pad, tm), :] * w_ref[0:1, :]
    for d in range(1, DN_CONV):
        acc = acc + xe_s[pl.ds(hal - pad + d, tm), :] * w_ref[d:d + 1, :]
    y = acc * _sigmoid(acc)
    n_qk_tiles = 2 * DN_QK_W // x_ref.shape[1]

    @pl.when(ct >= n_qk_tiles)
    def _():
        o_ref[...] = y.astype(o_ref.dtype)

    @pl.when(ct < n_qk_tiles)
    def _():
        is_q = ct * x_ref.shape[1] < DN_QK_W
        scale = jnp.where(is_q, DN_HEAD_DIM ** -0.5, 1.0)
        for s in range(x_ref.shape[1] // DN_HEAD_DIM):
            ys = y[:, s * DN_HEAD_DIM:(s + 1) * DN_HEAD_DIM]
            inv = lax.rsqrt(jnp.sum(ys * ys, -1, keepdims=True) + L2_EPS) * scale
            o_ref[:, s * DN_HEAD_DIM:(s + 1) * DN_HEAD_DIM] = (ys * inv).astype(o_ref.dtype)


def dn_prep(lay, p, conv_w):
    n_tok = p.shape[0]
    w = 2 * DN_QK_W + DN_V_W
    tm = math.gcd(256, lay.n_ctx, lay.seq)
    tc = 1024
    hb = tm // DN_HALO
    last_hb = n_tok // DN_HALO - 1
    return pl.pallas_call(
        functools.partial(_dn_prep_kernel, lay, tm),
        grid=(n_tok // tm, w // tc),
        in_specs=[pl.BlockSpec((DN_HALO, tc), lambda j, c: (jnp.maximum(j * hb - 1, 0), c)),
                  pl.BlockSpec((tm, tc), lambda j, c: (j, c)),
                  pl.BlockSpec((DN_HALO, tc), lambda j, c: (jnp.minimum((j + 1) * hb, last_hb), c)),
                  pl.BlockSpec((DN_CONV, tc), lambda j, c: (0, c))],
        out_specs=pl.BlockSpec((tm, tc), lambda j, c: (j, c)),
        out_shape=jax.ShapeDtypeStruct((n_tok, w), BF16),
        scratch_shapes=[pltpu.VMEM((tm + 2 * DN_HALO, tc), F32)],
        compiler_params=_params("arbitrary", "arbitrary"), name="dn_prep",
    )(p, p, p, conv_w)


def _dn_scan_kernel(backward, q_ref, k_ref, v_ref, ab_ref, na_row, dt_row, *rest):
    if backward:
        of_ref, z_ref, ng_ref, o_ref, s_ref = rest
    else:
        o_ref, s_ref = rest
    i = pl.program_id(1)
    c = q_ref.shape[0]
    z = 1 if backward else 0
    nh = DN_V_HEADS
    hd = DN_HEAD_DIM
    rep = DN_V_HEADS // DN_QK_HEADS

    @pl.when(i == 0)
    def _():
        s_ref[...] = jnp.zeros_like(s_ref)

    row = lax.broadcasted_iota(jnp.int32, (c, c), 0)
    col = lax.broadcasted_iota(jnp.int32, (c, c), 1)
    lag = (col - row) if backward else (row - col)
    incl = lag >= 0
    strict = lag > 0
    eye = (row == col).astype(F32)
    ab = ab_ref[...]
    g_cols = na_row[...] * _softplus(ab + dt_row[...])
    beta_cols = _sigmoid(ab)
    gc_cols = _hdot(incl.astype(F32), g_cols)
    gc_rows = lax.dot_general(gc_cols, eye, _TN, precision=lax.Precision.HIGHEST,
                              preferred_element_type=F32)
    gend = jnp.sum(g_cols, axis=0, keepdims=True)
    heads = range(nh)
    gi = [z * 2 * nh + h for h in heads]
    bi = [z * 2 * nh + nh + h for h in heads]
    gcc = [gc_cols[:, gi[h]:gi[h] + 1] for h in heads]
    dec = [jnp.where(incl, jnp.exp(jnp.minimum(gcc[h] - gc_rows[gi[h]:gi[h] + 1, :], 0.0)), 0.0) for h in heads]
    beta = [beta_cols[:, bi[h]:bi[h] + 1] for h in heads]
    egc = [jnp.exp(gcc[h]) for h in heads]
    eend = [jnp.exp(gend[:, gi[h]:gi[h] + 1] - gcc[h]) for h in heads]
    tail = [jnp.exp(gend[:, gi[h]:gi[h] + 1]) for h in heads]
    qs = [q_ref[:, j * hd:(j + 1) * hd] for j in range(DN_QK_HEADS)]
    ks = [k_ref[:, j * hd:(j + 1) * hd] for j in range(DN_QK_HEADS)]
    vs = [v_ref[:, h * hd:(h + 1) * hd].astype(F32) for h in heads]
    kk = [_bdot(ks[j], ks[j], _NT) for j in range(DN_QK_HEADS)]
    qk = [_bdot(qs[j], ks[j], _NT) for j in range(DN_QK_HEADS)]
    a_m = [jnp.where(strict, kk[h // rep] * beta[h] * dec[h], 0.0) for h in heads]
    qkd = [(qk[h // rep] * dec[h]).astype(BF16) for h in heads]
    tm = [eye - a_m[h] for h in heads]
    pw = [(-a_m[h]).astype(BF16) for h in heads]
    for _ in range(int(math.log2(c)) - 1):
        pw = [_bdot(pw[h], pw[h]).astype(BF16) for h in heads]
        tm = [tm[h] + _bdot(tm[h], pw[h]) for h in heads]
    kf = [ks[h // rep].astype(F32) for h in heads]
    rhs = [jnp.concatenate([vs[h] * beta[h], kf[h] * (beta[h] * egc[h])], axis=1) for h in heads]
    uw = [_bdot(tm[h], rhs[h]) for h in heads]
    s0 = [s_ref[h] for h in heads]
    lhs = [jnp.concatenate([uw[h][:, hd:], qs[h // rep].astype(F32) * egc[h]], axis=0) for h in heads]
    ws_qs = [_bdot(lhs[h], s0[h]) for h in heads]
    v_new = [(uw[h][:, :hd] - ws_qs[h][:c]).astype(BF16) for h in heads]
    o = [ws_qs[h][c:] + _bdot(qkd[h], v_new[h]) for h in heads]
    for h in heads:
        s_ref[h] = s0[h] * tail[h] + _bdot(kf[h] * eend[h], v_new[h], _TN)
    if not backward:
        for h in heads:
            o_ref[:, h * hd:(h + 1) * hd] = o[h]
    else:
        for h in heads:
            ot = o[h] + of_ref[:, h * hd:(h + 1) * hd]
            on = ot * lax.rsqrt(jnp.mean(ot * ot, -1, keepdims=True) + RMS_EPS) * ng_ref[...]
            zz = z_ref[:, h * hd:(h + 1) * hd].astype(F32)
            o_ref[:, h * hd:(h + 1) * hd] = (on * (zz * _sigmoid(zz))).astype(o_ref.dtype)


def deltanet_mix(lay, qkv, p, ab, a_log, dt_bias, norm_g):
    c = DN_CHUNK
    n = lay.n_chunks(c)
    n_tok = qkv.shape[0]
    nh = DN_V_HEADS
    neg_a = -jnp.exp(a_log.astype(F32))
    na = jnp.concatenate([neg_a, jnp.zeros_like(neg_a)], axis=1).reshape(1, 4 * nh)
    dt = jnp.concatenate([dt_bias.astype(F32), jnp.zeros_like(neg_a)], axis=1).reshape(1, 4 * nh)
    small = lambda a: pl.BlockSpec(a.shape, lambda b, i: (0, 0))
    consts = (na, dt)
    state = [pltpu.VMEM((nh, DN_HEAD_DIM, DN_HEAD_DIM), F32)]

    def specs(backward):
        rb = lambda b, i: lay.row_block(c, b, lay.seq_chunk(c, backward, i))
        return lambda w, off: pl.BlockSpec((c, w), lambda b, i: (rb(b, i), off))

    common = lambda blk: [blk(DN_QK_W, 0), blk(DN_QK_W, 1), blk(DN_V_W, 1), blk(4 * nh, 0)] + [small(a) for a in consts]
    blk = specs(False)
    o_f = pl.pallas_call(
        functools.partial(_dn_scan_kernel, False), grid=(lay.batch, n),
        in_specs=common(blk), out_specs=blk(DN_V_W, 0),
        out_shape=jax.ShapeDtypeStruct((n_tok, DN_V_W), F32),
        scratch_shapes=state, compiler_params=_params("arbitrary", "arbitrary"), name="dn_scan_fwd",
    )(qkv, qkv, qkv, ab, *consts)
    blk = specs(True)
    return pl.pallas_call(
        functools.partial(_dn_scan_kernel, True), grid=(lay.batch, n),
        in_specs=common(blk) + [blk(DN_V_W, 0), blk(DN_V_W, 2), small(norm_g.reshape(1, DN_HEAD_DIM))],
        out_specs=blk(DN_V_W, 0),
        out_shape=jax.ShapeDtypeStruct((n_tok, DN_V_W), BF16),
        scratch_shapes=state, compiler_params=_params("arbitrary", "arbitrary"), name="dn_scan_bwd",
    )(qkv, qkv, qkv, ab, *consts, o_f, p, norm_g.reshape(1, DN_HEAD_DIM))


def _rwkv_chunk_kernel(r_ref, v_ref, kk_ref, wl_ref, kd_ref, a_ref, o_ref, s_ref,
                       at_s, bt_s, kt_s, rt_s, be_s, ke_s, vb_s, gc_s):
    z = pl.program_id(0)
    i = pl.program_id(2)
    c = r_ref.shape[0]

    @pl.when(i == 0)
    def _():
        s_ref[...] = jnp.zeros_like(s_ref)

    row = lax.broadcasted_iota(jnp.int32, (c, c), 0)
    col = lax.broadcasted_iota(jnp.int32, (c, c), 1)
    lag = (row - col) * (1 - 2 * z)
    incl = lag >= 0
    strict = lag > 0

    wl = wl_ref[...]
    logw = -jnp.exp(-_softplus(-wl) - 0.5)
    cum = _hdot(incl.astype(F32), logw)
    c_last = jnp.sum(logw, axis=0, keepdims=True)
    kk = kk_ref[...].astype(F32)
    kb = kk * a_ref[...].astype(F32)
    kd = kd_ref[...].astype(F32)
    g_inv = jnp.exp(-cum)
    e_end = jnp.exp(c_last - cum)
    at_s[...] = (-kk * jnp.exp(cum - logw)).astype(BF16)
    bt_s[...] = (kb * g_inv).astype(BF16)
    kt_s[...] = (kd * g_inv).astype(BF16)
    rt_s[...] = (r_ref[...].astype(F32) * jnp.exp(cum)).astype(BF16)
    be_s[...] = (kb * e_end).astype(BF16)
    ke_s[...] = (kd * e_end).astype(BF16)
    vb_s[...] = v_ref[...].astype(BF16)
    gc_s[...] = jnp.exp(c_last)

    eye = (row == col).astype(F32)
    hh = range(RWKV_HEADS)
    sl = [slice(h * RWKV_HEAD, (h + 1) * RWKV_HEAD) for h in hh]
    a_ab = [jnp.where(strict, _bdot(at_s[:, sl[h]], bt_s[:, sl[h]], _NT), 0.0) for h in hh]
    a_ak = [jnp.where(strict, _bdot(at_s[:, sl[h]], kt_s[:, sl[h]], _NT), 0.0).astype(BF16) for h in hh]
    m_rb = [jnp.where(incl, _bdot(rt_s[:, sl[h]], bt_s[:, sl[h]], _NT), 0.0).astype(BF16) for h in hh]
    m_rk = [jnp.where(incl, _bdot(rt_s[:, sl[h]], kt_s[:, sl[h]], _NT), 0.0).astype(BF16) for h in hh]
    tm = [eye + a_ab[h] for h in hh]
    p = [a_ab[h].astype(BF16) for h in hh]
    for _ in range(int(math.log2(c)) - 1):
        p = [_bdot(p[h], p[h]).astype(BF16) for h in hh]
        tm = [tm[h] + _bdot(tm[h], p[h]) for h in hh]
    s0 = [s_ref[h] for h in hh]
    rhs = [_bdot(at_s[:, sl[h]], s0[h], _NT) + _bdot(a_ak[h], vb_s[:, sl[h]]) for h in hh]
    u = [_bdot(tm[h], rhs[h]).astype(BF16) for h in hh]
    for h in hh:
        o_ref[:, sl[h]] = _bdot(rt_s[:, sl[h]], s0[h], _NT) + _bdot(m_rb[h], u[h]) + _bdot(m_rk[h], vb_s[:, sl[h]])
    for h in hh:
        s_ref[h] = s0[h] * gc_s[:, sl[h]] + _bdot(u[h], be_s[:, sl[h]], _TN) + _bdot(vb_s[:, sl[h]], ke_s[:, sl[h]], _TN)


def rwkv_scan(lay, r, v, kk, wl, kd, a):
    n_tok, d = r.shape
    c = RWKV_CHUNK
    n = lay.n_chunks(c)

    def rb(z, b, i):
        sc = jnp.where(z == 0, lay.seq_chunk(c, False, i), lay.seq_chunk(c, True, i))
        return lay.row_block(c, b, sc)

    shared = pl.BlockSpec((c, d), lambda z, b, i: (rb(z, b, i), 0))
    perdir = pl.BlockSpec((None, c, d), lambda z, b, i: (z, rb(z, b, i), 0))
    bf = lambda: pltpu.VMEM((c, d), BF16)
    return pl.pallas_call(
        _rwkv_chunk_kernel,
        grid=(2, lay.batch, n),
        in_specs=[shared, shared, shared, perdir, perdir, perdir],
        out_specs=perdir,
        out_shape=jax.ShapeDtypeStruct((2, n_tok, d), F32),
        scratch_shapes=[pltpu.VMEM((RWKV_HEADS, RWKV_HEAD, RWKV_HEAD), F32),
                        bf(), bf(), bf(), bf(), bf(), bf(), bf(), pltpu.VMEM((1, d), F32)],
        compiler_params=_params("arbitrary", "arbitrary", "arbitrary"),
        name="rwkv_scan",
    )(r, v, kk, wl, kd, a)


def _seq_views(lay, h):
    d = h.shape[-1]
    return h[:lay.ctx_tok].reshape(lay.batch, lay.n_ctx, d), h[lay.ctx_tok:].reshape(lay.batch, lay.seq, d)


def _centred_shift(lay, h):
    def one(x):
        p = jnp.pad(x, ((0, 0), (1, 1), (0, 0)))
        return (0.5 * (p[:, :-2] + p[:, 2:]) - x).reshape(-1, x.shape[-1])
    return jnp.concatenate([one(v) for v in _seq_views(lay, h)], 0)


def rwkv7_mix(lay, h, mix, w_rkv, w0, w1, w2, a0, a1, a2, g1, g2, k_k, k_a, r_k, lnx_g):
    d = D_MODEL
    hf = h.astype(F32)
    xx = _centred_shift(lay, hf)
    xm = [(hf + xx * mix[i]).astype(BF16) for i in range(6)]
    bw = lambda w: w.astype(BF16)
    r = proj(xm[0], bw(w_rkv[0]), F32, 1024)
    k = proj(xm[1], bw(w_rkv[1]), F32, 1024)
    v = proj(xm[2], bw(w_rkv[2]), F32, 1024)
    lw = w1.shape[-1]
    hw = proj(xm[3], bw(jnp.concatenate([w1[0], w1[1]], -1)), F32, 2 * lw)
    ha = proj(xm[4], bw(jnp.concatenate([a1[0], a1[1]], -1)), F32, 2 * lw)
    hg = proj(xm[5], bw(g1), F32, g1.shape[-1])
    th = jnp.tanh(hw).astype(BF16)
    wl = jnp.stack([w0[z] + proj(th[:, z * lw:(z + 1) * lw], bw(w2[z]), F32, 1024) for z in range(2)])
    ha = ha.astype(BF16)
    a = jax.nn.sigmoid(jnp.stack([a0[z] + proj(ha[:, z * lw:(z + 1) * lw], bw(a2[z]), F32, 1024) for z in range(2)]))
    g = proj(jax.nn.sigmoid(hg).astype(BF16), bw(g2), F32, 1024)
    heads = lambda u: u.reshape(*u.shape[:-1], RWKV_HEADS, RWKV_HEAD)
    kkh = heads(k * k_k)
    kk = (kkh * lax.rsqrt(jnp.sum(kkh * kkh, -1, keepdims=True) + L2_EPS)).reshape(k.shape)
    kd = k[None] * (1.0 + (a - 1.0) * k_a)
    o = rwkv_scan(lay, r.astype(BF16), v.astype(BF16), kk.astype(BF16), wl, kd.astype(BF16), a.astype(BF16))
    bonus = jnp.sum(heads(r)[None] * heads(kd) * r_k, axis=(0, -1))[..., None] * heads(v)
    oh = heads(o[0] + o[1])
    mu = jnp.mean(oh, -1, keepdims=True)
    var = jnp.mean(jnp.square(oh - mu), -1, keepdims=True)
    on = ((oh - mu) * lax.rsqrt(var + LNX_EPS)).reshape(-1, d) * lnx_g + bonus.reshape(-1, d)
    return (on * g).astype(BF16)


def _moe_kernel(be_ref, nb_ref, x_ref, wg_ref, wu_ref, wd_ref, gate_ref, o_ref, acc_s):
    j = pl.program_id(0)
    f = pl.program_id(1)

    @pl.when(f == 0)
    def _():
        acc_s[...] = jnp.zeros_like(acc_s)

    @pl.when(j < nb_ref[0])
    def _():
        x = x_ref[...]
        g = _bdot(x, wg_ref[...])
        u = _bdot(x, wu_ref[...])
        acc_s[...] += _bdot(g * _sigmoid(g) * u, wd_ref[...])

    @pl.when(f == pl.num_programs(1) - 1)
    def _():
        o_ref[...] = acc_s[...] * gate_ref[...]


def moe_experts(xb, block_e, n_used, w_gu, w_down, slot_gate):
    n_slots, d = xb.shape
    bm = MOE_ROWS
    nf = MOE_F_CHUNKS
    tf = w_down.shape[1] // nf
    return pl.pallas_call(
        _moe_kernel,
        grid_spec=pltpu.PrefetchScalarGridSpec(
            num_scalar_prefetch=2,
            grid=(n_slots // bm, nf),
            in_specs=[pl.BlockSpec((bm, d), lambda j, f, be, nb: (j, 0)),
                      pl.BlockSpec((None, d, tf), lambda j, f, be, nb: (be[j], 0, f)),
                      pl.BlockSpec((None, d, tf), lambda j, f, be, nb: (be[j], 0, f + nf)),
                      pl.BlockSpec((None, tf, d), lambda j, f, be, nb: (be[j], f, 0)),
                      pl.BlockSpec((bm, 1), lambda j, f, be, nb: (j, 0))],
            out_specs=pl.BlockSpec((bm, d), lambda j, f, be, nb: (j, 0)),
            scratch_shapes=[pltpu.VMEM((bm, d), F32)]),
        out_shape=jax.ShapeDtypeStruct((n_slots, d), F32),
        compiler_params=_params("arbitrary", "arbitrary"),
        name="moe_experts",
    )(block_e, n_used, xb, w_gu, w_gu, w_down, slot_gate)


def moe_swiglu(h, w_router, w_gu, w_down):
    n, d = h.shape
    logits = proj(h, w_router.astype(BF16), F32, N_EXPERTS)
    top_logit, top_e = lax.top_k(logits, TOP_K)
    gate = jax.nn.softmax(top_logit, axis=-1)
    flat_e = top_e.reshape(-1)
    order = jnp.argsort(flat_e)
    e_sorted = flat_e[order]
    counts = jnp.bincount(flat_e, length=N_EXPERTS)
    padded = (counts + MOE_ROWS - 1) // MOE_ROWS * MOE_ROWS
    start = jnp.cumsum(counts) - counts
    pend = jnp.cumsum(padded)
    pstart = pend - padded
    slot = (pstart[e_sorted] + jnp.arange(n * TOP_K) - start[e_sorted]).astype(jnp.int32)
    n_slots = (n * TOP_K + MOE_ROWS - 1) // MOE_ROWS * MOE_ROWS + N_EXPERTS * MOE_ROWS
    n_blocks = n_slots // MOE_ROWS
    slot_tok = jnp.full((n_slots,), n, jnp.int32).at[slot].set((order // TOP_K).astype(jnp.int32))
    slot_gate = jnp.zeros((n_slots,), F32).at[slot].set(gate.reshape(-1)[order])
    block_e = jnp.minimum(jnp.sum(jnp.arange(n_blocks)[:, None] * MOE_ROWS >= pend[None, :], axis=1),
                          N_EXPERTS - 1).astype(jnp.int32)
    n_used = (pend[-1] // MOE_ROWS).astype(jnp.int32).reshape(1)
    x_pad = jnp.concatenate([h, jnp.zeros((1, d), h.dtype)], 0)
    yb = moe_experts(x_pad[slot_tok], block_e, n_used, w_gu.astype(BF16), w_down.astype(BF16), slot_gate[:, None])
    tok_slot = jnp.zeros((n * TOP_K,), jnp.int32).at[order].set(slot).reshape(n, TOP_K)
    return yb[tok_slot[:, 0]] + yb[tok_slot[:, 1]]


def kernel(x, c, ctx, c_ctx, mod_w, mod_b, ln_g, ln_b,
           ret_w_in, ret_decay, ret_gn_g, ret_w_out,
           dn_w_in, dn_conv_w, dn_a_log, dn_dt_bias, dn_norm_g, dn_w_out,
           rk_mix, rk_w_rkv, rk_w0, rk_w1, rk_w2, rk_a0, rk_a1, rk_a2, rk_g1, rk_g2,
           rk_k_k, rk_k_a, rk_r_k, rk_lnx_g, rk_w_out,
           ffn_w_gu, ffn_w_down, moe_router, moe_w_gu, moe_w_down):
    bsz, t, d = x.shape
    n_ctx = ctx.shape[1]
    lay = Layout(bsz, n_ctx, t)
    bw = lambda w: w.astype(BF16)
    s_rows = jax.nn.silu(jnp.concatenate([c_ctx[None], c], 0))
    s_pad = jnp.zeros((8, d), F32).at[:1 + bsz].set(s_rows)
    mods = modulation_rows(s_pad, mod_w, mod_b)[:, :1 + bsz].reshape(DEPTH, 1 + bsz, 6, 1, d)
    mod = lambda i, k: mods[i, :, k]
    cos_t, sin_t = rope_tables(lay)
    xs = jnp.concatenate([ctx.reshape(-1, d), x.reshape(-1, d)], 0)
    h = modulate(lay, xs, mod(0, 1), mod(0, 0))
    for i in range(DEPTH):
        last = i == DEPTH - 1
        kind, j = i % N_MIXERS, i // N_MIXERS
        if kind == 0:
            p = proj(h, bw(ret_w_in[j]), BF16, 1024)
            log_gamma = jax.nn.log_sigmoid(ret_decay[j].astype(F32))
            a = retention_mix(lay, p, log_gamma, ret_gn_g[j], cos_t, sin_t)
            w_out = ret_w_out[j]
        elif kind == 1:
            n_main = 2 * DN_QK_W + 2 * DN_V_W
            p = proj(h, bw(dn_w_in[j][:, :n_main]), BF16, 1024)
            w_ab = bw(dn_w_in[j][:, n_main:])
            ab = proj(h, w_ab, F32, w_ab.shape[1])
            qkv = dn_prep(lay, p, dn_conv_w[j])
            a = deltanet_mix(lay, qkv, p, ab, dn_a_log[j], dn_dt_bias[j], dn_norm_g[j])
            w_out = dn_w_out[j]
        else:
            a = rwkv7_mix(lay, h, rk_mix[j], rk_w_rkv[j], rk_w0[j], rk_w1[j], rk_w2[j], rk_a0[j], rk_a1[j],
                          rk_a2[j], rk_g1[j], rk_g2[j], rk_k_k[j], rk_k_a[j], rk_r_k[j], rk_lnx_g[j])
            w_out = rk_w_out[j]
        xs, h = out_ln(lay, a, bw(w_out), xs, mod(i, 2), ln_g[i, 0], ln_b[i, 0], mod(i, 4), mod(i, 3))
        nxt = (i + 1) % DEPTH
        if i % 2 == 0:
            hm = swiglu_in(h, bw(ffn_w_gu[i // 2]), 256)
            xs, h = out_ln(lay, hm, bw(ffn_w_down[i // 2]), xs, mod(i, 5), ln_g[i, 1], ln_b[i, 1],
                           mod(nxt, 1), mod(nxt, 0))
        else:
            f = moe_swiglu(h, moe_router[i // 2], moe_w_gu[i // 2], moe_w_down[i // 2])
            xs, h = out_ln(lay, f, None, xs, mod(i, 5), ln_g[i, 1], ln_b[i, 1], mod(nxt, 1), mod(nxt, 0))
    return xs[lay.ctx_tok:].reshape(bsz, t, d)
```

```python
import math, functools
import jax
import jax.numpy as jnp
from jax import lax
import numpy as np
from jax.experimental import pallas as pl
from jax.experimental.pallas import tpu as pltpu

D_MODEL = 1024
DEPTH = 4
GRID_W = 64
N_MIXERS = 3
ALPHA = (2 * DEPTH) ** 0.25
LN_EPS = 1e-5
GN_EPS = 1e-5
RMS_EPS = 1e-6
LNX_EPS = 64e-5
L2_EPS = 1e-6

RET_HEADS = 4
RET_DK = D_MODEL // RET_HEADS
RET_DV = 2 * RET_DK
RET_CHUNK = 128
ROPE_BASE = 10000.0

DN_QK_HEADS = 8
DN_V_HEADS = 16
DN_HEAD_DIM = 128
DN_CHUNK = 64
DN_CONV = 5
DN_QK_W = DN_QK_HEADS * DN_HEAD_DIM
DN_V_W = DN_V_HEADS * DN_HEAD_DIM
SEQ_HALO = 16

RWKV_HEAD = 64
RWKV_HEADS = D_MODEL // RWKV_HEAD
RWKV_CHUNK = 64

FFN_DIM = 2816
N_EXPERTS = 8
TOP_K = 2
EXPERT_DIM = 3584
MOE_ROWS = 1024
MOE_F_CHUNKS = 7

ROW_TILE = 512
VMEM_LIMIT = 48 * 1024 * 1024

BF16 = jnp.bfloat16
F32 = jnp.float32
_NT = (((1,), (1,)), ((), ()))
_TN = (((0,), (0,)), ((), ()))


def _bdot(x, y, dims=None):
    x = x.astype(BF16)
    y = y.astype(BF16)
    if dims is None:
        return jnp.dot(x, y, preferred_element_type=F32)
    return lax.dot_general(x, y, dims, preferred_element_type=F32)


def _hdot(x, y):
    return jnp.dot(x, y, precision=lax.Precision.HIGHEST, preferred_element_type=F32)


def _sigmoid(x):
    return 1.0 / (1.0 + jnp.exp(-x))


def _softplus(x):
    return jnp.maximum(x, 0.0) + jnp.log(1.0 + jnp.exp(-jnp.abs(x)))


def _params(*sem):
    return pltpu.CompilerParams(dimension_semantics=sem, vmem_limit_bytes=VMEM_LIMIT)


class Layout:
    def __init__(self, batch, n_ctx, seq):
        self.batch, self.n_ctx, self.seq = batch, n_ctx, seq
        self.ctx_tok = batch * n_ctx
        self.n_tok = self.ctx_tok + batch * seq
        self.row_tile = math.gcd(ROW_TILE, n_ctx * batch, seq)

    def mod_index(self, tile, j):
        r0 = j * tile
        return jnp.where(r0 < self.ctx_tok, 0, 1 + (r0 - self.ctx_tok) // self.seq)

    def seq_chunk(self, chunk, backward, i):
        nc, nl = self.n_ctx // chunk, self.seq // chunk
        if not backward:
            return i
        return jnp.where(i < nc, nc - 1 - i, 2 * nc + nl - 1 - i)

    def row_block(self, chunk, b, sc):
        nc, nl = self.n_ctx // chunk, self.seq // chunk
        return jnp.where(sc < nc, b * nc + sc, self.batch * nc + b * nl + sc - nc)

    def n_chunks(self, chunk):
        return (self.n_ctx + self.seq) // chunk


def _proj_kernel(h_ref, w_ref, o_ref):
    o_ref[...] = _bdot(h_ref[...], w_ref[...]).astype(o_ref.dtype)


def proj(h, w, out_dtype, tn):
    n_tok, k = h.shape
    n = w.shape[1]
    tm = math.gcd(1024, n_tok)
    return pl.pallas_call(
        _proj_kernel,
        grid=(n // tn, n_tok // tm),
        in_specs=[pl.BlockSpec((tm, k), lambda c, j: (j, 0)), pl.BlockSpec((k, tn), lambda c, j: (0, c))],
        out_specs=pl.BlockSpec((tm, tn), lambda c, j: (j, c)),
        out_shape=jax.ShapeDtypeStruct((n_tok, n), out_dtype),
        compiler_params=_params("arbitrary", "arbitrary"),
        name="proj",
    )(h, w)


def _swiglu_in_kernel(h_ref, wg_ref, wu_ref, o_ref):
    h = h_ref[...]
    g = _bdot(h, wg_ref[...])
    u = _bdot(h, wu_ref[...])
    o_ref[...] = (g * _sigmoid(g) * u).astype(o_ref.dtype)


def swiglu_in(h, w_gu, tn):
    n_tok, k = h.shape
    f = w_gu.shape[1] // 2
    tm = math.gcd(1024, n_tok)
    nf = f // tn
    return pl.pallas_call(
        _swiglu_in_kernel,
        grid=(nf, n_tok // tm),
        in_specs=[pl.BlockSpec((tm, k), lambda c, j: (j, 0)),
                  pl.BlockSpec((k, tn), lambda c, j: (0, c)),
                  pl.BlockSpec((k, tn), lambda c, j: (0, c + nf))],
        out_specs=pl.BlockSpec((tm, tn), lambda c, j: (j, c)),
        out_shape=jax.ShapeDtypeStruct((n_tok, f), BF16),
        compiler_params=_params("arbitrary", "arbitrary"),
        name="swiglu_in",
    )(h, w_gu, w_gu)


def _deepnorm_epilogue(x, f, ga_ref, g_ref, b_ref, sc_ref, sh_ref, x_out, h_out):
    y = ALPHA * x + (1.0 + ga_ref[...]) * f
    mu = jnp.mean(y, -1, keepdims=True)
    yc = y - mu
    var = jnp.mean(yc * yc, -1, keepdims=True)
    xn = yc * lax.rsqrt(var + LN_EPS) * g_ref[...] + b_ref[...]
    x_out[...] = xn
    h_out[...] = (xn * (1.0 + sc_ref[...]) + sh_ref[...]).astype(h_out.dtype)


def _out_ln_kernel(a_ref, w_ref, x_ref, ga_ref, g_ref, b_ref, sc_ref, sh_ref, x_out, h_out):
    _deepnorm_epilogue(x_ref[...], _bdot(a_ref[...], w_ref[...]), ga_ref, g_ref, b_ref, sc_ref, sh_ref, x_out, h_out)


def _resid_ln_kernel(f_ref, x_ref, ga_ref, g_ref, b_ref, sc_ref, sh_ref, x_out, h_out):
    _deepnorm_epilogue(x_ref[...], f_ref[...], ga_ref, g_ref, b_ref, sc_ref, sh_ref, x_out, h_out)


def out_ln(lay, a, w, x, gate, ln_g, ln_b, sc_next, sh_next):
    n_tok, d = x.shape
    tm = lay.row_tile
    row = lambda j: (j, 0)
    mod = pl.BlockSpec((None, 1, d), lambda j: (lay.mod_index(tm, j), 0, 0))
    vec = pl.BlockSpec((1, d), lambda j: (0, 0))
    tok = pl.BlockSpec((tm, d), row)
    if w is None:
        body, lhs, lhs_specs = _resid_ln_kernel, (a,), [tok]
    else:
        k = a.shape[1]
        body, lhs = _out_ln_kernel, (a, w)
        lhs_specs = [pl.BlockSpec((tm, k), row), pl.BlockSpec((k, d), lambda j: (0, 0))]
    return pl.pallas_call(
        body,
        grid=(n_tok // tm,),
        in_specs=lhs_specs + [tok, mod, vec, vec, mod, mod],
        out_specs=[tok, tok],
        out_shape=[jax.ShapeDtypeStruct((n_tok, d), F32), jax.ShapeDtypeStruct((n_tok, d), BF16)],
        compiler_params=_params("arbitrary"),
        name="out_ln",
    )(*lhs, x, gate, ln_g.reshape(1, d), ln_b.reshape(1, d), sc_next, sh_next)


def _modulate_kernel(x_ref, sc_ref, sh_ref, h_out):
    h_out[...] = (x_ref[...] * (1.0 + sc_ref[...]) + sh_ref[...]).astype(h_out.dtype)


def modulate(lay, x, sc, sh):
    n_tok, d = x.shape
    tm = lay.row_tile
    mod = pl.BlockSpec((None, 1, d), lambda j: (lay.mod_index(tm, j), 0, 0))
    tok = pl.BlockSpec((tm, d), lambda j: (j, 0))
    return pl.pallas_call(
        _modulate_kernel, grid=(n_tok // tm,), in_specs=[tok, mod, mod], out_specs=tok,
        out_shape=jax.ShapeDtypeStruct((n_tok, d), BF16), compiler_params=_params("arbitrary"), name="modulate",
    )(x, sc, sh)


def _mod_kernel(s_ref, w_ref, b_ref, o_ref):
    o_ref[...] = _bdot(s_ref[...], w_ref[...]) + b_ref[...]


def modulation_rows(s, mod_w, mod_b):
    r, d = s.shape
    depth, _, n = mod_w.shape
    tn = 1024
    return pl.pallas_call(
        _mod_kernel,
        grid=(depth, n // tn),
        in_specs=[pl.BlockSpec((r, d), lambda i, c: (0, 0)),
                  pl.BlockSpec((None, d, tn), lambda i, c: (i, 0, c)),
                  pl.BlockSpec((None, 1, tn), lambda i, c: (i, 0, c))],
        out_specs=pl.BlockSpec((None, r, tn), lambda i, c: (i, 0, c)),
        out_shape=jax.ShapeDtypeStruct((depth, r, n), F32),
        compiler_params=_params("arbitrary", "arbitrary"),
        name="modulation_rows",
    )(s, mod_w, mod_b.reshape(depth, 1, n))


def _rope(x, cos, sin):
    half = x.shape[1] // 2
    parts = []
    for p in range(2):
        xs = x[:, p * half:(p + 1) * half]
        parts.append(xs * cos[:, p * half:(p + 1) * half]
                     + pltpu.roll(xs, half // 2, axis=1) * sin[:, p * half:(p + 1) * half])
    return jnp.concatenate(parts, axis=1)


def _ret_heads(q_ref, k_ref, v_ref, cos_ref, sin_ref):
    cos, sin = cos_ref[...], sin_ref[...]
    hs = range(RET_HEADS)
    q = [_rope(q_ref[:, h * RET_DK:(h + 1) * RET_DK].astype(F32), cos, sin) for h in hs]
    k = [_rope(k_ref[:, h * RET_DK:(h + 1) * RET_DK].astype(F32), cos, sin) * (RET_DK ** -0.5) for h in hs]
    v = [v_ref[:, h * RET_DV:(h + 1) * RET_DV] for h in hs]
    return hs, q, k, v


def _ret_fwd_kernel(lg_ref, q_ref, k_ref, v_ref, cos_ref, sin_ref, o_ref, s_ref):
    i = pl.program_id(1)
    c = q_ref.shape[0]

    @pl.when(i == 0)
    def _():
        s_ref[...] = jnp.zeros_like(s_ref)

    hs, q, k, v = _ret_heads(q_ref, k_ref, v_ref, cos_ref, sin_ref)
    row = lax.broadcasted_iota(jnp.int32, (c, c), 0)
    col = lax.broadcasted_iota(jnp.int32, (c, c), 1)
    lag = (row - col).astype(F32)
    pos = lax.broadcasted_iota(jnp.int32, (c, RET_DK), 0).astype(F32)
    full_c = jnp.full((1, 1), c, F32)
    decay = [jnp.where(lag >= 0, jnp.exp(jnp.maximum(lag, 0.0) * lg_ref[0, h]), 0.0)
             + jnp.where(lag <= 0, jnp.exp(jnp.maximum(-lag, 0.0) * lg_ref[1, h]), 0.0) for h in hs]
    scores = [(_bdot(q[h], k[h], _NT) * decay[h]).astype(BF16) for h in hs]
    s = [s_ref[h] for h in hs]
    qd = [(q[h] * jnp.exp((pos + 1.0) * lg_ref[0, h])).astype(BF16) for h in hs]
    kd = [(k[h] * jnp.exp((c - 1.0 - pos) * lg_ref[0, h])).astype(BF16) for h in hs]
    for h in hs:
        o_ref[:, h * RET_DV:(h + 1) * RET_DV] = _bdot(scores[h], v[h]) + _bdot(qd[h], s[h])
    for h in hs:
        s_ref[h] = s[h] * jnp.exp(full_c * lg_ref[0, h]) + _bdot(kd[h], v[h], _TN)


def _ret_bwd_kernel(lg_ref, q_ref, k_ref, v_ref, g_ref, cos_ref, sin_ref, op_ref, gn_ref, o_ref, s_ref):
    i = pl.program_id(1)
    c = q_ref.shape[0]

    @pl.when(i == 0)
    def _():
        s_ref[...] = jnp.zeros_like(s_ref)

    hs, q, k, v = _ret_heads(q_ref, k_ref, v_ref, cos_ref, sin_ref)
    pos = lax.broadcasted_iota(jnp.int32, (c, RET_DK), 0).astype(F32)
    full_c = jnp.full((1, 1), c, F32)
    s = [s_ref[h] for h in hs]
    qd = [(q[h] * jnp.exp((c - pos) * lg_ref[1, h])).astype(BF16) for h in hs]
    kd = [(k[h] * jnp.exp(pos * lg_ref[1, h])).astype(BF16) for h in hs]
    o = [op_ref[:, h * RET_DV:(h + 1) * RET_DV] + _bdot(qd[h], s[h]) for h in hs]
    for h in hs:
        s_ref[h] = s[h] * jnp.exp(full_c * lg_ref[1, h]) + _bdot(kd[h], v[h], _TN)
    for h in hs:
        hv = slice(h * RET_DV, (h + 1) * RET_DV)
        mu = jnp.mean(o[h], -1, keepdims=True)
        oc = o[h] - mu
        var = jnp.mean(oc * oc, -1, keepdims=True)
        g = g_ref[:, hv].astype(F32)
        o_ref[:, hv] = (g * _sigmoid(g) * (oc * lax.rsqrt(var + GN_EPS) * gn_ref[:, hv])).astype(o_ref.dtype)


def retention_mix(lay, p, log_gamma, gn_g, cos_t, sin_t):
    c = RET_CHUNK
    n = lay.n_chunks(c)
    n_tok = p.shape[0]
    hk = RET_HEADS * RET_DK
    hv = RET_HEADS * RET_DV

    def specs(backward):
        sc = lambda i: lay.seq_chunk(c, backward, i)
        blk = lambda w, off: pl.BlockSpec((c, w), lambda b, i: (lay.row_block(c, b, sc(i)), off))
        tab = pl.BlockSpec((c, RET_DK), lambda b, i: (sc(i), 0))
        return blk, tab

    smem = pl.BlockSpec(memory_space=pltpu.SMEM)
    grid = (lay.batch, n)
    state = [pltpu.VMEM((RET_HEADS, RET_DK, RET_DV), F32)]
    blk, tab = specs(False)
    o_part = pl.pallas_call(
        _ret_fwd_kernel, grid=grid,
        in_specs=[smem, blk(hk, 0), blk(hk, 1), blk(hv, 1), tab, tab],
        out_specs=blk(hv, 0),
        out_shape=jax.ShapeDtypeStruct((n_tok, hv), F32),
        scratch_shapes=state, compiler_params=_params("arbitrary", "arbitrary"), name="ret_fwd",
    )(log_gamma, p, p, p, cos_t, sin_t)
    blk, tab = specs(True)
    return pl.pallas_call(
        _ret_bwd_kernel, grid=grid,
        in_specs=[smem, blk(hk, 0), blk(hk, 1), blk(hv, 1), blk(hv, 2), tab, tab,
                  blk(hv, 0), pl.BlockSpec((1, hv), lambda b, i: (0, 0))],
        out_specs=blk(hv, 0),
        out_shape=jax.ShapeDtypeStruct((n_tok, hv), BF16),
        scratch_shapes=state, compiler_params=_params("arbitrary", "arbitrary"), name="ret_bwd",
    )(log_gamma, p, p, p, p, cos_t, sin_t, o_part, gn_g.reshape(1, hv))


def rope_tables(lay):
    quarter = RET_DK // 4
    t = jnp.arange(lay.seq)
    inv = ROPE_BASE ** (-jnp.arange(quarter, dtype=F32) / quarter)
    ang_r = (t // GRID_W).astype(F32)[:, None] * inv
    ang_c = (t % GRID_W).astype(F32)[:, None] * inv
    cos = jnp.concatenate([jnp.cos(ang_r)] * 2 + [jnp.cos(ang_c)] * 2, -1)
    sin = jnp.concatenate([-jnp.sin(ang_r), jnp.sin(ang_r), -jnp.sin(ang_c), jnp.sin(ang_c)], -1)
    cos = jnp.concatenate([jnp.ones((lay.n_ctx, RET_DK), F32), cos], 0)
    sin = jnp.concatenate([jnp.zeros((lay.n_ctx, RET_DK), F32), sin], 0)
    return cos, sin


def _dn_prep_kernel(lay, tm, prev_ref, x_ref, next_ref, w_ref, o_ref, xe_s):
    j = pl.program_id(0)
    ct = pl.program_id(1)
    r0 = j * tm
    in_ctx = r0 < lay.ctx_tok
    seq_len = jnp.where(in_ctx, lay.n_ctx, lay.seq)
    off = jnp.where(in_ctx, r0, r0 - lay.ctx_tok) % seq_len
    first = off == 0
    last = off + tm == seq_len
    hal = SEQ_HALO
    xe_s[0:hal, :] = jnp.where(first, 0.0, prev_ref[...].astype(F32))
    xe_s[hal:hal + tm, :] = x_ref[...].astype(F32)
    xe_s[hal + tm:, :] = jnp.where(last, 0.0, next_ref[...].astype(F32))
    pad = (DN_CONV - 1) // 2
    acc = xe_s[pl.ds(hal - pad, tm), :] * w_ref[0:1, :]
    for d in range(1, DN_CONV):
        acc = acc + xe_s[pl.ds(hal - pad + d, tm), :] * w_ref[d:d + 1, :]
    y = acc * _sigmoid(acc)
    n_qk_tiles = 2 * DN_QK_W // x_ref.shape[1]

    @pl.when(ct >= n_qk_tiles)
    def _():
        o_ref[...] = y.astype(o_ref.dtype)

    @pl.when(ct < n_qk_tiles)
    def _():
        is_q = ct * x_ref.shape[1] < DN_QK_W
        scale = jnp.where(is_q, DN_HEAD_DIM ** -0.5, 1.0)
        for s in range(x_ref.shape[1] // DN_HEAD_DIM):
            ys = y[:, s * DN_HEAD_DIM:(s + 1) * DN_HEAD_DIM]
            inv = lax.rsqrt(jnp.sum(ys * ys, -1, keepdims=True) + L2_EPS) * scale
            o_ref[:, s * DN_HEAD_DIM:(s + 1) * DN_HEAD_DIM] = (ys * inv).astype(o_ref.dtype)


def dn_prep(lay, p, conv_w):
    n_tok = p.shape[0]
    w = 2 * DN_QK_W + DN_V_W
    tm = math.gcd(256, lay.n_ctx, lay.seq)
    tc = 1024
    hb = tm // SEQ_HALO
    last_hb = n_tok // SEQ_HALO - 1
    return pl.pallas_call(
        functools.partial(_dn_prep_kernel, lay, tm),
        grid=(n_tok // tm, w // tc),
        in_specs=[pl.BlockSpec((SEQ_HALO, tc), lambda j, c: (jnp.maximum(j * hb - 1, 0), c)),
                  pl.BlockSpec((tm, tc), lambda j, c: (j, c)),
                  pl.BlockSpec((SEQ_HALO, tc), lambda j, c: (jnp.minimum((j + 1) * hb, last_hb), c)),
                  pl.BlockSpec((DN_CONV, tc), lambda j, c: (0, c))],
        out_specs=pl.BlockSpec((tm, tc), lambda j, c: (j, c)),
        out_shape=jax.ShapeDtypeStruct((n_tok, w), BF16),
        scratch_shapes=[pltpu.VMEM((tm + 2 * SEQ_HALO, tc), F32)],
        compiler_params=_params("arbitrary", "arbitrary"), name="dn_prep",
    )(p, p, p, conv_w)


def _dn_scan_kernel(backward, q_ref, k_ref, v_ref, ab_ref, na_row, dt_row, *rest):
    if backward:
        of_ref, z_ref, ng_ref, o_ref, s_ref = rest
    else:
        o_ref, s_ref = rest
    i = pl.program_id(1)
    c = q_ref.shape[0]
    z = 1 if backward else 0
    nh = DN_V_HEADS
    hd = DN_HEAD_DIM
    rep = DN_V_HEADS // DN_QK_HEADS

    @pl.when(i == 0)
    def _():
        s_ref[...] = jnp.zeros_like(s_ref)

    row = lax.broadcasted_iota(jnp.int32, (c, c), 0)
    col = lax.broadcasted_iota(jnp.int32, (c, c), 1)
    lag = (col - row) if backward else (row - col)
    incl = lag >= 0
    strict = lag > 0
    eye = (row == col).astype(F32)
    ab = ab_ref[...]
    g_cols = na_row[...] * _softplus(ab + dt_row[...])
    beta_cols = _sigmoid(ab)
    gc_cols = _hdot(incl.astype(F32), g_cols)
    gc_rows = lax.dot_general(gc_cols, eye, _TN, precision=lax.Precision.HIGHEST,
                              preferred_element_type=F32)
    gend = jnp.sum(g_cols, axis=0, keepdims=True)
    heads = range(nh)
    gi = [z * 2 * nh + h for h in heads]
    bi = [z * 2 * nh + nh + h for h in heads]
    gcc = [gc_cols[:, gi[h]:gi[h] + 1] for h in heads]
    dec = [jnp.where(incl, jnp.exp(jnp.minimum(gcc[h] - gc_rows[gi[h]:gi[h] + 1, :], 0.0)), 0.0) for h in heads]
    beta = [beta_cols[:, bi[h]:bi[h] + 1] for h in heads]
    egc = [jnp.exp(gcc[h]) for h in heads]
    eend = [jnp.exp(gend[:, gi[h]:gi[h] + 1] - gcc[h]) for h in heads]
    tail = [jnp.exp(gend[:, gi[h]:gi[h] + 1]) for h in heads]
    qs = [q_ref[:, j * hd:(j + 1) * hd] for j in range(DN_QK_HEADS)]
    ks = [k_ref[:, j * hd:(j + 1) * hd] for j in range(DN_QK_HEADS)]
    vs = [v_ref[:, h * hd:(h + 1) * hd].astype(F32) for h in heads]
    kk = [_bdot(ks[j], ks[j], _NT) for j in range(DN_QK_HEADS)]
    qk = [_bdot(qs[j], ks[j], _NT) for j in range(DN_QK_HEADS)]
    a_m = [jnp.where(strict, kk[h // rep] * beta[h] * dec[h], 0.0) for h in heads]
    qkd = [(qk[h // rep] * dec[h]).astype(BF16) for h in heads]
    tm = [eye - a_m[h] for h in heads]
    pw = [(-a_m[h]).astype(BF16) for h in heads]
    for _ in range(int(math.log2(c)) - 1):
        pw = [_bdot(pw[h], pw[h]).astype(BF16) for h in heads]
        tm = [tm[h] + _bdot(tm[h], pw[h]) for h in heads]
    kf = [ks[h // rep].astype(F32) for h in heads]
    rhs = [jnp.concatenate([vs[h] * beta[h], kf[h] * (beta[h] * egc[h])], axis=1) for h in heads]
    uw = [_bdot(tm[h], rhs[h]) for h in heads]
    s0 = [s_ref[h] for h in heads]
    lhs = [jnp.concatenate([uw[h][:, hd:], qs[h // rep].astype(F32) * egc[h]], axis=0) for h in heads]
    ws_qs = [_bdot(lhs[h], s0[h]) for h in heads]
    v_new = [(uw[h][:, :hd] - ws_qs[h][:c]).astype(BF16) for h in heads]
    o = [ws_qs[h][c:] + _bdot(qkd[h], v_new[h]) for h in heads]
    for h in heads:
        s_ref[h] = s0[h] * tail[h] + _bdot(kf[h] * eend[h], v_new[h], _TN)
    if not backward:
        for h in heads:
            o_ref[:, h * hd:(h + 1) * hd] = o[h]
    else:
        for h in heads:
            ot = o[h] + of_ref[:, h * hd:(h + 1) * hd]
            on = ot * lax.rsqrt(jnp.mean(ot * ot, -1, keepdims=True) + RMS_EPS) * ng_ref[...]
            zz = z_ref[:, h * hd:(h + 1) * hd].astype(F32)
            o_ref[:, h * hd:(h + 1) * hd] = (on * (zz * _sigmoid(zz))).astype(o_ref.dtype)


def deltanet_mix(lay, qkv, p, ab, a_log, dt_bias, norm_g):
    c = DN_CHUNK
    n = lay.n_chunks(c)
    n_tok = qkv.shape[0]
    nh = DN_V_HEADS
    neg_a = -jnp.exp(a_log.astype(F32))
    na = jnp.concatenate([neg_a, jnp.zeros_like(neg_a)], axis=1).reshape(1, 4 * nh)
    dt = jnp.concatenate([dt_bias.astype(F32), jnp.zeros_like(neg_a)], axis=1).reshape(1, 4 * nh)
    small = lambda a: pl.BlockSpec(a.shape, lambda b, i: (0, 0))
    consts = (na, dt)
    state = [pltpu.VMEM((nh, DN_HEAD_DIM, DN_HEAD_DIM), F32)]

    def specs(backward):
        rb = lambda b, i: lay.row_block(c, b, lay.seq_chunk(c, backward, i))
        return lambda w, off: pl.BlockSpec((c, w), lambda b, i: (rb(b, i), off))

    common = lambda blk: [blk(DN_QK_W, 0), blk(DN_QK_W, 1), blk(DN_V_W, 1), blk(4 * nh, 0)] + [small(a) for a in consts]
    blk = specs(False)
    o_f = pl.pallas_call(
        functools.partial(_dn_scan_kernel, False), grid=(lay.batch, n),
        in_specs=common(blk), out_specs=blk(DN_V_W, 0),
        out_shape=jax.ShapeDtypeStruct((n_tok, DN_V_W), F32),
        scratch_shapes=state, compiler_params=_params("arbitrary", "arbitrary"), name="dn_scan_fwd",
    )(qkv, qkv, qkv, ab, *consts)
    blk = specs(True)
    return pl.pallas_call(
        functools.partial(_dn_scan_kernel, True), grid=(lay.batch, n),
        in_specs=common(blk) + [blk(DN_V_W, 0), blk(DN_V_W, 2), small(norm_g.reshape(1, DN_HEAD_DIM))],
        out_specs=blk(DN_V_W, 0),
        out_shape=jax.ShapeDtypeStruct((n_tok, DN_V_W), BF16),
        scratch_shapes=state, compiler_params=_params("arbitrary", "arbitrary"), name="dn_scan_bwd",
    )(qkv, qkv, qkv, ab, *consts, o_f, p, norm_g.reshape(1, DN_HEAD_DIM))


def _rwkv_chunk_kernel(r_ref, v_ref, kk_ref, wl_ref, kd_ref, a_ref, o_ref, s_ref,
                       at_s, bt_s, kt_s, rt_s, be_s, ke_s, vb_s, gc_s):
    z = pl.program_id(0)
    i = pl.program_id(2)
    c = r_ref.shape[0]

    @pl.when(i == 0)
    def _():
        s_ref[...] = jnp.zeros_like(s_ref)

    row = lax.broadcasted_iota(jnp.int32, (c, c), 0)
    col = lax.broadcasted_iota(jnp.int32, (c, c), 1)
    lag = (row - col) * (1 - 2 * z)
    incl = lag >= 0
    strict = lag > 0

    wl = wl_ref[...]
    logw = -jnp.exp(-_softplus(-wl) - 0.5)
    cum = _hdot(incl.astype(F32), logw)
    c_last = jnp.sum(logw, axis=0, keepdims=True)
    kk = kk_ref[...].astype(F32)
    kb = kk * a_ref[...].astype(F32)
    kd = kd_ref[...].astype(F32)
    g_inv = jnp.exp(-cum)
    e_end = jnp.exp(c_last - cum)
    at_s[...] = (-kk * jnp.exp(cum - logw)).astype(BF16)
    bt_s[...] = (kb * g_inv).astype(BF16)
    kt_s[...] = (kd * g_inv).astype(BF16)
    rt_s[...] = (r_ref[...].astype(F32) * jnp.exp(cum)).astype(BF16)
    be_s[...] = (kb * e_end).astype(BF16)
    ke_s[...] = (kd * e_end).astype(BF16)
    vb_s[...] = v_ref[...].astype(BF16)
    gc_s[...] = jnp.exp(c_last)

    eye = (row == col).astype(F32)
    hh = range(RWKV_HEADS)
    sl = [slice(h * RWKV_HEAD, (h + 1) * RWKV_HEAD) for h in hh]
    a_ab = [jnp.where(strict, _bdot(at_s[:, sl[h]], bt_s[:, sl[h]], _NT), 0.0) for h in hh]
    a_ak = [jnp.where(strict, _bdot(at_s[:, sl[h]], kt_s[:, sl[h]], _NT), 0.0).astype(BF16) for h in hh]
    m_rb = [jnp.where(incl, _bdot(rt_s[:, sl[h]], bt_s[:, sl[h]], _NT), 0.0).astype(BF16) for h in hh]
    m_rk = [jnp.where(incl, _bdot(rt_s[:, sl[h]], kt_s[:, sl[h]], _NT), 0.0).astype(BF16) for h in hh]
    tm = [eye + a_ab[h] for h in hh]
    p = [a_ab[h].astype(BF16) for h in hh]
    for _ in range(int(math.log2(c)) - 1):
        p = [_bdot(p[h], p[h]).astype(BF16) for h in hh]
        tm = [tm[h] + _bdot(tm[h], p[h]) for h in hh]
    s0 = [s_ref[h] for h in hh]
    rhs = [_bdot(at_s[:, sl[h]], s0[h], _NT) + _bdot(a_ak[h], vb_s[:, sl[h]]) for h in hh]
    u = [_bdot(tm[h], rhs[h]).astype(BF16) for h in hh]
    for h in hh:
        o_ref[:, sl[h]] = _bdot(rt_s[:, sl[h]], s0[h], _NT) + _bdot(m_rb[h], u[h]) + _bdot(m_rk[h], vb_s[:, sl[h]])
    for h in hh:
        s_ref[h] = s0[h] * gc_s[:, sl[h]] + _bdot(u[h], be_s[:, sl[h]], _TN) + _bdot(vb_s[:, sl[h]], ke_s[:, sl[h]], _TN)


def rwkv_scan(lay, r, v, kk, wl, kd, a):
    n_tok, d = r.shape
    c = RWKV_CHUNK
    n = lay.n_chunks(c)

    def rb(z, b, i):
        sc = jnp.where(z == 0, lay.seq_chunk(c, False, i), lay.seq_chunk(c, True, i))
        return lay.row_block(c, b, sc)

    shared = pl.BlockSpec((c, d), lambda z, b, i: (rb(z, b, i), 0))
    perdir = pl.BlockSpec((None, c, d), lambda z, b, i: (z, rb(z, b, i), 0))
    bf = lambda: pltpu.VMEM((c, d), BF16)
    return pl.pallas_call(
        _rwkv_chunk_kernel,
        grid=(2, lay.batch, n),
        in_specs=[shared, shared, shared, perdir, perdir, perdir],
        out_specs=perdir,
        out_shape=jax.ShapeDtypeStruct((2, n_tok, d), F32),
        scratch_shapes=[pltpu.VMEM((RWKV_HEADS, RWKV_HEAD, RWKV_HEAD), F32),
                        bf(), bf(), bf(), bf(), bf(), bf(), bf(), pltpu.VMEM((1, d), F32)],
        compiler_params=_params("arbitrary", "arbitrary", "arbitrary"),
        name="rwkv_scan",
    )(r, v, kk, wl, kd, a)


def _rwkv_pre_kernel(lay, tm, prev_ref, h_ref, next_ref, mix_ref, wr_ref, wk_ref, wv_ref, w1_ref, a1_ref, g1_ref,
                     w2_ref, a2_ref, g2_ref, w0_ref, a0_ref, kk_ref, ka_ref, rk_ref, seg_ref,
                     r_out, v_out, kk_out, wl_out, kd_out, a_out, g_out, bonus_out, xe_s):
    j = pl.program_id(0)
    r0 = j * tm
    in_ctx = r0 < lay.ctx_tok
    seq_len = jnp.where(in_ctx, lay.n_ctx, lay.seq)
    off = jnp.where(in_ctx, r0, r0 - lay.ctx_tok) % seq_len
    hal = SEQ_HALO
    xe_s[0:hal, :] = jnp.where(off == 0, 0.0, prev_ref[...].astype(F32))
    xe_s[hal:hal + tm, :] = h_ref[...].astype(F32)
    xe_s[hal + tm:, :] = jnp.where(off + tm == seq_len, 0.0, next_ref[...].astype(F32))
    h = xe_s[pl.ds(hal, tm), :]
    xx = 0.5 * (xe_s[pl.ds(hal - 1, tm), :] + xe_s[pl.ds(hal + 1, tm), :]) - h
    xm = lambda i: (h + xx * mix_ref[i:i + 1, :]).astype(BF16)
    r = _bdot(xm(0), wr_ref[...])
    k = _bdot(xm(1), wk_ref[...])
    v = _bdot(xm(2), wv_ref[...])
    hw = jnp.tanh(_bdot(xm(3), w1_ref[...]))
    ha = _bdot(xm(4), a1_ref[...])
    hg = _sigmoid(_bdot(xm(5), g1_ref[...]))
    seg = seg_ref[...]
    kx = k * kk_ref[...]
    r_out[...] = r.astype(r_out.dtype)
    v_out[...] = v.astype(v_out.dtype)
    kk_out[...] = (kx * lax.rsqrt(_bdot(kx * kx, seg) + L2_EPS)).astype(kk_out.dtype)
    g_out[...] = _bdot(hg, g2_ref[...]).astype(g_out.dtype)
    lw = w2_ref.shape[1]
    rr = r * rk_ref[...]
    bsum = None
    for z in range(2):
        wl_out[z] = w0_ref[z:z + 1, :] + _bdot(hw[:, z * lw:(z + 1) * lw], w2_ref[z])
        a = _sigmoid(a0_ref[z:z + 1, :] + _bdot(ha[:, z * lw:(z + 1) * lw], a2_ref[z]))
        kd = k * (1.0 + (a - 1.0) * ka_ref[...])
        a_out[z] = a.astype(a_out.dtype)
        kd_out[z] = kd.astype(kd_out.dtype)
        bsum = rr * kd if bsum is None else bsum + rr * kd
    bonus_out[...] = (_bdot(bsum, seg) * v).astype(bonus_out.dtype)


def _rwkv_post_kernel(o_ref, g_ref, bonus_ref, lnx_ref, seg_ref, a_out):
    seg = seg_ref[...]
    o = o_ref[0] + o_ref[1]
    hi = o.astype(BF16)
    lo = o - hi.astype(F32)
    inv_n = 1.0 / RWKV_HEAD
    oc = o - (_bdot(hi, seg) + _bdot(lo, seg)) * inv_n
    var = _bdot(oc * oc, seg) * inv_n
    on = oc * lax.rsqrt(var + LNX_EPS) * lnx_ref[...] + bonus_ref[...].astype(F32)
    a_out[...] = (on * g_ref[...].astype(F32)).astype(a_out.dtype)


def rwkv7_mix(lay, h, mix, w_rkv, w0, w1, w2, a0, a1, a2, g1, g2, k_k, k_a, r_k, lnx_g):
    n_tok, d = h.shape
    bw = lambda w: w.astype(BF16)
    tm = math.gcd(256, lay.n_ctx, lay.seq)
    hb = tm // SEQ_HALO
    last_hb = n_tok // SEQ_HALO - 1
    head_of = jnp.arange(d) // RWKV_HEAD
    seg = (head_of[:, None] == head_of[None, :]).astype(BF16)
    full = lambda a: pl.BlockSpec(a.shape, lambda j: (0,) * a.ndim)
    row = lambda a: a.reshape(1, d)
    consts = (mix, bw(w_rkv[0]), bw(w_rkv[1]), bw(w_rkv[2]), bw(jnp.concatenate([w1[0], w1[1]], -1)),
              bw(jnp.concatenate([a1[0], a1[1]], -1)), bw(g1), bw(w2), bw(a2), bw(g2), w0, a0,
              row(k_k), row(k_a), row(r_k), seg)
    tok = pl.BlockSpec((tm, d), lambda j: (j, 0))
    tok2 = pl.BlockSpec((2, tm, d), lambda j: (0, j, 0))
    one = lambda dt: jax.ShapeDtypeStruct((n_tok, d), dt)
    two = lambda dt: jax.ShapeDtypeStruct((2, n_tok, d), dt)
    r, v, kk, wl, kd, a, g, bonus = pl.pallas_call(
        functools.partial(_rwkv_pre_kernel, lay, tm),
        grid=(n_tok // tm,),
        in_specs=[pl.BlockSpec((SEQ_HALO, d), lambda j: (jnp.maximum(j * hb - 1, 0), 0)), tok,
                  pl.BlockSpec((SEQ_HALO, d), lambda j: (jnp.minimum((j + 1) * hb, last_hb), 0))]
                 + [full(a) for a in consts],
        out_specs=[tok, tok, tok, tok2, tok2, tok2, tok, tok],
        out_shape=[one(BF16), one(BF16), one(BF16), two(F32), two(BF16), two(BF16), one(BF16), one(BF16)],
        scratch_shapes=[pltpu.VMEM((tm + 2 * SEQ_HALO, d), F32)],
        compiler_params=_params("arbitrary"), name="rwkv_pre",
    )(h, h, h, *consts)
    o = rwkv_scan(lay, r, v, kk, wl, kd, a)
    return pl.pallas_call(
        _rwkv_post_kernel,
        grid=(n_tok // tm,),
        in_specs=[tok2, tok, tok, full(row(lnx_g)), full(seg)],
        out_specs=tok,
        out_shape=one(BF16),
        compiler_params=_params("arbitrary"), name="rwkv_post",
    )(o, g, bonus, row(lnx_g), seg)


def _moe_kernel(be_ref, nb_ref, x_ref, wg_ref, wu_ref, wd_ref, gate_ref, o_ref, acc_s):
    j = pl.program_id(0)
    f = pl.program_id(1)

    @pl.when(f == 0)
    def _():
        acc_s[...] = jnp.zeros_like(acc_s)

    @pl.when(j < nb_ref[0])
    def _():
        x = x_ref[...]
        g = _bdot(x, wg_ref[...])
        u = _bdot(x, wu_ref[...])
        acc_s[...] += _bdot(g * _sigmoid(g) * u, wd_ref[...])

    @pl.when(f == pl.num_programs(1) - 1)
    def _():
        o_ref[...] = acc_s[...] * gate_ref[...]


def moe_experts(xb, block_e, n_used, w_gu, w_down, slot_gate):
    n_slots, d = xb.shape
    bm = MOE_ROWS
    nf = MOE_F_CHUNKS
    tf = w_down.shape[1] // nf
    return pl.pallas_call(
        _moe_kernel,
        grid_spec=pltpu.PrefetchScalarGridSpec(
            num_scalar_prefetch=2,
            grid=(n_slots // bm, nf),
            in_specs=[pl.BlockSpec((bm, d), lambda j, f, be, nb: (j, 0)),
                      pl.BlockSpec((None, d, tf), lambda j, f, be, nb: (be[j], 0, f)),
                      pl.BlockSpec((None, d, tf), lambda j, f, be, nb: (be[j], 0, f + nf)),
                      pl.BlockSpec((None, tf, d), lambda j, f, be, nb: (be[j], f, 0)),
                      pl.BlockSpec((bm, 1), lambda j, f, be, nb: (j, 0))],
            out_specs=pl.BlockSpec((bm, d), lambda j, f, be, nb: (j, 0)),
            scratch_shapes=[pltpu.VMEM((bm, d), F32)]),
        out_shape=jax.ShapeDtypeStruct((n_slots, d), F32),
        compiler_params=_params("arbitrary", "arbitrary"),
        name="moe_experts",
    )(block_e, n_used, xb, w_gu, w_gu, w_down, slot_gate)


def moe_swiglu(h, w_router, w_gu, w_down):
    n, d = h.shape
    logits = proj(h, w_router.astype(BF16), F32, N_EXPERTS)
    top_logit, top_e = lax.top_k(logits, TOP_K)
    gate = jax.nn.softmax(top_logit, axis=-1)
    flat_e = top_e.reshape(-1).astype(jnp.int32)
    order = jnp.argsort(flat_e).astype(jnp.int32)
    onehot = (flat_e[:, None] == jnp.arange(N_EXPERTS, dtype=jnp.int32)).astype(jnp.int32)
    seen = jnp.cumsum(onehot, axis=0)
    counts = seen[-1]
    rank = jnp.sum(seen * onehot, axis=1) - 1
    padded = (counts + MOE_ROWS - 1) // MOE_ROWS * MOE_ROWS
    start = jnp.cumsum(counts) - counts
    pend = jnp.cumsum(padded)
    pstart = pend - padded
    n_slots = (n * TOP_K + MOE_ROWS - 1) // MOE_ROWS * MOE_ROWS + N_EXPERTS * MOE_ROWS
    n_blocks = n_slots // MOE_ROWS
    block_e = jnp.minimum(jnp.sum(jnp.arange(n_blocks)[:, None] * MOE_ROWS >= pend[None, :], axis=1),
                          N_EXPERTS - 1).astype(jnp.int32)
    n_used = (pend[-1] // MOE_ROWS).astype(jnp.int32).reshape(1)
    slot_e = jnp.repeat(block_e, MOE_ROWS)
    slot_off = jnp.arange(n_slots, dtype=jnp.int32) - pstart[slot_e]
    slot_valid = slot_off < counts[slot_e]
    slot_asg = order[jnp.clip(start[slot_e] + slot_off, 0, n * TOP_K - 1)]
    slot_tok = jnp.where(slot_valid, slot_asg // TOP_K, n)
    slot_gate = jnp.where(slot_valid, gate.reshape(-1)[slot_asg], 0.0)
    x_pad = jnp.concatenate([h, jnp.zeros((1, d), h.dtype)], 0)
    yb = moe_experts(x_pad[slot_tok], block_e, n_used, w_gu, w_down, slot_gate[:, None])
    tok_slot = (pstart[flat_e] + rank).reshape(n, TOP_K)
    return yb[tok_slot[:, 0]] + yb[tok_slot[:, 1]]


def kernel(x, c, ctx, c_ctx, mod_w, mod_b, ln_g, ln_b,
           ret_w_in, ret_decay, ret_gn_g, ret_w_out,
           dn_w_in, dn_conv_w, dn_a_log, dn_dt_bias, dn_norm_g, dn_w_out,
           rk_mix, rk_w_rkv, rk_w0, rk_w1, rk_w2, rk_a0, rk_a1, rk_a2, rk_g1, rk_g2,
           rk_k_k, rk_k_a, rk_r_k, rk_lnx_g, rk_w_out,
           ffn_w_gu, ffn_w_down, moe_router, moe_w_gu, moe_w_down):
    bsz, t, d = x.shape
    n_ctx = ctx.shape[1]
    lay = Layout(bsz, n_ctx, t)
    bw = lambda w: w.astype(BF16)
    s_rows = jax.nn.silu(jnp.concatenate([c_ctx[None], c], 0))
    s_pad = jnp.zeros((8, d), F32).at[:1 + bsz].set(s_rows)
    mods = modulation_rows(s_pad, mod_w, mod_b)[:, :1 + bsz].reshape(DEPTH, 1 + bsz, 6, 1, d)
    mod = lambda i, k: mods[i, :, k]
    cos_t, sin_t = rope_tables(lay)
    xs = jnp.concatenate([ctx.reshape(-1, d), x.reshape(-1, d)], 0)
    h = modulate(lay, xs, mod(0, 1), mod(0, 0))
    for i in range(DEPTH):
        last = i == DEPTH - 1
        kind, j = i % N_MIXERS, i // N_MIXERS
        if kind == 0:
            p = proj(h, bw(ret_w_in[j]), BF16, 1024)
            log_gamma = jax.nn.log_sigmoid(ret_decay[j].astype(F32))
            a = retention_mix(lay, p, log_gamma, ret_gn_g[j], cos_t, sin_t)
            w_out = ret_w_out[j]
        elif kind == 1:
            n_main = 2 * DN_QK_W + 2 * DN_V_W
            p = proj(h, bw(dn_w_in[j][:, :n_main]), BF16, 1024)
            w_ab = bw(dn_w_in[j][:, n_main:])
            ab = proj(h, w_ab, F32, w_ab.shape[1])
            qkv = dn_prep(lay, p, dn_conv_w[j])
            a = deltanet_mix(lay, qkv, p, ab, dn_a_log[j], dn_dt_bias[j], dn_norm_g[j])
            w_out = dn_w_out[j]
        else:
            a = rwkv7_mix(lay, h, rk_mix[j], rk_w_rkv[j], rk_w0[j], rk_w1[j], rk_w2[j], rk_a0[j], rk_a1[j],
                          rk_a2[j], rk_g1[j], rk_g2[j], rk_k_k[j], rk_k_a[j], rk_r_k[j], rk_lnx_g[j])
            w_out = rk_w_out[j]
        xs, h = out_ln(lay, a, bw(w_out), xs, mod(i, 2), ln_g[i, 0], ln_b[i, 0], mod(i, 4), mod(i, 3))
        nxt = (i + 1) % DEPTH
        if i % 2 == 0:
            hm = swiglu_in(h, bw(ffn_w_gu[i // 2]), 256)
            xs, h = out_ln(lay, hm, bw(ffn_w_down[i // 2]), xs, mod(i, 5), ln_g[i, 1], ln_b[i, 1],
                           mod(nxt, 1), mod(nxt, 0))
        else:
            f = moe_swiglu(h, moe_router[i // 2], moe_w_gu[i // 2], moe_w_down[i // 2])
            xs, h = out_ln(lay, f, None, xs, mod(i, 5), ln_g[i, 1], ln_b[i, 1], mod(nxt, 1), mod(nxt, 0))
    return xs[lay.ctx_tok:].reshape(bsz, t, d)
```

```python
import math, functools
import jax
import jax.numpy as jnp
from jax import lax
import numpy as np
from jax.experimental import pallas as pl
from jax.experimental.pallas import tpu as pltpu

D_MODEL = 1024
DEPTH = 4
GRID_W = 64
N_MIXERS = 3
ALPHA = (2 * DEPTH) ** 0.25
LN_EPS = 1e-5
GN_EPS = 1e-5
RMS_EPS = 1e-6
LNX_EPS = 64e-5
L2_EPS = 1e-6

RET_HEADS = 4
RET_DK = D_MODEL // RET_HEADS
RET_DV = 2 * RET_DK
RET_CHUNK = 128
ROPE_BASE = 10000.0

DN_QK_HEADS = 8
DN_V_HEADS = 16
DN_HEAD_DIM = 128
DN_CHUNK = 64
DN_CONV = 5
DN_QK_W = DN_QK_HEADS * DN_HEAD_DIM
DN_V_W = DN_V_HEADS * DN_HEAD_DIM
SEQ_HALO = 16

RWKV_HEAD = 64
RWKV_HEADS = D_MODEL // RWKV_HEAD
RWKV_CHUNK = 64

FFN_DIM = 2816
N_EXPERTS = 8
TOP_K = 2
EXPERT_DIM = 3584
MOE_ROWS = 1024
MOE_F_CHUNKS = 7

ROW_TILE = 512
VMEM_LIMIT = 48 * 1024 * 1024

BF16 = jnp.bfloat16
F32 = jnp.float32
_NT = (((1,), (1,)), ((), ()))
_TN = (((0,), (0,)), ((), ()))


def _bdot(x, y, dims=None):
    x = x.astype(BF16)
    y = y.astype(BF16)
    if dims is None:
        return jnp.dot(x, y, preferred_element_type=F32)
    return lax.dot_general(x, y, dims, preferred_element_type=F32)


def _hdot(x, y):
    return jnp.dot(x, y, precision=lax.Precision.HIGHEST, preferred_element_type=F32)


def _sigmoid(x):
    return 1.0 / (1.0 + jnp.exp(-x))


def _softplus(x):
    return jnp.maximum(x, 0.0) + jnp.log(1.0 + jnp.exp(-jnp.abs(x)))


def _params(*sem):
    return pltpu.CompilerParams(dimension_semantics=sem, vmem_limit_bytes=VMEM_LIMIT)


class Layout:
    def __init__(self, batch, n_ctx, seq):
        self.batch, self.n_ctx, self.seq = batch, n_ctx, seq
        self.ctx_tok = batch * n_ctx
        self.n_tok = self.ctx_tok + batch * seq
        self.row_tile = math.gcd(ROW_TILE, n_ctx * batch, seq)

    def mod_index(self, tile, j):
        r0 = j * tile
        return jnp.where(r0 < self.ctx_tok, 0, 1 + (r0 - self.ctx_tok) // self.seq)

    def seq_chunk(self, chunk, backward, i):
        nc, nl = self.n_ctx // chunk, self.seq // chunk
        if not backward:
            return i
        return jnp.where(i < nc, nc - 1 - i, 2 * nc + nl - 1 - i)

    def row_block(self, chunk, b, sc):
        nc, nl = self.n_ctx // chunk, self.seq // chunk
        return jnp.where(sc < nc, b * nc + sc, self.batch * nc + b * nl + sc - nc)

    def n_chunks(self, chunk):
        return (self.n_ctx + self.seq) // chunk


def _proj_kernel(h_ref, w_ref, o_ref):
    o_ref[...] = _bdot(h_ref[...], w_ref[...]).astype(o_ref.dtype)


def proj(h, w, out_dtype, tn):
    n_tok, k = h.shape
    n = w.shape[1]
    tm = math.gcd(1024, n_tok)
    return pl.pallas_call(
        _proj_kernel,
        grid=(n // tn, n_tok // tm),
        in_specs=[pl.BlockSpec((tm, k), lambda c, j: (j, 0)), pl.BlockSpec((k, tn), lambda c, j: (0, c))],
        out_specs=pl.BlockSpec((tm, tn), lambda c, j: (j, c)),
        out_shape=jax.ShapeDtypeStruct((n_tok, n), out_dtype),
        compiler_params=_params("arbitrary", "arbitrary"),
        name="proj",
    )(h, w)


def _swiglu_in_kernel(h_ref, wg_ref, wu_ref, o_ref):
    h = h_ref[...]
    g = _bdot(h, wg_ref[...])
    u = _bdot(h, wu_ref[...])
    o_ref[...] = (g * _sigmoid(g) * u).astype(o_ref.dtype)


def swiglu_in(h, w_gu, tn):
    n_tok, k = h.shape
    f = w_gu.shape[1] // 2
    tm = math.gcd(1024, n_tok)
    nf = f // tn
    return pl.pallas_call(
        _swiglu_in_kernel,
        grid=(nf, n_tok // tm),
        in_specs=[pl.BlockSpec((tm, k), lambda c, j: (j, 0)),
                  pl.BlockSpec((k, tn), lambda c, j: (0, c)),
                  pl.BlockSpec((k, tn), lambda c, j: (0, c + nf))],
        out_specs=pl.BlockSpec((tm, tn), lambda c, j: (j, c)),
        out_shape=jax.ShapeDtypeStruct((n_tok, f), BF16),
        compiler_params=_params("arbitrary", "arbitrary"),
        name="swiglu_in",
    )(h, w_gu, w_gu)


def _deepnorm_epilogue(x, f, ga_ref, g_ref, b_ref, sc_ref, sh_ref, x_out, h_out):
    y = ALPHA * x + (1.0 + ga_ref[...]) * f
    mu = jnp.mean(y, -1, keepdims=True)
    yc = y - mu
    var = jnp.mean(yc * yc, -1, keepdims=True)
    xn = yc * lax.rsqrt(var + LN_EPS) * g_ref[...] + b_ref[...]
    x_out[...] = xn
    h_out[...] = (xn * (1.0 + sc_ref[...]) + sh_ref[...]).astype(h_out.dtype)


def _out_ln_kernel(a_ref, w_ref, x_ref, ga_ref, g_ref, b_ref, sc_ref, sh_ref, x_out, h_out):
    _deepnorm_epilogue(x_ref[...], _bdot(a_ref[...], w_ref[...]), ga_ref, g_ref, b_ref, sc_ref, sh_ref, x_out, h_out)


def _resid_ln_kernel(f1_ref, f2_ref, x_ref, ga_ref, g_ref, b_ref, sc_ref, sh_ref, x_out, h_out):
    _deepnorm_epilogue(x_ref[...], f1_ref[...] + f2_ref[...], ga_ref, g_ref, b_ref, sc_ref, sh_ref, x_out, h_out)


def out_ln(lay, a, w, x, gate, ln_g, ln_b, sc_next, sh_next):
    n_tok, d = x.shape
    tm = lay.row_tile
    row = lambda j: (j, 0)
    mod = pl.BlockSpec((None, 1, d), lambda j: (lay.mod_index(tm, j), 0, 0))
    vec = pl.BlockSpec((1, d), lambda j: (0, 0))
    tok = pl.BlockSpec((tm, d), row)
    if w is None:
        body, lhs, lhs_specs = _resid_ln_kernel, tuple(a), [tok, tok]
    else:
        k = a.shape[1]
        body, lhs = _out_ln_kernel, (a, w)
        lhs_specs = [pl.BlockSpec((tm, k), row), pl.BlockSpec((k, d), lambda j: (0, 0))]
    return pl.pallas_call(
        body,
        grid=(n_tok // tm,),
        in_specs=lhs_specs + [tok, mod, vec, vec, mod, mod],
        out_specs=[tok, tok],
        out_shape=[jax.ShapeDtypeStruct((n_tok, d), F32), jax.ShapeDtypeStruct((n_tok, d), BF16)],
        compiler_params=_params("arbitrary"),
        name="out_ln",
    )(*lhs, x, gate, ln_g.reshape(1, d), ln_b.reshape(1, d), sc_next, sh_next)


def _modulate_kernel(x_ref, sc_ref, sh_ref, h_out):
    h_out[...] = (x_ref[...] * (1.0 + sc_ref[...]) + sh_ref[...]).astype(h_out.dtype)


def modulate(lay, x, sc, sh):
    n_tok, d = x.shape
    tm = lay.row_tile
    mod = pl.BlockSpec((None, 1, d), lambda j: (lay.mod_index(tm, j), 0, 0))
    tok = pl.BlockSpec((tm, d), lambda j: (j, 0))
    return pl.pallas_call(
        _modulate_kernel, grid=(n_tok // tm,), in_specs=[tok, mod, mod], out_specs=tok,
        out_shape=jax.ShapeDtypeStruct((n_tok, d), BF16), compiler_params=_params("arbitrary"), name="modulate",
    )(x, sc, sh)


def _mod_kernel(s_ref, w_ref, b_ref, o_ref):
    o_ref[...] = _bdot(s_ref[...], w_ref[...]) + b_ref[...]


def modulation_rows(s, mod_w, mod_b):
    r, d = s.shape
    depth, _, n = mod_w.shape
    tn = 1024
    return pl.pallas_call(
        _mod_kernel,
        grid=(depth, n // tn),
        in_specs=[pl.BlockSpec((r, d), lambda i, c: (0, 0)),
                  pl.BlockSpec((None, d, tn), lambda i, c: (i, 0, c)),
                  pl.BlockSpec((None, 1, tn), lambda i, c: (i, 0, c))],
        out_specs=pl.BlockSpec((None, r, tn), lambda i, c: (i, 0, c)),
        out_shape=jax.ShapeDtypeStruct((depth, r, n), F32),
        compiler_params=_params("arbitrary", "arbitrary"),
        name="modulation_rows",
    )(s, mod_w, mod_b.reshape(depth, 1, n))


def _rope(x, cos, sin):
    half = x.shape[1] // 2
    parts = []
    for p in range(2):
        xs = x[:, p * half:(p + 1) * half]
        parts.append(xs * cos[:, p * half:(p + 1) * half]
                     + pltpu.roll(xs, half // 2, axis=1) * sin[:, p * half:(p + 1) * half])
    return jnp.concatenate(parts, axis=1)


def _ret_heads(q_ref, k_ref, v_ref, cos_ref, sin_ref):
    cos, sin = cos_ref[...], sin_ref[...]
    hs = range(RET_HEADS)
    q = [_rope(q_ref[:, h * RET_DK:(h + 1) * RET_DK].astype(F32), cos, sin) for h in hs]
    k = [_rope(k_ref[:, h * RET_DK:(h + 1) * RET_DK].astype(F32), cos, sin) * (RET_DK ** -0.5) for h in hs]
    v = [v_ref[:, h * RET_DV:(h + 1) * RET_DV] for h in hs]
    return hs, q, k, v


def _ret_fwd_kernel(lg_ref, q_ref, k_ref, v_ref, cos_ref, sin_ref, o_ref, s_ref):
    i = pl.program_id(1)
    c = q_ref.shape[0]

    @pl.when(i == 0)
    def _():
        s_ref[...] = jnp.zeros_like(s_ref)

    hs, q, k, v = _ret_heads(q_ref, k_ref, v_ref, cos_ref, sin_ref)
    row = lax.broadcasted_iota(jnp.int32, (c, c), 0)
    col = lax.broadcasted_iota(jnp.int32, (c, c), 1)
    lag = (row - col).astype(F32)
    pos = lax.broadcasted_iota(jnp.int32, (c, RET_DK), 0).astype(F32)
    full_c = jnp.full((1, 1), c, F32)
    decay = [jnp.where(lag >= 0, jnp.exp(jnp.maximum(lag, 0.0) * lg_ref[0, h]), 0.0)
             + jnp.where(lag <= 0, jnp.exp(jnp.maximum(-lag, 0.0) * lg_ref[1, h]), 0.0) for h in hs]
    scores = [(_bdot(q[h], k[h], _NT) * decay[h]).astype(BF16) for h in hs]
    s = [s_ref[h] for h in hs]
    qd = [(q[h] * jnp.exp((pos + 1.0) * lg_ref[0, h])).astype(BF16) for h in hs]
    kd = [(k[h] * jnp.exp((c - 1.0 - pos) * lg_ref[0, h])).astype(BF16) for h in hs]
    for h in hs:
        o_ref[:, h * RET_DV:(h + 1) * RET_DV] = _bdot(scores[h], v[h]) + _bdot(qd[h], s[h])
    for h in hs:
        s_ref[h] = s[h] * jnp.exp(full_c * lg_ref[0, h]) + _bdot(kd[h], v[h], _TN)


def _ret_bwd_kernel(lg_ref, q_ref, k_ref, v_ref, g_ref, cos_ref, sin_ref, op_ref, gn_ref, o_ref, s_ref):
    i = pl.program_id(1)
    c = q_ref.shape[0]

    @pl.when(i == 0)
    def _():
        s_ref[...] = jnp.zeros_like(s_ref)

    hs, q, k, v = _ret_heads(q_ref, k_ref, v_ref, cos_ref, sin_ref)
    pos = lax.broadcasted_iota(jnp.int32, (c, RET_DK), 0).astype(F32)
    full_c = jnp.full((1, 1), c, F32)
    s = [s_ref[h] for h in hs]
    qd = [(q[h] * jnp.exp((c - pos) * lg_ref[1, h])).astype(BF16) for h in hs]
    kd = [(k[h] * jnp.exp(pos * lg_ref[1, h])).astype(BF16) for h in hs]
    o = [op_ref[:, h * RET_DV:(h + 1) * RET_DV] + _bdot(qd[h], s[h]) for h in hs]
    for h in hs:
        s_ref[h] = s[h] * jnp.exp(full_c * lg_ref[1, h]) + _bdot(kd[h], v[h], _TN)
    for h in hs:
        hv = slice(h * RET_DV, (h + 1) * RET_DV)
        mu = jnp.mean(o[h], -1, keepdims=True)
        oc = o[h] - mu
        var = jnp.mean(oc * oc, -1, keepdims=True)
        g = g_ref[:, hv].astype(F32)
        o_ref[:, hv] = (g * _sigmoid(g) * (oc * lax.rsqrt(var + GN_EPS) * gn_ref[:, hv])).astype(o_ref.dtype)


def retention_mix(lay, p, log_gamma, gn_g, cos_t, sin_t):
    c = RET_CHUNK
    n = lay.n_chunks(c)
    n_tok = p.shape[0]
    hk = RET_HEADS * RET_DK
    hv = RET_HEADS * RET_DV

    def specs(backward):
        sc = lambda i: lay.seq_chunk(c, backward, i)
        blk = lambda w, off: pl.BlockSpec((c, w), lambda b, i: (lay.row_block(c, b, sc(i)), off))
        tab = pl.BlockSpec((c, RET_DK), lambda b, i: (sc(i), 0))
        return blk, tab

    smem = pl.BlockSpec(memory_space=pltpu.SMEM)
    grid = (lay.batch, n)
    state = [pltpu.VMEM((RET_HEADS, RET_DK, RET_DV), F32)]
    blk, tab = specs(False)
    o_part = pl.pallas_call(
        _ret_fwd_kernel, grid=grid,
        in_specs=[smem, blk(hk, 0), blk(hk, 1), blk(hv, 1), tab, tab],
        out_specs=blk(hv, 0),
        out_shape=jax.ShapeDtypeStruct((n_tok, hv), F32),
        scratch_shapes=state, compiler_params=_params("arbitrary", "arbitrary"), name="ret_fwd",
    )(log_gamma, p, p, p, cos_t, sin_t)
    blk, tab = specs(True)
    return pl.pallas_call(
        _ret_bwd_kernel, grid=grid,
        in_specs=[smem, blk(hk, 0), blk(hk, 1), blk(hv, 1), blk(hv, 2), tab, tab,
                  blk(hv, 0), pl.BlockSpec((1, hv), lambda b, i: (0, 0))],
        out_specs=blk(hv, 0),
        out_shape=jax.ShapeDtypeStruct((n_tok, hv), BF16),
        scratch_shapes=state, compiler_params=_params("arbitrary", "arbitrary"), name="ret_bwd",
    )(log_gamma, p, p, p, p, cos_t, sin_t, o_part, gn_g.reshape(1, hv))


def rope_tables(lay):
    quarter = RET_DK // 4
    t = jnp.arange(lay.seq)
    inv = ROPE_BASE ** (-jnp.arange(quarter, dtype=F32) / quarter)
    ang_r = (t // GRID_W).astype(F32)[:, None] * inv
    ang_c = (t % GRID_W).astype(F32)[:, None] * inv
    cos = jnp.concatenate([jnp.cos(ang_r)] * 2 + [jnp.cos(ang_c)] * 2, -1)
    sin = jnp.concatenate([-jnp.sin(ang_r), jnp.sin(ang_r), -jnp.sin(ang_c), jnp.sin(ang_c)], -1)
    cos = jnp.concatenate([jnp.ones((lay.n_ctx, RET_DK), F32), cos], 0)
    sin = jnp.concatenate([jnp.zeros((lay.n_ctx, RET_DK), F32), sin], 0)
    return cos, sin


def _dn_prep_kernel(lay, tm, prev_ref, x_ref, next_ref, w_ref, o_ref, xe_s):
    j = pl.program_id(0)
    ct = pl.program_id(1)
    r0 = j * tm
    in_ctx = r0 < lay.ctx_tok
    seq_len = jnp.where(in_ctx, lay.n_ctx, lay.seq)
    off = jnp.where(in_ctx, r0, r0 - lay.ctx_tok) % seq_len
    first = off == 0
    last = off + tm == seq_len
    hal = SEQ_HALO
    xe_s[0:hal, :] = jnp.where(first, 0.0, prev_ref[...].astype(F32))
    xe_s[hal:hal + tm, :] = x_ref[...].astype(F32)
    xe_s[hal + tm:, :] = jnp.where(last, 0.0, next_ref[...].astype(F32))
    pad = (DN_CONV - 1) // 2
    acc = xe_s[pl.ds(hal - pad, tm), :] * w_ref[0:1, :]
    for d in range(1, DN_CONV):
        acc = acc + xe_s[pl.ds(hal - pad + d, tm), :] * w_ref[d:d + 1, :]
    y = acc * _sigmoid(acc)
    n_qk_tiles = 2 * DN_QK_W // x_ref.shape[1]

    @pl.when(ct >= n_qk_tiles)
    def _():
        o_ref[...] = y.astype(o_ref.dtype)

    @pl.when(ct < n_qk_tiles)
    def _():
        is_q = ct * x_ref.shape[1] < DN_QK_W
        scale = jnp.where(is_q, DN_HEAD_DIM ** -0.5, 1.0)
        for s in range(x_ref.shape[1] // DN_HEAD_DIM):
            ys = y[:, s * DN_HEAD_DIM:(s + 1) * DN_HEAD_DIM]
            inv = lax.rsqrt(jnp.sum(ys * ys, -1, keepdims=True) + L2_EPS) * scale
            o_ref[:, s * DN_HEAD_DIM:(s + 1) * DN_HEAD_DIM] = (ys * inv).astype(o_ref.dtype)


def dn_prep(lay, p, conv_w):
    n_tok = p.shape[0]
    w = 2 * DN_QK_W + DN_V_W
    tm = math.gcd(256, lay.n_ctx, lay.seq)
    tc = 1024
    hb = tm // SEQ_HALO
    last_hb = n_tok // SEQ_HALO - 1
    return pl.pallas_call(
        functools.partial(_dn_prep_kernel, lay, tm),
        grid=(n_tok // tm, w // tc),
        in_specs=[pl.BlockSpec((SEQ_HALO, tc), lambda j, c: (jnp.maximum(j * hb - 1, 0), c)),
                  pl.BlockSpec((tm, tc), lambda j, c: (j, c)),
                  pl.BlockSpec((SEQ_HALO, tc), lambda j, c: (jnp.minimum((j + 1) * hb, last_hb), c)),
                  pl.BlockSpec((DN_CONV, tc), lambda j, c: (0, c))],
        out_specs=pl.BlockSpec((tm, tc), lambda j, c: (j, c)),
        out_shape=jax.ShapeDtypeStruct((n_tok, w), BF16),
        scratch_shapes=[pltpu.VMEM((tm + 2 * SEQ_HALO, tc), F32)],
        compiler_params=_params("arbitrary", "arbitrary"), name="dn_prep",
    )(p, p, p, conv_w)


def _dn_scan_kernel(backward, q_ref, k_ref, v_ref, ab_ref, na_row, dt_row, *rest):
    if backward:
        of_ref, z_ref, ng_ref, o_ref, s_ref = rest
    else:
        o_ref, s_ref = rest
    i = pl.program_id(1)
    c = q_ref.shape[0]
    z = 1 if backward else 0
    nh = DN_V_HEADS
    hd = DN_HEAD_DIM
    rep = DN_V_HEADS // DN_QK_HEADS

    @pl.when(i == 0)
    def _():
        s_ref[...] = jnp.zeros_like(s_ref)

    row = lax.broadcasted_iota(jnp.int32, (c, c), 0)
    col = lax.broadcasted_iota(jnp.int32, (c, c), 1)
    lag = (col - row) if backward else (row - col)
    incl = lag >= 0
    strict = lag > 0
    eye = (row == col).astype(F32)
    ab = ab_ref[...]
    g_cols = na_row[...] * _softplus(ab + dt_row[...])
    beta_cols = _sigmoid(ab)
    gc_cols = _hdot(incl.astype(F32), g_cols)
    gc_rows = lax.dot_general(gc_cols, eye, _TN, precision=lax.Precision.HIGHEST,
                              preferred_element_type=F32)
    gend = jnp.sum(g_cols, axis=0, keepdims=True)
    heads = range(nh)
    gi = [z * 2 * nh + h for h in heads]
    bi = [z * 2 * nh + nh + h for h in heads]
    gcc = [gc_cols[:, gi[h]:gi[h] + 1] for h in heads]
    dec = [jnp.where(incl, jnp.exp(jnp.minimum(gcc[h] - gc_rows[gi[h]:gi[h] + 1, :], 0.0)), 0.0) for h in heads]
    beta = [beta_cols[:, bi[h]:bi[h] + 1] for h in heads]
    egc = [jnp.exp(gcc[h]) for h in heads]
    eend = [jnp.exp(gend[:, gi[h]:gi[h] + 1] - gcc[h]) for h in heads]
    tail = [jnp.exp(gend[:, gi[h]:gi[h] + 1]) for h in heads]
    qs = [q_ref[:, j * hd:(j + 1) * hd] for j in range(DN_QK_HEADS)]
    ks = [k_ref[:, j * hd:(j + 1) * hd] for j in range(DN_QK_HEADS)]
    vs = [v_ref[:, h * hd:(h + 1) * hd].astype(F32) for h in heads]
    kq = [_bdot(jnp.concatenate([ks[j], qs[j]], axis=0), ks[j], _NT) for j in range(DN_QK_HEADS)]
    neg_a = [jnp.where(strict, -(kq[h // rep][:c] * beta[h] * dec[h]), 0.0) for h in heads]
    qkd = [(kq[h // rep][c:] * dec[h]).astype(BF16) for h in heads]
    tm = _unit_tri_inverse(neg_a, eye, heads)
    kf = [ks[h // rep].astype(F32) for h in heads]
    rhs = [jnp.concatenate([vs[h] * beta[h], kf[h] * (beta[h] * egc[h])], axis=1) for h in heads]
    uw = [_bdot(tm[h], rhs[h]) for h in heads]
    s0 = [s_ref[h] for h in heads]
    lhs = [jnp.concatenate([uw[h][:, hd:], qs[h // rep].astype(F32) * egc[h]], axis=0) for h in heads]
    ws_qs = [_bdot(lhs[h], s0[h]) for h in heads]
    v_new = [(uw[h][:, :hd] - ws_qs[h][:c]).astype(BF16) for h in heads]
    o = [ws_qs[h][c:] + _bdot(qkd[h], v_new[h]) for h in heads]
    for h in heads:
        s_ref[h] = s0[h] * tail[h] + _bdot(kf[h] * eend[h], v_new[h], _TN)
    if not backward:
        for h in heads:
            o_ref[:, h * hd:(h + 1) * hd] = o[h]
    else:
        for h in heads:
            ot = o[h] + of_ref[:, h * hd:(h + 1) * hd]
            on = ot * lax.rsqrt(jnp.mean(ot * ot, -1, keepdims=True) + RMS_EPS) * ng_ref[...]
            zz = z_ref[:, h * hd:(h + 1) * hd].astype(F32)
            o_ref[:, h * hd:(h + 1) * hd] = (on * (zz * _sigmoid(zz))).astype(o_ref.dtype)


def deltanet_mix(lay, qkv, p, ab, a_log, dt_bias, norm_g):
    c = DN_CHUNK
    n = lay.n_chunks(c)
    n_tok = qkv.shape[0]
    nh = DN_V_HEADS
    neg_a = -jnp.exp(a_log.astype(F32))
    na = jnp.concatenate([neg_a, jnp.zeros_like(neg_a)], axis=1).reshape(1, 4 * nh)
    dt = jnp.concatenate([dt_bias.astype(F32), jnp.zeros_like(neg_a)], axis=1).reshape(1, 4 * nh)
    small = lambda a: pl.BlockSpec(a.shape, lambda b, i: (0, 0))
    consts = (na, dt)
    state = [pltpu.VMEM((nh, DN_HEAD_DIM, DN_HEAD_DIM), F32)]

    def specs(backward):
        rb = lambda b, i: lay.row_block(c, b, lay.seq_chunk(c, backward, i))
        return lambda w, off: pl.BlockSpec((c, w), lambda b, i: (rb(b, i), off))

    common = lambda blk: [blk(DN_QK_W, 0), blk(DN_QK_W, 1), blk(DN_V_W, 1), blk(4 * nh, 0)] + [small(a) for a in consts]
    blk = specs(False)
    o_f = pl.pallas_call(
        functools.partial(_dn_scan_kernel, False), grid=(lay.batch, n),
        in_specs=common(blk), out_specs=blk(DN_V_W, 0),
        out_shape=jax.ShapeDtypeStruct((n_tok, DN_V_W), F32),
        scratch_shapes=state, compiler_params=_params("arbitrary", "arbitrary"), name="dn_scan_fwd",
    )(qkv, qkv, qkv, ab, *consts)
    blk = specs(True)
    return pl.pallas_call(
        functools.partial(_dn_scan_kernel, True), grid=(lay.batch, n),
        in_specs=common(blk) + [blk(DN_V_W, 0), blk(DN_V_W, 2), small(norm_g.reshape(1, DN_HEAD_DIM))],
        out_specs=blk(DN_V_W, 0),
        out_shape=jax.ShapeDtypeStruct((n_tok, DN_V_W), BF16),
        scratch_shapes=state, compiler_params=_params("arbitrary", "arbitrary"), name="dn_scan_bwd",
    )(qkv, qkv, qkv, ab, *consts, o_f, p, norm_g.reshape(1, DN_HEAD_DIM))


def _unit_tri_inverse(nm, eye, heads):
    c = eye.shape[0]
    tm = [eye + nm[h] for h in heads]
    p = [_bdot(nm[h], nm[h]).astype(BF16) for h in heads]
    for _ in range(int(math.log2(c)) - 2):
        pt = [_bdot(jnp.concatenate([p[h], tm[h].astype(BF16)], axis=0), p[h]) for h in heads]
        tm = [tm[h] + pt[h][c:] for h in heads]
        p = [pt[h][:c].astype(BF16) for h in heads]
    return [tm[h] + _bdot(tm[h], p[h]) for h in heads]


def _rwkv_chunk_kernel(r_ref, v_ref, kk_ref, wl_ref, kd_ref, a_ref, o_ref, s_ref, ar_s, bt_s, kt_s, bk_s, uv_s, gc_s):
    z = pl.program_id(0)
    i = pl.program_id(2)
    c = r_ref.shape[0]

    @pl.when(i == 0)
    def _():
        s_ref[...] = jnp.zeros_like(s_ref)

    row = lax.broadcasted_iota(jnp.int32, (c, c), 0)
    col = lax.broadcasted_iota(jnp.int32, (c, c), 1)
    lag = (row - col) * (1 - 2 * z)
    incl = lag >= 0
    strict = lag > 0

    wl = wl_ref[...]
    logw = -jnp.exp(-_softplus(-wl) - 0.5)
    cum = _hdot(incl.astype(F32), logw)
    c_last = jnp.sum(logw, axis=0, keepdims=True)
    kk = kk_ref[...].astype(F32)
    kb = kk * a_ref[...].astype(F32)
    kd = kd_ref[...].astype(F32)
    g_inv = jnp.exp(-cum)
    e_end = jnp.exp(c_last - cum)
    ar_s[0:c, :] = (-kk * jnp.exp(cum - logw)).astype(BF16)
    ar_s[c:, :] = (r_ref[...].astype(F32) * jnp.exp(cum)).astype(BF16)
    bt_s[...] = (kb * g_inv).astype(BF16)
    kt_s[...] = (kd * g_inv).astype(BF16)
    bk_s[0:c, :] = (kb * e_end).astype(BF16)
    bk_s[c:, :] = (kd * e_end).astype(BF16)
    uv_s[c:, :] = v_ref[...].astype(BF16)
    gc_s[...] = jnp.exp(c_last)

    eye = (row == col).astype(F32)
    hh = range(RWKV_HEADS)
    sl = [slice(h * RWKV_HEAD, (h + 1) * RWKV_HEAD) for h in hh]
    mask2 = jnp.concatenate([lag, lag + 1], axis=0) > 0
    xb = [jnp.where(mask2, _bdot(ar_s[:, sl[h]], bt_s[:, sl[h]], _NT), 0.0) for h in hh]
    xk = [jnp.where(mask2, _bdot(ar_s[:, sl[h]], kt_s[:, sl[h]], _NT), 0.0).astype(BF16) for h in hh]
    tm = _unit_tri_inverse([xb[h][:c] for h in hh], eye, hh)
    s0 = [s_ref[h] for h in hh]
    xs = [_bdot(ar_s[:, sl[h]], s0[h], _NT) + _bdot(xk[h], uv_s[c:, sl[h]]) for h in hh]
    u = [_bdot(tm[h], xs[h][:c]).astype(BF16) for h in hh]
    for h in hh:
        uv_s[0:c, sl[h]] = u[h]
        o_ref[:, sl[h]] = xs[h][c:] + _bdot(xb[h][c:], u[h])
    for h in hh:
        s_ref[h] = s0[h] * gc_s[:, sl[h]] + _bdot(uv_s[:, sl[h]], bk_s[:, sl[h]], _TN)


def rwkv_scan(lay, r, v, kk, wl, kd, a):
    n_tok, d = r.shape
    c = RWKV_CHUNK
    n = lay.n_chunks(c)

    def rb(z, b, i):
        sc = jnp.where(z == 0, lay.seq_chunk(c, False, i), lay.seq_chunk(c, True, i))
        return lay.row_block(c, b, sc)

    shared = pl.BlockSpec((c, d), lambda z, b, i: (rb(z, b, i), 0))
    perdir = pl.BlockSpec((None, c, d), lambda z, b, i: (z, rb(z, b, i), 0))
    bf = lambda rows: pltpu.VMEM((rows, d), BF16)
    return pl.pallas_call(
        _rwkv_chunk_kernel,
        grid=(2, lay.batch, n),
        in_specs=[shared, shared, shared, perdir, perdir, perdir],
        out_specs=perdir,
        out_shape=jax.ShapeDtypeStruct((2, n_tok, d), F32),
        scratch_shapes=[pltpu.VMEM((RWKV_HEADS, RWKV_HEAD, RWKV_HEAD), F32),
                        bf(2 * c), bf(c), bf(c), bf(2 * c), bf(2 * c), pltpu.VMEM((1, d), F32)],
        compiler_params=_params("arbitrary", "arbitrary", "arbitrary"),
        name="rwkv_scan",
    )(r, v, kk, wl, kd, a)


def _rwkv_pre_kernel(lay, tm, prev_ref, h_ref, next_ref, mix_ref, wr_ref, wk_ref, wv_ref, w1_ref, a1_ref, g1_ref,
                     w2_ref, a2_ref, g2_ref, w0_ref, a0_ref, kk_ref, ka_ref, rk_ref, seg_ref,
                     r_out, v_out, kk_out, wl_out, kd_out, a_out, g_out, bonus_out, xe_s):
    j = pl.program_id(0)
    r0 = j * tm
    in_ctx = r0 < lay.ctx_tok
    seq_len = jnp.where(in_ctx, lay.n_ctx, lay.seq)
    off = jnp.where(in_ctx, r0, r0 - lay.ctx_tok) % seq_len
    hal = SEQ_HALO
    xe_s[0:hal, :] = jnp.where(off == 0, 0.0, prev_ref[...].astype(F32))
    xe_s[hal:hal + tm, :] = h_ref[...].astype(F32)
    xe_s[hal + tm:, :] = jnp.where(off + tm == seq_len, 0.0, next_ref[...].astype(F32))
    h = xe_s[pl.ds(hal, tm), :]
    xx = 0.5 * (xe_s[pl.ds(hal - 1, tm), :] + xe_s[pl.ds(hal + 1, tm), :]) - h
    xm = lambda i: (h + xx * mix_ref[i:i + 1, :]).astype(BF16)
    r = _bdot(xm(0), wr_ref[...])
    k = _bdot(xm(1), wk_ref[...])
    v = _bdot(xm(2), wv_ref[...])
    hw = jnp.tanh(_bdot(xm(3), w1_ref[...]))
    ha = _bdot(xm(4), a1_ref[...])
    hg = _sigmoid(_bdot(xm(5), g1_ref[...]))
    seg = seg_ref[...]
    kx = k * kk_ref[...]
    r_out[...] = r.astype(r_out.dtype)
    v_out[...] = v.astype(v_out.dtype)
    kk_out[...] = (kx * lax.rsqrt(_bdot(kx * kx, seg) + L2_EPS)).astype(kk_out.dtype)
    g_out[...] = _bdot(hg, g2_ref[...]).astype(g_out.dtype)
    lw = w2_ref.shape[1]
    rr = r * rk_ref[...]
    bsum = None
    for z in range(2):
        wl_out[z] = w0_ref[z:z + 1, :] + _bdot(hw[:, z * lw:(z + 1) * lw], w2_ref[z])
        a = _sigmoid(a0_ref[z:z + 1, :] + _bdot(ha[:, z * lw:(z + 1) * lw], a2_ref[z]))
        kd = k * (1.0 + (a - 1.0) * ka_ref[...])
        a_out[z] = a.astype(a_out.dtype)
        kd_out[z] = kd.astype(kd_out.dtype)
        bsum = rr * kd if bsum is None else bsum + rr * kd
    bonus_out[...] = (_bdot(bsum, seg) * v).astype(bonus_out.dtype)


def _rwkv_post_kernel(o_ref, g_ref, bonus_ref, lnx_ref, seg_ref, a_out):
    seg = seg_ref[...]
    o = o_ref[0] + o_ref[1]
    hi = o.astype(BF16)
    lo = o - hi.astype(F32)
    inv_n = 1.0 / RWKV_HEAD
    oc = o - (_bdot(hi, seg) + _bdot(lo, seg)) * inv_n
    var = _bdot(oc * oc, seg) * inv_n
    on = oc * lax.rsqrt(var + LNX_EPS) * lnx_ref[...] + bonus_ref[...].astype(F32)
    a_out[...] = (on * g_ref[...].astype(F32)).astype(a_out.dtype)


def rwkv7_mix(lay, h, mix, w_rkv, w0, w1, w2, a0, a1, a2, g1, g2, k_k, k_a, r_k, lnx_g):
    n_tok, d = h.shape
    bw = lambda w: w.astype(BF16)
    tm = math.gcd(256, lay.n_ctx, lay.seq)
    hb = tm // SEQ_HALO
    last_hb = n_tok // SEQ_HALO - 1
    head_of = jnp.arange(d) // RWKV_HEAD
    seg = (head_of[:, None] == head_of[None, :]).astype(BF16)
    full = lambda a: pl.BlockSpec(a.shape, lambda j: (0,) * a.ndim)
    row = lambda a: a.reshape(1, d)
    consts = (mix, bw(w_rkv[0]), bw(w_rkv[1]), bw(w_rkv[2]), bw(jnp.concatenate([w1[0], w1[1]], -1)),
              bw(jnp.concatenate([a1[0], a1[1]], -1)), bw(g1), bw(w2), bw(a2), bw(g2), w0, a0,
              row(k_k), row(k_a), row(r_k), seg)
    tok = pl.BlockSpec((tm, d), lambda j: (j, 0))
    tok2 = pl.BlockSpec((2, tm, d), lambda j: (0, j, 0))
    one = lambda dt: jax.ShapeDtypeStruct((n_tok, d), dt)
    two = lambda dt: jax.ShapeDtypeStruct((2, n_tok, d), dt)
    r, v, kk, wl, kd, a, g, bonus = pl.pallas_call(
        functools.partial(_rwkv_pre_kernel, lay, tm),
        grid=(n_tok // tm,),
        in_specs=[pl.BlockSpec((SEQ_HALO, d), lambda j: (jnp.maximum(j * hb - 1, 0), 0)), tok,
                  pl.BlockSpec((SEQ_HALO, d), lambda j: (jnp.minimum((j + 1) * hb, last_hb), 0))]
                 + [full(a) for a in consts],
        out_specs=[tok, tok, tok, tok2, tok2, tok2, tok, tok],
        out_shape=[one(BF16), one(BF16), one(BF16), two(F32), two(BF16), two(BF16), one(BF16), one(BF16)],
        scratch_shapes=[pltpu.VMEM((tm + 2 * SEQ_HALO, d), F32)],
        compiler_params=_params("arbitrary"), name="rwkv_pre",
    )(h, h, h, *consts)
    o = rwkv_scan(lay, r, v, kk, wl, kd, a)
    return pl.pallas_call(
        _rwkv_post_kernel,
        grid=(n_tok // tm,),
        in_specs=[tok2, tok, tok, full(row(lnx_g)), full(seg)],
        out_specs=tok,
        out_shape=one(BF16),
        compiler_params=_params("arbitrary"), name="rwkv_post",
    )(o, g, bonus, row(lnx_g), seg)


def _moe_kernel(be_ref, nb_ref, x_ref, wg_ref, wu_ref, wd_ref, gate_ref, o_ref, acc_s):
    j = pl.program_id(0)
    f = pl.program_id(1)

    @pl.when(f == 0)
    def _():
        acc_s[...] = jnp.zeros_like(acc_s)

    @pl.when(j < nb_ref[0])
    def _():
        x = x_ref[...]
        g = _bdot(x, wg_ref[...])
        u = _bdot(x, wu_ref[...])
        acc_s[...] += _bdot(g * _sigmoid(g) * u, wd_ref[...])

    @pl.when(f == pl.num_programs(1) - 1)
    def _():
        o_ref[...] = acc_s[...] * gate_ref[...]


def moe_experts(xb, block_e, n_used, w_gu, w_down, slot_gate):
    n_slots, d = xb.shape
    bm = MOE_ROWS
    nf = MOE_F_CHUNKS
    tf = w_down.shape[1] // nf
    return pl.pallas_call(
        _moe_kernel,
        grid_spec=pltpu.PrefetchScalarGridSpec(
            num_scalar_prefetch=2,
            grid=(n_slots // bm, nf),
            in_specs=[pl.BlockSpec((bm, d), lambda j, f, be, nb: (j, 0)),
                      pl.BlockSpec((None, d, tf), lambda j, f, be, nb: (be[j], 0, f)),
                      pl.BlockSpec((None, d, tf), lambda j, f, be, nb: (be[j], 0, f + nf)),
                      pl.BlockSpec((None, tf, d), lambda j, f, be, nb: (be[j], f, 0)),
                      pl.BlockSpec((bm, 1), lambda j, f, be, nb: (j, 0))],
            out_specs=pl.BlockSpec((bm, d), lambda j, f, be, nb: (j, 0)),
            scratch_shapes=[pltpu.VMEM((bm, d), F32)]),
        out_shape=jax.ShapeDtypeStruct((n_slots, d), F32),
        compiler_params=_params("arbitrary", "arbitrary"),
        name="moe_experts",
    )(block_e, n_used, xb, w_gu, w_gu, w_down, slot_gate)


def moe_swiglu(h, w_router, w_gu, w_down):
    n, d = h.shape
    logits = proj(h, w_router.astype(BF16), F32, N_EXPERTS)
    top_logit, top_e = lax.top_k(logits, TOP_K)
    gate = jax.nn.softmax(top_logit, axis=-1)
    flat_e = top_e.reshape(-1).astype(jnp.int32)
    order = jnp.argsort(flat_e).astype(jnp.int32)
    onehot = (flat_e[:, None] == jnp.arange(N_EXPERTS, dtype=jnp.int32)).astype(jnp.int32)
    seen = jnp.cumsum(onehot, axis=0)
    counts = seen[-1]
    rank = jnp.sum(seen * onehot, axis=1) - 1
    padded = (counts + MOE_ROWS - 1) // MOE_ROWS * MOE_ROWS
    start = jnp.cumsum(counts) - counts
    pend = jnp.cumsum(padded)
    pstart = pend - padded
    n_slots = (n * TOP_K + MOE_ROWS - 1) // MOE_ROWS * MOE_ROWS + N_EXPERTS * MOE_ROWS
    n_blocks = n_slots // MOE_ROWS
    block_e = jnp.minimum(jnp.sum(jnp.arange(n_blocks)[:, None] * MOE_ROWS >= pend[None, :], axis=1),
                          N_EXPERTS - 1).astype(jnp.int32)
    n_used = (pend[-1] // MOE_ROWS).astype(jnp.int32).reshape(1)
    slot_e = jnp.repeat(block_e, MOE_ROWS)
    slot_off = jnp.arange(n_slots, dtype=jnp.int32) - pstart[slot_e]
    slot_valid = slot_off < counts[slot_e]
    slot_asg = order[jnp.clip(start[slot_e] + slot_off, 0, n * TOP_K - 1)]
    slot_tok = jnp.where(slot_valid, slot_asg // TOP_K, 0)
    slot_gate = jnp.where(slot_valid, gate.reshape(-1)[slot_asg], 0.0)
    yb = moe_experts(h[slot_tok], block_e, n_used, w_gu, w_down, slot_gate[:, None])
    tok_slot = (pstart[flat_e] + rank).reshape(n, TOP_K)
    return yb[tok_slot[:, 0]], yb[tok_slot[:, 1]]


def kernel(x, c, ctx, c_ctx, mod_w, mod_b, ln_g, ln_b,
           ret_w_in, ret_decay, ret_gn_g, ret_w_out,
           dn_w_in, dn_conv_w, dn_a_log, dn_dt_bias, dn_norm_g, dn_w_out,
           rk_mix, rk_w_rkv, rk_w0, rk_w1, rk_w2, rk_a0, rk_a1, rk_a2, rk_g1, rk_g2,
           rk_k_k, rk_k_a, rk_r_k, rk_lnx_g, rk_w_out,
           ffn_w_gu, ffn_w_down, moe_router, moe_w_gu, moe_w_down):
    bsz, t, d = x.shape
    n_ctx = ctx.shape[1]
    lay = Layout(bsz, n_ctx, t)
    bw = lambda w: w.astype(BF16)
    s_rows = jax.nn.silu(jnp.concatenate([c_ctx[None], c], 0))
    s_pad = jnp.zeros((8, d), F32).at[:1 + bsz].set(s_rows)
    mods = modulation_rows(s_pad, mod_w, mod_b)[:, :1 + bsz].reshape(DEPTH, 1 + bsz, 6, 1, d)
    mod = lambda i, k: mods[i, :, k]
    cos_t, sin_t = rope_tables(lay)
    xs = jnp.concatenate([ctx.reshape(-1, d), x.reshape(-1, d)], 0)
    h = modulate(lay, xs, mod(0, 1), mod(0, 0))
    for i in range(DEPTH):
        last = i == DEPTH - 1
        kind, j = i % N_MIXERS, i // N_MIXERS
        if kind == 0:
            p = proj(h, bw(ret_w_in[j]), BF16, 1024)
            log_gamma = jax.nn.log_sigmoid(ret_decay[j].astype(F32))
            a = retention_mix(lay, p, log_gamma, ret_gn_g[j], cos_t, sin_t)
            w_out = ret_w_out[j]
        elif kind == 1:
            n_main = 2 * DN_QK_W + 2 * DN_V_W
            p = proj(h, bw(dn_w_in[j][:, :n_main]), BF16, 1024)
            w_ab = bw(dn_w_in[j][:, n_main:])
            ab = proj(h, w_ab, F32, w_ab.shape[1])
            qkv = dn_prep(lay, p, dn_conv_w[j])
            a = deltanet_mix(lay, qkv, p, ab, dn_a_log[j], dn_dt_bias[j], dn_norm_g[j])
            w_out = dn_w_out[j]
        else:
            a = rwkv7_mix(lay, h, rk_mix[j], rk_w_rkv[j], rk_w0[j], rk_w1[j], rk_w2[j], rk_a0[j], rk_a1[j],
                          rk_a2[j], rk_g1[j], rk_g2[j], rk_k_k[j], rk_k_a[j], rk_r_k[j], rk_lnx_g[j])
            w_out = rk_w_out[j]
        xs, h = out_ln(lay, a, bw(w_out), xs, mod(i, 2), ln_g[i, 0], ln_b[i, 0], mod(i, 4), mod(i, 3))
        nxt = (i + 1) % DEPTH
        if i % 2 == 0:
            hm = swiglu_in(h, bw(ffn_w_gu[i // 2]), 256)
            xs, h = out_ln(lay, hm, bw(ffn_w_down[i // 2]), xs, mod(i, 5), ln_g[i, 1], ln_b[i, 1],
                           mod(nxt, 1), mod(nxt, 0))
        else:
            f = moe_swiglu(h, moe_router[i // 2], moe_w_gu[i // 2], moe_w_down[i // 2])
            xs, h = out_ln(lay, f, None, xs, mod(i, 5), ln_g[i, 1], ln_b[i, 1], mod(nxt, 1), mod(nxt, 0))
    return xs[lay.ctx_tok:].reshape(bsz, t, d)
```

```python
import math, functools
import jax
import jax.numpy as jnp
from jax import lax
import numpy as np
from jax.experimental import pallas as pl
from jax.experimental.pallas import tpu as pltpu

D_MODEL = 1024
DEPTH = 4
GRID_W = 64
N_MIXERS = 3
ALPHA = (2 * DEPTH) ** 0.25
LN_EPS = 1e-5
GN_EPS = 1e-5
RMS_EPS = 1e-6
LNX_EPS = 64e-5
L2_EPS = 1e-6

RET_HEADS = 4
RET_DK = D_MODEL // RET_HEADS
RET_DV = 2 * RET_DK
RET_CHUNK = 256
ROPE_BASE = 10000.0

DN_QK_HEADS = 8
DN_V_HEADS = 16
DN_HEAD_DIM = 128
DN_CHUNK = 64
DN_CONV = 5
DN_QK_W = DN_QK_HEADS * DN_HEAD_DIM
DN_V_W = DN_V_HEADS * DN_HEAD_DIM
SEQ_HALO = 16

RWKV_HEAD = 64
RWKV_HEADS = D_MODEL // RWKV_HEAD
RWKV_CHUNK = 64

FFN_DIM = 2816
N_EXPERTS = 8
TOP_K = 2
EXPERT_DIM = 3584
MOE_ROWS = 1024
MOE_F_CHUNKS = 7

ROW_TILE = 512
VMEM_LIMIT = 48 * 1024 * 1024

BF16 = jnp.bfloat16
F32 = jnp.float32
_NT = (((1,), (1,)), ((), ()))
_TN = (((0,), (0,)), ((), ()))


def _bdot(x, y, dims=None):
    x = x.astype(BF16)
    y = y.astype(BF16)
    if dims is None:
        return jnp.dot(x, y, preferred_element_type=F32)
    return lax.dot_general(x, y, dims, preferred_element_type=F32)


def _hdot(x, y):
    return jnp.dot(x, y, precision=lax.Precision.HIGHEST, preferred_element_type=F32)


def _sigmoid(x):
    return 1.0 / (1.0 + jnp.exp(-x))


def _softplus(x):
    return jnp.maximum(x, 0.0) + jnp.log(1.0 + jnp.exp(-jnp.abs(x)))


def _params(*sem):
    return pltpu.CompilerParams(dimension_semantics=sem, vmem_limit_bytes=VMEM_LIMIT)


class Layout:
    def __init__(self, batch, n_ctx, seq):
        self.batch, self.n_ctx, self.seq = batch, n_ctx, seq
        self.ctx_tok = batch * n_ctx
        self.n_tok = self.ctx_tok + batch * seq
        self.row_tile = math.gcd(ROW_TILE, n_ctx * batch, seq)

    def mod_index(self, tile, j):
        r0 = j * tile
        return jnp.where(r0 < self.ctx_tok, 0, 1 + (r0 - self.ctx_tok) // self.seq)

    def seq_chunk(self, chunk, backward, i):
        nc, nl = self.n_ctx // chunk, self.seq // chunk
        if not backward:
            return i
        return jnp.where(i < nc, nc - 1 - i, 2 * nc + nl - 1 - i)

    def row_block(self, chunk, b, sc):
        nc, nl = self.n_ctx // chunk, self.seq // chunk
        return jnp.where(sc < nc, b * nc + sc, self.batch * nc + b * nl + sc - nc)

    def n_chunks(self, chunk):
        return (self.n_ctx + self.seq) // chunk


def _proj_kernel(h_ref, w_ref, o_ref):
    o_ref[...] = _bdot(h_ref[...], w_ref[...]).astype(o_ref.dtype)


def proj(h, w, out_dtype, tn):
    n_tok, k = h.shape
    n = w.shape[1]
    tm = math.gcd(1024, n_tok)
    return pl.pallas_call(
        _proj_kernel,
        grid=(n // tn, n_tok // tm),
        in_specs=[pl.BlockSpec((tm, k), lambda c, j: (j, 0)), pl.BlockSpec((k, tn), lambda c, j: (0, c))],
        out_specs=pl.BlockSpec((tm, tn), lambda c, j: (j, c)),
        out_shape=jax.ShapeDtypeStruct((n_tok, n), out_dtype),
        compiler_params=_params("arbitrary", "arbitrary"),
        name="proj",
    )(h, w)


def _swiglu_in_kernel(h_ref, wg_ref, wu_ref, o_ref):
    h = h_ref[...]
    g = _bdot(h, wg_ref[...])
    u = _bdot(h, wu_ref[...])
    o_ref[...] = (g * _sigmoid(g) * u).astype(o_ref.dtype)


def swiglu_in(h, w_gu, tn):
    n_tok, k = h.shape
    f = w_gu.shape[1] // 2
    tm = math.gcd(1024, n_tok)
    nf = f // tn
    return pl.pallas_call(
        _swiglu_in_kernel,
        grid=(nf, n_tok // tm),
        in_specs=[pl.BlockSpec((tm, k), lambda c, j: (j, 0)),
                  pl.BlockSpec((k, tn), lambda c, j: (0, c)),
                  pl.BlockSpec((k, tn), lambda c, j: (0, c + nf))],
        out_specs=pl.BlockSpec((tm, tn), lambda c, j: (j, c)),
        out_shape=jax.ShapeDtypeStruct((n_tok, f), BF16),
        compiler_params=_params("arbitrary", "arbitrary"),
        name="swiglu_in",
    )(h, w_gu, w_gu)


def _deepnorm_epilogue(x, f, ga_ref, g_ref, b_ref, sc_ref, sh_ref, x_out, h_out):
    y = ALPHA * x + (1.0 + ga_ref[...]) * f
    mu = jnp.mean(y, -1, keepdims=True)
    yc = y - mu
    var = jnp.mean(yc * yc, -1, keepdims=True)
    xn = yc * lax.rsqrt(var + LN_EPS) * g_ref[...] + b_ref[...]
    x_out[...] = xn
    h_out[...] = (xn * (1.0 + sc_ref[...]) + sh_ref[...]).astype(h_out.dtype)


def _out_ln_kernel(a_ref, w_ref, x_ref, ga_ref, g_ref, b_ref, sc_ref, sh_ref, x_out, h_out):
    _deepnorm_epilogue(x_ref[...], _bdot(a_ref[...], w_ref[...]), ga_ref, g_ref, b_ref, sc_ref, sh_ref, x_out, h_out)


def _resid_ln_kernel(f1_ref, f2_ref, x_ref, ga_ref, g_ref, b_ref, sc_ref, sh_ref, x_out, h_out):
    f = f1_ref[...].astype(F32) + f2_ref[...].astype(F32)
    _deepnorm_epilogue(x_ref[...], f, ga_ref, g_ref, b_ref, sc_ref, sh_ref, x_out, h_out)


def out_ln(lay, a, w, x, gate, ln_g, ln_b, sc_next, sh_next):
    n_tok, d = x.shape
    tm = lay.row_tile
    row = lambda j: (j, 0)
    mod = pl.BlockSpec((None, 1, d), lambda j: (lay.mod_index(tm, j), 0, 0))
    vec = pl.BlockSpec((1, d), lambda j: (0, 0))
    tok = pl.BlockSpec((tm, d), row)
    if w is None:
        body, lhs, lhs_specs = _resid_ln_kernel, tuple(a), [tok, tok]
    else:
        k = a.shape[1]
        body, lhs = _out_ln_kernel, (a, w)
        lhs_specs = [pl.BlockSpec((tm, k), row), pl.BlockSpec((k, d), lambda j: (0, 0))]
    return pl.pallas_call(
        body,
        grid=(n_tok // tm,),
        in_specs=lhs_specs + [tok, mod, vec, vec, mod, mod],
        out_specs=[tok, tok],
        out_shape=[jax.ShapeDtypeStruct((n_tok, d), F32), jax.ShapeDtypeStruct((n_tok, d), BF16)],
        compiler_params=_params("arbitrary"),
        name="out_ln",
    )(*lhs, x, gate, ln_g.reshape(1, d), ln_b.reshape(1, d), sc_next, sh_next)


def _modulate_kernel(x_ref, sc_ref, sh_ref, h_out):
    h_out[...] = (x_ref[...] * (1.0 + sc_ref[...]) + sh_ref[...]).astype(h_out.dtype)


def modulate(lay, x, sc, sh):
    n_tok, d = x.shape
    tm = lay.row_tile
    mod = pl.BlockSpec((None, 1, d), lambda j: (lay.mod_index(tm, j), 0, 0))
    tok = pl.BlockSpec((tm, d), lambda j: (j, 0))
    return pl.pallas_call(
        _modulate_kernel, grid=(n_tok // tm,), in_specs=[tok, mod, mod], out_specs=tok,
        out_shape=jax.ShapeDtypeStruct((n_tok, d), BF16), compiler_params=_params("arbitrary"), name="modulate",
    )(x, sc, sh)


def _mod_kernel(s_ref, w_ref, b_ref, o_ref):
    o_ref[...] = _bdot(s_ref[...], w_ref[...]) + b_ref[...]


def modulation_rows(s, mod_w, mod_b):
    r, d = s.shape
    depth, _, n = mod_w.shape
    tn = 1024
    return pl.pallas_call(
        _mod_kernel,
        grid=(depth, n // tn),
        in_specs=[pl.BlockSpec((r, d), lambda i, c: (0, 0)),
                  pl.BlockSpec((None, d, tn), lambda i, c: (i, 0, c)),
                  pl.BlockSpec((None, 1, tn), lambda i, c: (i, 0, c))],
        out_specs=pl.BlockSpec((None, r, tn), lambda i, c: (i, 0, c)),
        out_shape=jax.ShapeDtypeStruct((depth, r, n), F32),
        compiler_params=_params("arbitrary", "arbitrary"),
        name="modulation_rows",
    )(s, mod_w, mod_b.reshape(depth, 1, n))


def _rope(x, cos, sin):
    half = x.shape[1] // 2
    parts = []
    for p in range(2):
        xs = x[:, p * half:(p + 1) * half]
        parts.append(xs * cos[:, p * half:(p + 1) * half]
                     + pltpu.roll(xs, half // 2, axis=1) * sin[:, p * half:(p + 1) * half])
    return jnp.concatenate(parts, axis=1)


def _ret_heads(q_ref, k_ref, v_ref, cos_ref, sin_ref):
    cos, sin = cos_ref[...], sin_ref[...]
    hs = range(RET_HEADS)
    q = [_rope(q_ref[:, h * RET_DK:(h + 1) * RET_DK].astype(F32), cos, sin) for h in hs]
    k = [_rope(k_ref[:, h * RET_DK:(h + 1) * RET_DK].astype(F32), cos, sin) for h in hs]
    v = [v_ref[:, h * RET_DV:(h + 1) * RET_DV] for h in hs]
    return hs, q, k, v


def _ret_fwd_kernel(cd_ref, q_ref, k_ref, v_ref, cos_ref, sin_ref, dec_ref, rd_ref, o_ref, s_ref):
    i = pl.program_id(1)

    @pl.when(i == 0)
    def _():
        s_ref[...] = jnp.zeros_like(s_ref)

    hs, q, k, v = _ret_heads(q_ref, k_ref, v_ref, cos_ref, sin_ref)
    scores = [(_bdot(q[h], k[h], _NT) * dec_ref[h]).astype(BF16) for h in hs]
    s = [s_ref[h] for h in hs]
    qd = [(q[h] * rd_ref[0, h]).astype(BF16) for h in hs]
    kd = [(k[h] * rd_ref[1, h]).astype(BF16) for h in hs]
    for h in hs:
        o_ref[:, h * RET_DV:(h + 1) * RET_DV] = _bdot(scores[h], v[h]) + _bdot(qd[h], s[h])
    for h in hs:
        s_ref[h] = s[h] * cd_ref[0, h] + _bdot(kd[h], v[h], _TN)


def _ret_bwd_kernel(cd_ref, q_ref, k_ref, v_ref, g_ref, cos_ref, sin_ref, rd_ref, op_ref, gn_ref, o_ref, s_ref):
    i = pl.program_id(1)

    @pl.when(i == 0)
    def _():
        s_ref[...] = jnp.zeros_like(s_ref)

    hs, q, k, v = _ret_heads(q_ref, k_ref, v_ref, cos_ref, sin_ref)
    s = [s_ref[h] for h in hs]
    qd = [(q[h] * rd_ref[0, h]).astype(BF16) for h in hs]
    kd = [(k[h] * rd_ref[1, h]).astype(BF16) for h in hs]
    o = [op_ref[:, h * RET_DV:(h + 1) * RET_DV] + _bdot(qd[h], s[h]) for h in hs]
    for h in hs:
        s_ref[h] = s[h] * cd_ref[1, h] + _bdot(kd[h], v[h], _TN)
    for h in hs:
        hv = slice(h * RET_DV, (h + 1) * RET_DV)
        mu = jnp.mean(o[h], -1, keepdims=True)
        oc = o[h] - mu
        var = jnp.mean(oc * oc, -1, keepdims=True)
        g = g_ref[:, hv].astype(F32)
        o_ref[:, hv] = (g * _sigmoid(g) * (oc * lax.rsqrt(var + GN_EPS) * gn_ref[:, hv])).astype(o_ref.dtype)


def retention_mix(lay, p, log_gamma, gn_g, cos_t, sin_t):
    c = RET_CHUNK
    n = lay.n_chunks(c)
    n_tok = p.shape[0]
    hk = RET_HEADS * RET_DK
    hv = RET_HEADS * RET_DV

    def specs(backward):
        sc = lambda i: lay.seq_chunk(c, backward, i)
        blk = lambda w, off: pl.BlockSpec((c, w), lambda b, i: (lay.row_block(c, b, sc(i)), off))
        tab = pl.BlockSpec((c, RET_DK), lambda b, i: (sc(i), 0))
        return blk, tab

    pos = jnp.arange(c, dtype=F32)
    lag = pos[:, None] - pos[None, :]
    lg_f, lg_b = log_gamma[0][:, None, None], log_gamma[1][:, None, None]
    k_scale = RET_DK ** -0.5
    dec = (jnp.where(lag >= 0, jnp.exp(jnp.maximum(lag, 0.0) * lg_f), 0.0)
           + jnp.where(lag <= 0, jnp.exp(jnp.maximum(-lag, 0.0) * lg_b), 0.0)) * k_scale
    rows = lambda e, lg: jnp.exp(e[None, :] * lg[:, None])[..., None]
    rd_f = jnp.stack([rows(pos + 1.0, log_gamma[0]), rows(c - 1.0 - pos, log_gamma[0]) * k_scale])
    rd_b = jnp.stack([rows(c - pos, log_gamma[1]), rows(pos, log_gamma[1]) * k_scale])
    cd = jnp.exp(c * log_gamma)

    smem = pl.BlockSpec(memory_space=pltpu.SMEM)
    full = lambda a: pl.BlockSpec(a.shape, lambda b, i: (0,) * a.ndim)
    grid = (lay.batch, n)
    state = [pltpu.VMEM((RET_HEADS, RET_DK, RET_DV), F32)]
    blk, tab = specs(False)
    o_part = pl.pallas_call(
        _ret_fwd_kernel, grid=grid,
        in_specs=[smem, blk(hk, 0), blk(hk, 1), blk(hv, 1), tab, tab, full(dec), full(rd_f)],
        out_specs=blk(hv, 0),
        out_shape=jax.ShapeDtypeStruct((n_tok, hv), F32),
        scratch_shapes=state, compiler_params=_params("arbitrary", "arbitrary"), name="ret_fwd",
    )(cd, p, p, p, cos_t, sin_t, dec, rd_f)
    blk, tab = specs(True)
    return pl.pallas_call(
        _ret_bwd_kernel, grid=grid,
        in_specs=[smem, blk(hk, 0), blk(hk, 1), blk(hv, 1), blk(hv, 2), tab, tab, full(rd_b),
                  blk(hv, 0), pl.BlockSpec((1, hv), lambda b, i: (0, 0))],
        out_specs=blk(hv, 0),
        out_shape=jax.ShapeDtypeStruct((n_tok, hv), BF16),
        scratch_shapes=state, compiler_params=_params("arbitrary", "arbitrary"), name="ret_bwd",
    )(cd, p, p, p, p, cos_t, sin_t, rd_b, o_part, gn_g.reshape(1, hv))


def rope_tables(lay):
    quarter = RET_DK // 4
    t = jnp.arange(lay.seq)
    inv = ROPE_BASE ** (-jnp.arange(quarter, dtype=F32) / quarter)
    ang_r = (t // GRID_W).astype(F32)[:, None] * inv
    ang_c = (t % GRID_W).astype(F32)[:, None] * inv
    cos = jnp.concatenate([jnp.cos(ang_r)] * 2 + [jnp.cos(ang_c)] * 2, -1)
    sin = jnp.concatenate([-jnp.sin(ang_r), jnp.sin(ang_r), -jnp.sin(ang_c), jnp.sin(ang_c)], -1)
    cos = jnp.concatenate([jnp.ones((lay.n_ctx, RET_DK), F32), cos], 0)
    sin = jnp.concatenate([jnp.zeros((lay.n_ctx, RET_DK), F32), sin], 0)
    return cos, sin


def _dn_prep_kernel(lay, tm, prev_ref, x_ref, next_ref, w_ref, o_ref, xe_s):
    j = pl.program_id(0)
    ct = pl.program_id(1)
    r0 = j * tm
    in_ctx = r0 < lay.ctx_tok
    seq_len = jnp.where(in_ctx, lay.n_ctx, lay.seq)
    off = jnp.where(in_ctx, r0, r0 - lay.ctx_tok) % seq_len
    first = off == 0
    last = off + tm == seq_len
    hal = SEQ_HALO
    xe_s[0:hal, :] = jnp.where(first, 0.0, prev_ref[...].astype(F32))
    xe_s[hal:hal + tm, :] = x_ref[...].astype(F32)
    xe_s[hal + tm:, :] = jnp.where(last, 0.0, next_ref[...].astype(F32))
    pad = (DN_CONV - 1) // 2
    acc = xe_s[pl.ds(hal - pad, tm), :] * w_ref[0:1, :]
    for d in range(1, DN_CONV):
        acc = acc + xe_s[pl.ds(hal - pad + d, tm), :] * w_ref[d:d + 1, :]
    y = acc * _sigmoid(acc)
    n_qk_tiles = 2 * DN_QK_W // x_ref.shape[1]

    @pl.when(ct >= n_qk_tiles)
    def _():
        o_ref[...] = y.astype(o_ref.dtype)

    @pl.when(ct < n_qk_tiles)
    def _():
        is_q = ct * x_ref.shape[1] < DN_QK_W
        scale = jnp.where(is_q, DN_HEAD_DIM ** -0.5, 1.0)
        for s in range(x_ref.shape[1] // DN_HEAD_DIM):
            ys = y[:, s * DN_HEAD_DIM:(s + 1) * DN_HEAD_DIM]
            inv = lax.rsqrt(jnp.sum(ys * ys, -1, keepdims=True) + L2_EPS) * scale
            o_ref[:, s * DN_HEAD_DIM:(s + 1) * DN_HEAD_DIM] = (ys * inv).astype(o_ref.dtype)


def dn_prep(lay, p, conv_w):
    n_tok = p.shape[0]
    w = 2 * DN_QK_W + DN_V_W
    tm = math.gcd(256, lay.n_ctx, lay.seq)
    tc = 1024
    hb = tm // SEQ_HALO
    last_hb = n_tok // SEQ_HALO - 1
    return pl.pallas_call(
        functools.partial(_dn_prep_kernel, lay, tm),
        grid=(n_tok // tm, w // tc),
        in_specs=[pl.BlockSpec((SEQ_HALO, tc), lambda j, c: (jnp.maximum(j * hb - 1, 0), c)),
                  pl.BlockSpec((tm, tc), lambda j, c: (j, c)),
                  pl.BlockSpec((SEQ_HALO, tc), lambda j, c: (jnp.minimum((j + 1) * hb, last_hb), c)),
                  pl.BlockSpec((DN_CONV, tc), lambda j, c: (0, c))],
        out_specs=pl.BlockSpec((tm, tc), lambda j, c: (j, c)),
        out_shape=jax.ShapeDtypeStruct((n_tok, w), BF16),
        scratch_shapes=[pltpu.VMEM((tm + 2 * SEQ_HALO, tc), F32)],
        compiler_params=_params("arbitrary", "arbitrary"), name="dn_prep",
    )(p, p, p, conv_w)


def _dn_scan_kernel(backward, q_ref, k_ref, v_ref, ab_ref, na_row, dt_row, *rest):
    if backward:
        of_ref, z_ref, ng_ref, o_ref, s_ref = rest
    else:
        o_ref, s_ref = rest
    i = pl.program_id(1)
    c = q_ref.shape[0]
    z = 1 if backward else 0
    nh = DN_V_HEADS
    hd = DN_HEAD_DIM
    rep = DN_V_HEADS // DN_QK_HEADS

    @pl.when(i == 0)
    def _():
        s_ref[...] = jnp.zeros_like(s_ref)

    row = lax.broadcasted_iota(jnp.int32, (c, c), 0)
    col = lax.broadcasted_iota(jnp.int32, (c, c), 1)
    lag = (col - row) if backward else (row - col)
    incl = lag >= 0
    strict = lag > 0
    eye = (row == col).astype(F32)
    ab = ab_ref[...]
    g_cols = na_row[...] * _softplus(ab + dt_row[...])
    beta_cols = _sigmoid(ab)
    gc_cols = _hdot(incl.astype(F32), g_cols)
    gc_rows = lax.dot_general(gc_cols, eye, _TN, precision=lax.Precision.HIGHEST,
                              preferred_element_type=F32)
    gend = jnp.sum(g_cols, axis=0, keepdims=True)
    heads = range(nh)
    gi = [z * 2 * nh + h for h in heads]
    bi = [z * 2 * nh + nh + h for h in heads]
    gcc = [gc_cols[:, gi[h]:gi[h] + 1] for h in heads]
    dec = [jnp.where(incl, jnp.exp(jnp.minimum(gcc[h] - gc_rows[gi[h]:gi[h] + 1, :], 0.0)), 0.0) for h in heads]
    beta = [beta_cols[:, bi[h]:bi[h] + 1] for h in heads]
    egc = [jnp.exp(gcc[h]) for h in heads]
    eend = [jnp.exp(gend[:, gi[h]:gi[h] + 1] - gcc[h]) for h in heads]
    tail = [jnp.exp(gend[:, gi[h]:gi[h] + 1]) for h in heads]
    qs = [q_ref[:, j * hd:(j + 1) * hd] for j in range(DN_QK_HEADS)]
    ks = [k_ref[:, j * hd:(j + 1) * hd] for j in range(DN_QK_HEADS)]
    vs = [v_ref[:, h * hd:(h + 1) * hd].astype(F32) for h in heads]
    kq = [_bdot(jnp.concatenate([ks[j], qs[j]], axis=0), ks[j], _NT) for j in range(DN_QK_HEADS)]
    neg_a = [jnp.where(strict, -(kq[h // rep][:c] * beta[h] * dec[h]), 0.0) for h in heads]
    qkd = [(kq[h // rep][c:] * dec[h]).astype(BF16) for h in heads]
    tm = _unit_tri_inverse(neg_a, eye, heads)
    kf = [ks[h // rep].astype(F32) for h in heads]
    rhs = [jnp.concatenate([vs[h] * beta[h], kf[h] * (beta[h] * egc[h])], axis=1) for h in heads]
    uw = [_bdot(tm[h], rhs[h]) for h in heads]
    s0 = [s_ref[h] for h in heads]
    lhs = [jnp.concatenate([uw[h][:, hd:], qs[h // rep].astype(F32) * egc[h]], axis=0) for h in heads]
    ws_qs = [_bdot(lhs[h], s0[h]) for h in heads]
    v_new = [(uw[h][:, :hd] - ws_qs[h][:c]).astype(BF16) for h in heads]
    o = [ws_qs[h][c:] + _bdot(qkd[h], v_new[h]) for h in heads]
    for h in heads:
        s_ref[h] = s0[h] * tail[h] + _bdot(kf[h] * eend[h], v_new[h], _TN)
    if not backward:
        for h in heads:
            o_ref[:, h * hd:(h + 1) * hd] = o[h]
    else:
        for h in heads:
            ot = o[h] + of_ref[:, h * hd:(h + 1) * hd]
            on = ot * lax.rsqrt(jnp.mean(ot * ot, -1, keepdims=True) + RMS_EPS) * ng_ref[...]
            zz = z_ref[:, h * hd:(h + 1) * hd].astype(F32)
            o_ref[:, h * hd:(h + 1) * hd] = (on * (zz * _sigmoid(zz))).astype(o_ref.dtype)


def deltanet_mix(lay, qkv, p, ab, a_log, dt_bias, norm_g):
    c = DN_CHUNK
    n = lay.n_chunks(c)
    n_tok = qkv.shape[0]
    nh = DN_V_HEADS
    neg_a = -jnp.exp(a_log.astype(F32))
    na = jnp.concatenate([neg_a, jnp.zeros_like(neg_a)], axis=1).reshape(1, 4 * nh)
    dt = jnp.concatenate([dt_bias.astype(F32), jnp.zeros_like(neg_a)], axis=1).reshape(1, 4 * nh)
    small = lambda a: pl.BlockSpec(a.shape, lambda b, i: (0, 0))
    consts = (na, dt)
    state = [pltpu.VMEM((nh, DN_HEAD_DIM, DN_HEAD_DIM), F32)]

    def specs(backward):
        rb = lambda b, i: lay.row_block(c, b, lay.seq_chunk(c, backward, i))
        return lambda w, off: pl.BlockSpec((c, w), lambda b, i: (rb(b, i), off))

    common = lambda blk: [blk(DN_QK_W, 0), blk(DN_QK_W, 1), blk(DN_V_W, 1), blk(4 * nh, 0)] + [small(a) for a in consts]
    blk = specs(False)
    o_f = pl.pallas_call(
        functools.partial(_dn_scan_kernel, False), grid=(lay.batch, n),
        in_specs=common(blk), out_specs=blk(DN_V_W, 0),
        out_shape=jax.ShapeDtypeStruct((n_tok, DN_V_W), F32),
        scratch_shapes=state, compiler_params=_params("arbitrary", "arbitrary"), name="dn_scan_fwd",
    )(qkv, qkv, qkv, ab, *consts)
    blk = specs(True)
    return pl.pallas_call(
        functools.partial(_dn_scan_kernel, True), grid=(lay.batch, n),
        in_specs=common(blk) + [blk(DN_V_W, 0), blk(DN_V_W, 2), small(norm_g.reshape(1, DN_HEAD_DIM))],
        out_specs=blk(DN_V_W, 0),
        out_shape=jax.ShapeDtypeStruct((n_tok, DN_V_W), BF16),
        scratch_shapes=state, compiler_params=_params("arbitrary", "arbitrary"), name="dn_scan_bwd",
    )(qkv, qkv, qkv, ab, *consts, o_f, p, norm_g.reshape(1, DN_HEAD_DIM))


def _unit_tri_inverse(nm, eye, heads):
    c = eye.shape[0]
    tm = [eye + nm[h] for h in heads]
    p = [_bdot(nm[h], nm[h]).astype(BF16) for h in heads]
    for _ in range(int(math.log2(c)) - 2):
        pt = [_bdot(jnp.concatenate([p[h], tm[h].astype(BF16)], axis=0), p[h]) for h in heads]
        tm = [tm[h] + pt[h][c:] for h in heads]
        p = [pt[h][:c].astype(BF16) for h in heads]
    return [tm[h] + _bdot(tm[h], p[h]) for h in heads]


def _rwkv_chunk_kernel(r_ref, v_ref, kk_ref, wl_ref, kd_ref, a_ref, o_ref, s_ref, ar_s, bt_s, kt_s, bk_s, uv_s, gc_s):
    z = pl.program_id(0)
    i = pl.program_id(2)
    c = r_ref.shape[0]

    @pl.when(i == 0)
    def _():
        s_ref[...] = jnp.zeros_like(s_ref)

    row = lax.broadcasted_iota(jnp.int32, (c, c), 0)
    col = lax.broadcasted_iota(jnp.int32, (c, c), 1)
    lag = (row - col) * (1 - 2 * z)
    incl = lag >= 0
    strict = lag > 0

    wl = wl_ref[...]
    logw = -jnp.exp(-_softplus(-wl) - 0.5)
    cum = _hdot(incl.astype(F32), logw)
    c_last = jnp.sum(logw, axis=0, keepdims=True)
    kk = kk_ref[...].astype(F32)
    kb = kk * a_ref[...].astype(F32)
    kd = kd_ref[...].astype(F32)
    g_inv = jnp.exp(-cum)
    e_end = jnp.exp(c_last - cum)
    ar_s[0:c, :] = (-kk * jnp.exp(cum - logw)).astype(BF16)
    ar_s[c:, :] = (r_ref[...].astype(F32) * jnp.exp(cum)).astype(BF16)
    bt_s[...] = (kb * g_inv).astype(BF16)
    kt_s[...] = (kd * g_inv).astype(BF16)
    bk_s[0:c, :] = (kb * e_end).astype(BF16)
    bk_s[c:, :] = (kd * e_end).astype(BF16)
    uv_s[c:, :] = v_ref[...].astype(BF16)
    gc_s[...] = jnp.exp(c_last)

    eye = (row == col).astype(F32)
    hh = range(RWKV_HEADS)
    sl = [slice(h * RWKV_HEAD, (h + 1) * RWKV_HEAD) for h in hh]
    mask2 = jnp.concatenate([lag, lag + 1], axis=0) > 0
    xb = [jnp.where(mask2, _bdot(ar_s[:, sl[h]], bt_s[:, sl[h]], _NT), 0.0) for h in hh]
    xk = [jnp.where(mask2, _bdot(ar_s[:, sl[h]], kt_s[:, sl[h]], _NT), 0.0).astype(BF16) for h in hh]
    tm = _unit_tri_inverse([xb[h][:c] for h in hh], eye, hh)
    s0 = [s_ref[h] for h in hh]
    xs = [_bdot(ar_s[:, sl[h]], s0[h], _NT) + _bdot(xk[h], uv_s[c:, sl[h]]) for h in hh]
    u = [_bdot(tm[h], xs[h][:c]).astype(BF16) for h in hh]
    for h in hh:
        uv_s[0:c, sl[h]] = u[h]
        o_ref[:, sl[h]] = xs[h][c:] + _bdot(xb[h][c:], u[h])
    for h in hh:
        s_ref[h] = s0[h] * gc_s[:, sl[h]] + _bdot(uv_s[:, sl[h]], bk_s[:, sl[h]], _TN)


def rwkv_scan(lay, r, v, kk, wl, kd, a):
    n_tok, d = r.shape
    c = RWKV_CHUNK
    n = lay.n_chunks(c)

    def rb(z, b, i):
        sc = jnp.where(z == 0, lay.seq_chunk(c, False, i), lay.seq_chunk(c, True, i))
        return lay.row_block(c, b, sc)

    shared = pl.BlockSpec((c, d), lambda z, b, i: (rb(z, b, i), 0))
    perdir = pl.BlockSpec((None, c, d), lambda z, b, i: (z, rb(z, b, i), 0))
    bf = lambda rows: pltpu.VMEM((rows, d), BF16)
    return pl.pallas_call(
        _rwkv_chunk_kernel,
        grid=(2, lay.batch, n),
        in_specs=[shared, shared, shared, perdir, perdir, perdir],
        out_specs=perdir,
        out_shape=jax.ShapeDtypeStruct((2, n_tok, d), F32),
        scratch_shapes=[pltpu.VMEM((RWKV_HEADS, RWKV_HEAD, RWKV_HEAD), F32),
                        bf(2 * c), bf(c), bf(c), bf(2 * c), bf(2 * c), pltpu.VMEM((1, d), F32)],
        compiler_params=_params("arbitrary", "arbitrary", "arbitrary"),
        name="rwkv_scan",
    )(r, v, kk, wl, kd, a)


def _rwkv_pre_kernel(lay, tm, prev_ref, h_ref, next_ref, mix_ref, wr_ref, wk_ref, wv_ref, w1_ref, a1_ref, g1_ref,
                     w2_ref, a2_ref, g2_ref, w0_ref, a0_ref, kk_ref, ka_ref, rk_ref, seg_ref,
                     r_out, v_out, kk_out, wl_out, kd_out, a_out, g_out, bonus_out, xe_s):
    j = pl.program_id(0)
    r0 = j * tm
    in_ctx = r0 < lay.ctx_tok
    seq_len = jnp.where(in_ctx, lay.n_ctx, lay.seq)
    off = jnp.where(in_ctx, r0, r0 - lay.ctx_tok) % seq_len
    hal = SEQ_HALO
    xe_s[0:hal, :] = jnp.where(off == 0, 0.0, prev_ref[...].astype(F32))
    xe_s[hal:hal + tm, :] = h_ref[...].astype(F32)
    xe_s[hal + tm:, :] = jnp.where(off + tm == seq_len, 0.0, next_ref[...].astype(F32))
    h = xe_s[pl.ds(hal, tm), :]
    xx = 0.5 * (xe_s[pl.ds(hal - 1, tm), :] + xe_s[pl.ds(hal + 1, tm), :]) - h
    xm = lambda i: (h + xx * mix_ref[i:i + 1, :]).astype(BF16)
    r = _bdot(xm(0), wr_ref[...])
    k = _bdot(xm(1), wk_ref[...])
    v = _bdot(xm(2), wv_ref[...])
    hw = jnp.tanh(_bdot(xm(3), w1_ref[...]))
    ha = _bdot(xm(4), a1_ref[...])
    hg = _sigmoid(_bdot(xm(5), g1_ref[...]))
    seg = seg_ref[...]
    kx = k * kk_ref[...]
    r_out[...] = r.astype(r_out.dtype)
    v_out[...] = v.astype(v_out.dtype)
    kk_out[...] = (kx * lax.rsqrt(_bdot(kx * kx, seg) + L2_EPS)).astype(kk_out.dtype)
    g_out[...] = _bdot(hg, g2_ref[...]).astype(g_out.dtype)
    lw = w2_ref.shape[1]
    rr = r * rk_ref[...]
    bsum = None
    for z in range(2):
        wl_out[z] = w0_ref[z:z + 1, :] + _bdot(hw[:, z * lw:(z + 1) * lw], w2_ref[z])
        a = _sigmoid(a0_ref[z:z + 1, :] + _bdot(ha[:, z * lw:(z + 1) * lw], a2_ref[z]))
        kd = k * (1.0 + (a - 1.0) * ka_ref[...])
        a_out[z] = a.astype(a_out.dtype)
        kd_out[z] = kd.astype(kd_out.dtype)
        bsum = rr * kd if bsum is None else bsum + rr * kd
    bonus_out[...] = (_bdot(bsum, seg) * v).astype(bonus_out.dtype)


def _rwkv_post_kernel(o_ref, g_ref, bonus_ref, lnx_ref, seg_ref, a_out):
    seg = seg_ref[...]
    o = o_ref[0] + o_ref[1]
    hi = o.astype(BF16)
    lo = o - hi.astype(F32)
    inv_n = 1.0 / RWKV_HEAD
    oc = o - (_bdot(hi, seg) + _bdot(lo, seg)) * inv_n
    var = _bdot(oc * oc, seg) * inv_n
    on = oc * lax.rsqrt(var + LNX_EPS) * lnx_ref[...] + bonus_ref[...].astype(F32)
    a_out[...] = (on * g_ref[...].astype(F32)).astype(a_out.dtype)


def rwkv7_mix(lay, h, mix, w_rkv, w0, w1, w2, a0, a1, a2, g1, g2, k_k, k_a, r_k, lnx_g):
    n_tok, d = h.shape
    bw = lambda w: w.astype(BF16)
    tm = math.gcd(256, lay.n_ctx, lay.seq)
    hb = tm // SEQ_HALO
    last_hb = n_tok // SEQ_HALO - 1
    head_of = jnp.arange(d) // RWKV_HEAD
    seg = (head_of[:, None] == head_of[None, :]).astype(BF16)
    full = lambda a: pl.BlockSpec(a.shape, lambda j: (0,) * a.ndim)
    row = lambda a: a.reshape(1, d)
    consts = (mix, bw(w_rkv[0]), bw(w_rkv[1]), bw(w_rkv[2]), bw(jnp.concatenate([w1[0], w1[1]], -1)),
              bw(jnp.concatenate([a1[0], a1[1]], -1)), bw(g1), bw(w2), bw(a2), bw(g2), w0, a0,
              row(k_k), row(k_a), row(r_k), seg)
    tok = pl.BlockSpec((tm, d), lambda j: (j, 0))
    tok2 = pl.BlockSpec((2, tm, d), lambda j: (0, j, 0))
    one = lambda dt: jax.ShapeDtypeStruct((n_tok, d), dt)
    two = lambda dt: jax.ShapeDtypeStruct((2, n_tok, d), dt)
    r, v, kk, wl, kd, a, g, bonus = pl.pallas_call(
        functools.partial(_rwkv_pre_kernel, lay, tm),
        grid=(n_tok // tm,),
        in_specs=[pl.BlockSpec((SEQ_HALO, d), lambda j: (jnp.maximum(j * hb - 1, 0), 0)), tok,
                  pl.BlockSpec((SEQ_HALO, d), lambda j: (jnp.minimum((j + 1) * hb, last_hb), 0))]
                 + [full(a) for a in consts],
        out_specs=[tok, tok, tok, tok2, tok2, tok2, tok, tok],
        out_shape=[one(BF16), one(BF16), one(BF16), two(F32), two(BF16), two(BF16), one(BF16), one(BF16)],
        scratch_shapes=[pltpu.VMEM((tm + 2 * SEQ_HALO, d), F32)],
        compiler_params=_params("arbitrary"), name="rwkv_pre",
    )(h, h, h, *consts)
    o = rwkv_scan(lay, r, v, kk, wl, kd, a)
    return pl.pallas_call(
        _rwkv_post_kernel,
        grid=(n_tok // tm,),
        in_specs=[tok2, tok, tok, full(row(lnx_g)), full(seg)],
        out_specs=tok,
        out_shape=one(BF16),
        compiler_params=_params("arbitrary"), name="rwkv_post",
    )(o, g, bonus, row(lnx_g), seg)


def _moe_kernel(be_ref, nb_ref, x_ref, wg_ref, wu_ref, wd_ref, gate_ref, o_ref, acc_s):
    j = pl.program_id(0)
    f = pl.program_id(1)

    @pl.when(f == 0)
    def _():
        acc_s[...] = jnp.zeros_like(acc_s)

    @pl.when(j < nb_ref[0])
    def _():
        x = x_ref[...]
        g = _bdot(x, wg_ref[...])
        u = _bdot(x, wu_ref[...])
        acc_s[...] += _bdot(g * _sigmoid(g) * u, wd_ref[...])

    @pl.when(f == pl.num_programs(1) - 1)
    def _():
        o_ref[...] = (acc_s[...] * gate_ref[...]).astype(o_ref.dtype)


def moe_experts(xb, block_e, n_used, w_gu, w_down, layer, slot_gate):
    n_slots, d = xb.shape
    bm = MOE_ROWS
    nf = MOE_F_CHUNKS
    tf = w_down.shape[2] // nf
    return pl.pallas_call(
        _moe_kernel,
        grid_spec=pltpu.PrefetchScalarGridSpec(
            num_scalar_prefetch=2,
            grid=(n_slots // bm, nf),
            in_specs=[pl.BlockSpec((bm, d), lambda j, f, be, nb: (j, 0)),
                      pl.BlockSpec((None, None, d, tf), lambda j, f, be, nb: (layer, be[j], 0, f)),
                      pl.BlockSpec((None, None, d, tf), lambda j, f, be, nb: (layer, be[j], 0, f + nf)),
                      pl.BlockSpec((None, None, tf, d), lambda j, f, be, nb: (layer, be[j], f, 0)),
                      pl.BlockSpec((bm, 1), lambda j, f, be, nb: (j, 0))],
            out_specs=pl.BlockSpec((bm, d), lambda j, f, be, nb: (j, 0)),
            scratch_shapes=[pltpu.VMEM((bm, d), F32)]),
        out_shape=jax.ShapeDtypeStruct((n_slots, d), BF16),
        compiler_params=_params("arbitrary", "arbitrary"),
        name="moe_experts",
    )(block_e, n_used, xb, w_gu, w_gu, w_down, slot_gate)


def moe_swiglu(h, w_router, w_gu, w_down, layer):
    n, d = h.shape
    logits = proj(h, w_router.astype(BF16), F32, N_EXPERTS)
    top_logit, top_e = lax.top_k(logits, TOP_K)
    gate = jax.nn.softmax(top_logit, axis=-1)
    flat_e = top_e.reshape(-1).astype(jnp.int32)
    order = jnp.argsort(flat_e).astype(jnp.int32)
    onehot = (flat_e[:, None] == jnp.arange(N_EXPERTS, dtype=jnp.int32)).astype(jnp.int32)
    seen = jnp.cumsum(onehot, axis=0)
    counts = seen[-1]
    rank = jnp.sum(seen * onehot, axis=1) - 1
    padded = (counts + MOE_ROWS - 1) // MOE_ROWS * MOE_ROWS
    start = jnp.cumsum(counts) - counts
    pend = jnp.cumsum(padded)
    pstart = pend - padded
    n_slots = (n * TOP_K + MOE_ROWS - 1) // MOE_ROWS * MOE_ROWS + N_EXPERTS * MOE_ROWS
    n_blocks = n_slots // MOE_ROWS
    block_e = jnp.minimum(jnp.sum(jnp.arange(n_blocks)[:, None] * MOE_ROWS >= pend[None, :], axis=1),
                          N_EXPERTS - 1).astype(jnp.int32)
    n_used = (pend[-1] // MOE_ROWS).astype(jnp.int32).reshape(1)
    slot_e = jnp.repeat(block_e, MOE_ROWS)
    slot_off = jnp.arange(n_slots, dtype=jnp.int32) - pstart[slot_e]
    slot_valid = slot_off < counts[slot_e]
    slot_asg = order[jnp.clip(start[slot_e] + slot_off, 0, n * TOP_K - 1)]
    slot_tok = jnp.where(slot_valid, slot_asg // TOP_K, 0)
    slot_gate = jnp.where(slot_valid, gate.reshape(-1)[slot_asg], 0.0)
    yb = moe_experts(h[slot_tok], block_e, n_used, w_gu, w_down, layer, slot_gate[:, None])
    tok_slot = (pstart[flat_e] + rank).reshape(n, TOP_K)
    return yb[tok_slot[:, 0]], yb[tok_slot[:, 1]]


def kernel(x, c, ctx, c_ctx, mod_w, mod_b, ln_g, ln_b,
           ret_w_in, ret_decay, ret_gn_g, ret_w_out,
           dn_w_in, dn_conv_w, dn_a_log, dn_dt_bias, dn_norm_g, dn_w_out,
           rk_mix, rk_w_rkv, rk_w0, rk_w1, rk_w2, rk_a0, rk_a1, rk_a2, rk_g1, rk_g2,
           rk_k_k, rk_k_a, rk_r_k, rk_lnx_g, rk_w_out,
           ffn_w_gu, ffn_w_down, moe_router, moe_w_gu, moe_w_down):
    bsz, t, d = x.shape
    n_ctx = ctx.shape[1]
    lay = Layout(bsz, n_ctx, t)
    bw = lambda w: w.astype(BF16)
    s_rows = jax.nn.silu(jnp.concatenate([c_ctx[None], c], 0))
    s_pad = jnp.zeros((8, d), F32).at[:1 + bsz].set(s_rows)
    mods = modulation_rows(s_pad, mod_w, mod_b)[:, :1 + bsz].reshape(DEPTH, 1 + bsz, 6, 1, d)
    mod = lambda i, k: mods[i, :, k]
    cos_t, sin_t = rope_tables(lay)
    xs = jnp.concatenate([ctx.reshape(-1, d), x.reshape(-1, d)], 0)
    h = modulate(lay, xs, mod(0, 1), mod(0, 0))
    for i in range(DEPTH):
        last = i == DEPTH - 1
        kind, j = i % N_MIXERS, i // N_MIXERS
        if kind == 0:
            p = proj(h, bw(ret_w_in[j]), BF16, 1024)
            log_gamma = jax.nn.log_sigmoid(ret_decay[j].astype(F32))
            a = retention_mix(lay, p, log_gamma, ret_gn_g[j], cos_t, sin_t)
            w_out = ret_w_out[j]
        elif kind == 1:
            n_main = 2 * DN_QK_W + 2 * DN_V_W
            p = proj(h, bw(dn_w_in[j][:, :n_main]), BF16, 1024)
            w_ab = bw(dn_w_in[j][:, n_main:])
            ab = proj(h, w_ab, F32, w_ab.shape[1])
            qkv = dn_prep(lay, p, dn_conv_w[j])
            a = deltanet_mix(lay, qkv, p, ab, dn_a_log[j], dn_dt_bias[j], dn_norm_g[j])
            w_out = dn_w_out[j]
        else:
            a = rwkv7_mix(lay, h, rk_mix[j], rk_w_rkv[j], rk_w0[j], rk_w1[j], rk_w2[j], rk_a0[j], rk_a1[j],
                          rk_a2[j], rk_g1[j], rk_g2[j], rk_k_k[j], rk_k_a[j], rk_r_k[j], rk_lnx_g[j])
            w_out = rk_w_out[j]
        xs, h = out_ln(lay, a, bw(w_out), xs, mod(i, 2), ln_g[i, 0], ln_b[i, 0], mod(i, 4), mod(i, 3))
        nxt = (i + 1) % DEPTH
        if i % 2 == 0:
            hm = swiglu_in(h, bw(ffn_w_gu[i // 2]), 256)
            xs, h = out_ln(lay, hm, bw(ffn_w_down[i // 2]), xs, mod(i, 5), ln_g[i, 1], ln_b[i, 1],
                           mod(nxt, 1), mod(nxt, 0))
        else:
            f = moe_swiglu(h, moe_router[i // 2], moe_w_gu, moe_w_down, i // 2)
            xs, h = out_ln(lay, f, None, xs, mod(i, 5), ln_g[i, 1], ln_b[i, 1], mod(nxt, 1), mod(nxt, 0))
    return xs[lay.ctx_tok:].reshape(bsz, t, d)
```

```python
import math, functools
import jax
import jax.numpy as jnp
from jax import lax
import numpy as np
from jax.experimental import pallas as pl
from jax.experimental.pallas import tpu as pltpu

D_MODEL = 1024
DEPTH = 4
GRID_W = 64
N_MIXERS = 3
ALPHA = (2 * DEPTH) ** 0.25
LN_EPS = 1e-5
GN_EPS = 1e-5
RMS_EPS = 1e-6
LNX_EPS = 64e-5
L2_EPS = 1e-6

RET_HEADS = 4
RET_DK = D_MODEL // RET_HEADS
RET_DV = 2 * RET_DK
RET_CHUNK = 256
ROPE_BASE = 10000.0

DN_QK_HEADS = 8
DN_V_HEADS = 16
DN_HEAD_DIM = 128
DN_CHUNK = 64
DN_STEP_CHUNKS = 4
DN_CONV = 5
DN_QK_W = DN_QK_HEADS * DN_HEAD_DIM
DN_V_W = DN_V_HEADS * DN_HEAD_DIM
SEQ_HALO = 16

RWKV_HEAD = 64
RWKV_HEADS = D_MODEL // RWKV_HEAD
RWKV_CHUNK = 64
RWKV_STEP_CHUNKS = 4

FFN_DIM = 2816
N_EXPERTS = 8
TOP_K = 2
EXPERT_DIM = 3584
MOE_ROWS = 1024
MOE_F_CHUNKS = 4
PROJ_COLS = 2048
FFN_COLS = FFN_DIM // 2

ROW_TILE = 512
VMEM_LIMIT = 48 * 1024 * 1024

BF16 = jnp.bfloat16
F32 = jnp.float32
_NT = (((1,), (1,)), ((), ()))
_TN = (((0,), (0,)), ((), ()))


def _bdot(x, y, dims=None):
    x = x.astype(BF16)
    y = y.astype(BF16)
    if dims is None:
        return jnp.dot(x, y, preferred_element_type=F32)
    return lax.dot_general(x, y, dims, preferred_element_type=F32)


def _hdot(x, y):
    return jnp.dot(x, y, precision=lax.Precision.HIGHEST, preferred_element_type=F32)


def _sigmoid(x):
    return 1.0 / (1.0 + jnp.exp(-x))


def _softplus(x):
    return jnp.maximum(x, 0.0) + jnp.log(1.0 + jnp.exp(-jnp.abs(x)))


def _params(*sem):
    return pltpu.CompilerParams(dimension_semantics=sem, vmem_limit_bytes=VMEM_LIMIT)


class Layout:
    def __init__(self, batch, n_ctx, seq):
        self.batch, self.n_ctx, self.seq = batch, n_ctx, seq
        self.ctx_tok = batch * n_ctx
        self.n_tok = self.ctx_tok + batch * seq
        self.row_tile = math.gcd(ROW_TILE, n_ctx * batch, seq)

    def mod_index(self, tile, j):
        r0 = j * tile
        return jnp.where(r0 < self.ctx_tok, 0, 1 + (r0 - self.ctx_tok) // self.seq)

    def seq_chunk(self, chunk, backward, i):
        nc, nl = self.n_ctx // chunk, self.seq // chunk
        if not backward:
            return i
        return jnp.where(i < nc, nc - 1 - i, 2 * nc + nl - 1 - i)

    def row_block(self, chunk, b, sc):
        nc, nl = self.n_ctx // chunk, self.seq // chunk
        return jnp.where(sc < nc, b * nc + sc, self.batch * nc + b * nl + sc - nc)

    def n_chunks(self, chunk):
        return (self.n_ctx + self.seq) // chunk


def _proj_kernel(h_ref, w_ref, o_ref):
    o_ref[...] = _bdot(h_ref[...], w_ref[...]).astype(o_ref.dtype)


def proj(h, w, out_dtype, tn):
    n_tok, k = h.shape
    n = w.shape[1]
    tm = math.gcd(1024, n_tok)
    return pl.pallas_call(
        _proj_kernel,
        grid=(n // tn, n_tok // tm),
        in_specs=[pl.BlockSpec((tm, k), lambda c, j: (j, 0)), pl.BlockSpec((k, tn), lambda c, j: (0, c))],
        out_specs=pl.BlockSpec((tm, tn), lambda c, j: (j, c)),
        out_shape=jax.ShapeDtypeStruct((n_tok, n), out_dtype),
        compiler_params=_params("arbitrary", "arbitrary"),
        name="proj",
    )(h, w)


def _swiglu_in_kernel(h_ref, wg_ref, wu_ref, o_ref):
    h = h_ref[...]
    g = _bdot(h, wg_ref[...])
    u = _bdot(h, wu_ref[...])
    o_ref[...] = (g * _sigmoid(g) * u).astype(o_ref.dtype)


def swiglu_in(h, w_gu, tn):
    n_tok, k = h.shape
    f = w_gu.shape[1] // 2
    tm = math.gcd(1024, n_tok)
    nf = f // tn
    return pl.pallas_call(
        _swiglu_in_kernel,
        grid=(nf, n_tok // tm),
        in_specs=[pl.BlockSpec((tm, k), lambda c, j: (j, 0)),
                  pl.BlockSpec((k, tn), lambda c, j: (0, c)),
                  pl.BlockSpec((k, tn), lambda c, j: (0, c + nf))],
        out_specs=pl.BlockSpec((tm, tn), lambda c, j: (j, c)),
        out_shape=jax.ShapeDtypeStruct((n_tok, f), BF16),
        compiler_params=_params("arbitrary", "arbitrary"),
        name="swiglu_in",
    )(h, w_gu, w_gu)


def _deepnorm_epilogue(x, f, ga_ref, g_ref, b_ref, sc_ref, sh_ref, x_out, h_out):
    y = ALPHA * x + (1.0 + ga_ref[...]) * f
    mu = jnp.mean(y, -1, keepdims=True)
    yc = y - mu
    var = jnp.mean(yc * yc, -1, keepdims=True)
    xn = yc * lax.rsqrt(var + LN_EPS) * g_ref[...] + b_ref[...]
    x_out[...] = xn
    h_out[...] = (xn * (1.0 + sc_ref[...]) + sh_ref[...]).astype(h_out.dtype)


def _out_ln_kernel(a_ref, w_ref, x_ref, ga_ref, g_ref, b_ref, sc_ref, sh_ref, x_out, h_out):
    _deepnorm_epilogue(x_ref[...], _bdot(a_ref[...], w_ref[...]), ga_ref, g_ref, b_ref, sc_ref, sh_ref, x_out, h_out)


def _resid_ln_kernel(f1_ref, f2_ref, x_ref, ga_ref, g_ref, b_ref, sc_ref, sh_ref, x_out, h_out):
    f = f1_ref[...].astype(F32) + f2_ref[...].astype(F32)
    _deepnorm_epilogue(x_ref[...], f, ga_ref, g_ref, b_ref, sc_ref, sh_ref, x_out, h_out)


def out_ln(lay, a, w, x, gate, ln_g, ln_b, sc_next, sh_next):
    n_tok, d = x.shape
    tm = lay.row_tile
    row = lambda j: (j, 0)
    mod = pl.BlockSpec((None, 1, d), lambda j: (lay.mod_index(tm, j), 0, 0))
    vec = pl.BlockSpec((1, d), lambda j: (0, 0))
    tok = pl.BlockSpec((tm, d), row)
    if w is None:
        body, lhs, lhs_specs = _resid_ln_kernel, tuple(a), [tok, tok]
    else:
        k = a.shape[1]
        body, lhs = _out_ln_kernel, (a, w)
        lhs_specs = [pl.BlockSpec((tm, k), row), pl.BlockSpec((k, d), lambda j: (0, 0))]
    return pl.pallas_call(
        body,
        grid=(n_tok // tm,),
        in_specs=lhs_specs + [tok, mod, vec, vec, mod, mod],
        out_specs=[tok, tok],
        out_shape=[jax.ShapeDtypeStruct((n_tok, d), F32), jax.ShapeDtypeStruct((n_tok, d), BF16)],
        compiler_params=_params("arbitrary"),
        name="out_ln",
    )(*lhs, x, gate, ln_g.reshape(1, d), ln_b.reshape(1, d), sc_next, sh_next)


def _modulate_kernel(x_ref, sc_ref, sh_ref, h_out):
    h_out[...] = (x_ref[...] * (1.0 + sc_ref[...]) + sh_ref[...]).astype(h_out.dtype)


def modulate(lay, x, sc, sh):
    n_tok, d = x.shape
    tm = lay.row_tile
    mod = pl.BlockSpec((None, 1, d), lambda j: (lay.mod_index(tm, j), 0, 0))
    tok = pl.BlockSpec((tm, d), lambda j: (j, 0))
    return pl.pallas_call(
        _modulate_kernel, grid=(n_tok // tm,), in_specs=[tok, mod, mod], out_specs=tok,
        out_shape=jax.ShapeDtypeStruct((n_tok, d), BF16), compiler_params=_params("arbitrary"), name="modulate",
    )(x, sc, sh)


def _mod_kernel(s_ref, w_ref, b_ref, o_ref):
    o_ref[...] = _bdot(s_ref[...], w_ref[...]) + b_ref[...]


def modulation_rows(s, mod_w, mod_b):
    r, d = s.shape
    depth, _, n = mod_w.shape
    tn = 1024
    return pl.pallas_call(
        _mod_kernel,
        grid=(depth, n // tn),
        in_specs=[pl.BlockSpec((r, d), lambda i, c: (0, 0)),
                  pl.BlockSpec((None, d, tn), lambda i, c: (i, 0, c)),
                  pl.BlockSpec((None, 1, tn), lambda i, c: (i, 0, c))],
        out_specs=pl.BlockSpec((None, r, tn), lambda i, c: (i, 0, c)),
        out_shape=jax.ShapeDtypeStruct((depth, r, n), F32),
        compiler_params=_params("arbitrary", "arbitrary"),
        name="modulation_rows",
    )(s, mod_w, mod_b.reshape(depth, 1, n))


def _rope(x, cos, sin):
    half = x.shape[1] // 2
    parts = []
    for p in range(2):
        xs = x[:, p * half:(p + 1) * half]
        parts.append(xs * cos[:, p * half:(p + 1) * half]
                     + pltpu.roll(xs, half // 2, axis=1) * sin[:, p * half:(p + 1) * half])
    return jnp.concatenate(parts, axis=1)


def _ret_heads(q_ref, k_ref, v_ref, cos_ref, sin_ref):
    cos, sin = cos_ref[...], sin_ref[...]
    hs = range(RET_HEADS)
    q = [_rope(q_ref[:, h * RET_DK:(h + 1) * RET_DK].astype(F32), cos, sin) for h in hs]
    k = [_rope(k_ref[:, h * RET_DK:(h + 1) * RET_DK].astype(F32), cos, sin) for h in hs]
    v = [v_ref[:, h * RET_DV:(h + 1) * RET_DV] for h in hs]
    return hs, q, k, v


def _ret_fwd_kernel(cd_ref, q_ref, k_ref, v_ref, cos_ref, sin_ref, dec_ref, rd_ref, o_ref, s_ref):
    i = pl.program_id(1)

    @pl.when(i == 0)
    def _():
        s_ref[...] = jnp.zeros_like(s_ref)

    hs, q, k, v = _ret_heads(q_ref, k_ref, v_ref, cos_ref, sin_ref)
    scores = [(_bdot(q[h], k[h], _NT) * dec_ref[h]).astype(BF16) for h in hs]
    s = [s_ref[h] for h in hs]
    qd = [(q[h] * rd_ref[0, h]).astype(BF16) for h in hs]
    kd = [(k[h] * rd_ref[1, h]).astype(BF16) for h in hs]
    for h in hs:
        o_ref[:, h * RET_DV:(h + 1) * RET_DV] = _bdot(scores[h], v[h]) + _bdot(qd[h], s[h])
    for h in hs:
        s_ref[h] = s[h] * cd_ref[0, h] + _bdot(kd[h], v[h], _TN)


def _ret_bwd_kernel(cd_ref, q_ref, k_ref, v_ref, g_ref, cos_ref, sin_ref, rd_ref, op_ref, gn_ref, o_ref, s_ref):
    i = pl.program_id(1)

    @pl.when(i == 0)
    def _():
        s_ref[...] = jnp.zeros_like(s_ref)

    hs, q, k, v = _ret_heads(q_ref, k_ref, v_ref, cos_ref, sin_ref)
    s = [s_ref[h] for h in hs]
    qd = [(q[h] * rd_ref[0, h]).astype(BF16) for h in hs]
    kd = [(k[h] * rd_ref[1, h]).astype(BF16) for h in hs]
    o = [op_ref[:, h * RET_DV:(h + 1) * RET_DV] + _bdot(qd[h], s[h]) for h in hs]
    for h in hs:
        s_ref[h] = s[h] * cd_ref[1, h] + _bdot(kd[h], v[h], _TN)
    for h in hs:
        hv = slice(h * RET_DV, (h + 1) * RET_DV)
        mu = jnp.mean(o[h], -1, keepdims=True)
        oc = o[h] - mu
        var = jnp.mean(oc * oc, -1, keepdims=True)
        g = g_ref[:, hv].astype(F32)
        o_ref[:, hv] = (g * _sigmoid(g) * (oc * lax.rsqrt(var + GN_EPS) * gn_ref[:, hv])).astype(o_ref.dtype)


def retention_mix(lay, p, log_gamma, gn_g, cos_t, sin_t):
    c = RET_CHUNK
    n = lay.n_chunks(c)
    n_tok = p.shape[0]
    hk = RET_HEADS * RET_DK
    hv = RET_HEADS * RET_DV

    def specs(backward):
        sc = lambda i: lay.seq_chunk(c, backward, i)
        blk = lambda w, off: pl.BlockSpec((c, w), lambda b, i: (lay.row_block(c, b, sc(i)), off))
        tab = pl.BlockSpec((c, RET_DK), lambda b, i: (sc(i), 0))
        return blk, tab

    pos = jnp.arange(c, dtype=F32)
    lag = pos[:, None] - pos[None, :]
    lg_f, lg_b = log_gamma[0][:, None, None], log_gamma[1][:, None, None]
    k_scale = RET_DK ** -0.5
    dec = (jnp.where(lag >= 0, jnp.exp(jnp.maximum(lag, 0.0) * lg_f), 0.0)
           + jnp.where(lag <= 0, jnp.exp(jnp.maximum(-lag, 0.0) * lg_b), 0.0)) * k_scale
    rows = lambda e, lg: jnp.exp(e[None, :] * lg[:, None])[..., None]
    rd_f = jnp.stack([rows(pos + 1.0, log_gamma[0]), rows(c - 1.0 - pos, log_gamma[0]) * k_scale])
    rd_b = jnp.stack([rows(c - pos, log_gamma[1]), rows(pos, log_gamma[1]) * k_scale])
    cd = jnp.exp(c * log_gamma)

    smem = pl.BlockSpec(memory_space=pltpu.SMEM)
    full = lambda a: pl.BlockSpec(a.shape, lambda b, i: (0,) * a.ndim)
    grid = (lay.batch, n)
    state = [pltpu.VMEM((RET_HEADS, RET_DK, RET_DV), F32)]
    blk, tab = specs(False)
    o_part = pl.pallas_call(
        _ret_fwd_kernel, grid=grid,
        in_specs=[smem, blk(hk, 0), blk(hk, 1), blk(hv, 1), tab, tab, full(dec), full(rd_f)],
        out_specs=blk(hv, 0),
        out_shape=jax.ShapeDtypeStruct((n_tok, hv), F32),
        scratch_shapes=state, compiler_params=_params("arbitrary", "arbitrary"), name="ret_fwd",
    )(cd, p, p, p, cos_t, sin_t, dec, rd_f)
    blk, tab = specs(True)
    return pl.pallas_call(
        _ret_bwd_kernel, grid=grid,
        in_specs=[smem, blk(hk, 0), blk(hk, 1), blk(hv, 1), blk(hv, 2), tab, tab, full(rd_b),
                  blk(hv, 0), pl.BlockSpec((1, hv), lambda b, i: (0, 0))],
        out_specs=blk(hv, 0),
        out_shape=jax.ShapeDtypeStruct((n_tok, hv), BF16),
        scratch_shapes=state, compiler_params=_params("arbitrary", "arbitrary"), name="ret_bwd",
    )(cd, p, p, p, p, cos_t, sin_t, rd_b, o_part, gn_g.reshape(1, hv))


def rope_tables(lay):
    quarter = RET_DK // 4
    t = jnp.arange(lay.seq)
    inv = ROPE_BASE ** (-jnp.arange(quarter, dtype=F32) / quarter)
    ang_r = (t // GRID_W).astype(F32)[:, None] * inv
    ang_c = (t % GRID_W).astype(F32)[:, None] * inv
    cos = jnp.concatenate([jnp.cos(ang_r)] * 2 + [jnp.cos(ang_c)] * 2, -1)
    sin = jnp.concatenate([-jnp.sin(ang_r), jnp.sin(ang_r), -jnp.sin(ang_c), jnp.sin(ang_c)], -1)
    cos = jnp.concatenate([jnp.ones((lay.n_ctx, RET_DK), F32), cos], 0)
    sin = jnp.concatenate([jnp.zeros((lay.n_ctx, RET_DK), F32), sin], 0)
    return cos, sin


def _dn_prep_kernel(lay, tm, prev_ref, x_ref, next_ref, w_ref, o_ref, xe_s):
    j = pl.program_id(0)
    ct = pl.program_id(1)
    r0 = j * tm
    in_ctx = r0 < lay.ctx_tok
    seq_len = jnp.where(in_ctx, lay.n_ctx, lay.seq)
    off = jnp.where(in_ctx, r0, r0 - lay.ctx_tok) % seq_len
    first = off == 0
    last = off + tm == seq_len
    hal = SEQ_HALO
    xe_s[0:hal, :] = jnp.where(first, 0.0, prev_ref[...].astype(F32))
    xe_s[hal:hal + tm, :] = x_ref[...].astype(F32)
    xe_s[hal + tm:, :] = jnp.where(last, 0.0, next_ref[...].astype(F32))
    pad = (DN_CONV - 1) // 2
    acc = xe_s[pl.ds(hal - pad, tm), :] * w_ref[0:1, :]
    for d in range(1, DN_CONV):
        acc = acc + xe_s[pl.ds(hal - pad + d, tm), :] * w_ref[d:d + 1, :]
    y = acc * _sigmoid(acc)
    n_qk_tiles = 2 * DN_QK_W // x_ref.shape[1]

    @pl.when(ct >= n_qk_tiles)
    def _():
        o_ref[...] = y.astype(o_ref.dtype)

    @pl.when(ct < n_qk_tiles)
    def _():
        is_q = ct * x_ref.shape[1] < DN_QK_W
        scale = jnp.where(is_q, DN_HEAD_DIM ** -0.5, 1.0)
        for s in range(x_ref.shape[1] // DN_HEAD_DIM):
            ys = y[:, s * DN_HEAD_DIM:(s + 1) * DN_HEAD_DIM]
            inv = lax.rsqrt(jnp.sum(ys * ys, -1, keepdims=True) + L2_EPS) * scale
            o_ref[:, s * DN_HEAD_DIM:(s + 1) * DN_HEAD_DIM] = (ys * inv).astype(o_ref.dtype)


def dn_prep(lay, p, conv_w):
    n_tok = p.shape[0]
    w = 2 * DN_QK_W + DN_V_W
    tm = math.gcd(256, lay.n_ctx, lay.seq)
    tc = 1024
    hb = tm // SEQ_HALO
    last_hb = n_tok // SEQ_HALO - 1
    return pl.pallas_call(
        functools.partial(_dn_prep_kernel, lay, tm),
        grid=(n_tok // tm, w // tc),
        in_specs=[pl.BlockSpec((SEQ_HALO, tc), lambda j, c: (jnp.maximum(j * hb - 1, 0), c)),
                  pl.BlockSpec((tm, tc), lambda j, c: (j, c)),
                  pl.BlockSpec((SEQ_HALO, tc), lambda j, c: (jnp.minimum((j + 1) * hb, last_hb), c)),
                  pl.BlockSpec((DN_CONV, tc), lambda j, c: (0, c))],
        out_specs=pl.BlockSpec((tm, tc), lambda j, c: (j, c)),
        out_shape=jax.ShapeDtypeStruct((n_tok, w), BF16),
        scratch_shapes=[pltpu.VMEM((tm + 2 * SEQ_HALO, tc), F32)],
        compiler_params=_params("arbitrary", "arbitrary"), name="dn_prep",
    )(p, p, p, conv_w)


def _dn_scan_kernel(backward, q_ref, k_ref, v_ref, ab_ref, na_row, dt_row, *rest):
    s_ref = rest[-1]

    @pl.when(pl.program_id(1) == 0)
    def _():
        s_ref[...] = jnp.zeros_like(s_ref)

    for j in range(DN_STEP_CHUNKS):
        sub = DN_STEP_CHUNKS - 1 - j if backward else j
        rows = lambda r: r.at[pl.ds(sub * DN_CHUNK, DN_CHUNK)]
        if backward:
            of_ref, z_ref, ng_ref, o_ref, _ = rest
            tail = (rows(of_ref), rows(z_ref), ng_ref, rows(o_ref), s_ref)
        else:
            tail = (rows(rest[0]), s_ref)
        _dn_chunk(backward, rows(q_ref), rows(k_ref), rows(v_ref), rows(ab_ref), na_row, dt_row, *tail)


def _dn_chunk(backward, q_ref, k_ref, v_ref, ab_ref, na_row, dt_row, *rest):
    if backward:
        of_ref, z_ref, ng_ref, o_ref, s_ref = rest
    else:
        o_ref, s_ref = rest
    c = q_ref.shape[0]
    z = 1 if backward else 0
    nh = DN_V_HEADS
    hd = DN_HEAD_DIM
    rep = DN_V_HEADS // DN_QK_HEADS

    row = lax.broadcasted_iota(jnp.int32, (c, c), 0)
    col = lax.broadcasted_iota(jnp.int32, (c, c), 1)
    lag = (col - row) if backward else (row - col)
    incl = lag >= 0
    strict = lag > 0
    eye = (row == col).astype(F32)
    ab = ab_ref[...]
    g_cols = na_row[...] * _softplus(ab + dt_row[...])
    beta_cols = _sigmoid(ab)
    gc_cols = _hdot(incl.astype(F32), g_cols)
    gc_rows = lax.dot_general(gc_cols, eye, _TN, precision=lax.Precision.HIGHEST,
                              preferred_element_type=F32)
    gend = jnp.sum(g_cols, axis=0, keepdims=True)
    heads = range(nh)
    gi = [z * 2 * nh + h for h in heads]
    bi = [z * 2 * nh + nh + h for h in heads]
    gcc = [gc_cols[:, gi[h]:gi[h] + 1] for h in heads]
    dec = [jnp.where(incl, jnp.exp(jnp.minimum(gcc[h] - gc_rows[gi[h]:gi[h] + 1, :], 0.0)), 0.0) for h in heads]
    beta = [beta_cols[:, bi[h]:bi[h] + 1] for h in heads]
    egc = [jnp.exp(gcc[h]) for h in heads]
    eend = [jnp.exp(gend[:, gi[h]:gi[h] + 1] - gcc[h]) for h in heads]
    tail = [jnp.exp(gend[:, gi[h]:gi[h] + 1]) for h in heads]
    qs = [q_ref[:, j * hd:(j + 1) * hd] for j in range(DN_QK_HEADS)]
    ks = [k_ref[:, j * hd:(j + 1) * hd] for j in range(DN_QK_HEADS)]
    vs = [v_ref[:, h * hd:(h + 1) * hd].astype(F32) for h in heads]
    kq = [_bdot(jnp.concatenate([ks[j], qs[j]], axis=0), ks[j], _NT) for j in range(DN_QK_HEADS)]
    neg_a = [jnp.where(strict, -(kq[h // rep][:c] * beta[h] * dec[h]), 0.0) for h in heads]
    qkd = [(kq[h // rep][c:] * dec[h]).astype(BF16) for h in heads]
    tm = _unit_tri_inverse(neg_a, eye, heads)
    kf = [ks[h // rep].astype(F32) for h in heads]
    rhs = [jnp.concatenate([vs[h] * beta[h], kf[h] * (beta[h] * egc[h])], axis=1) for h in heads]
    uw = [_bdot(tm[h], rhs[h]) for h in heads]
    s0 = [s_ref[h] for h in heads]
    lhs = [jnp.concatenate([uw[h][:, hd:], qs[h // rep].astype(F32) * egc[h]], axis=0) for h in heads]
    ws_qs = [_bdot(lhs[h], s0[h]) for h in heads]
    v_new = [(uw[h][:, :hd] - ws_qs[h][:c]).astype(BF16) for h in heads]
    o = [ws_qs[h][c:] + _bdot(qkd[h], v_new[h]) for h in heads]
    for h in heads:
        s_ref[h] = s0[h] * tail[h] + _bdot(kf[h] * eend[h], v_new[h], _TN)
    if not backward:
        for h in heads:
            o_ref[:, h * hd:(h + 1) * hd] = o[h]
    else:
        for h in heads:
            ot = o[h] + of_ref[:, h * hd:(h + 1) * hd]
            on = ot * lax.rsqrt(jnp.mean(ot * ot, -1, keepdims=True) + RMS_EPS) * ng_ref[...]
            zz = z_ref[:, h * hd:(h + 1) * hd].astype(F32)
            o_ref[:, h * hd:(h + 1) * hd] = (on * (zz * _sigmoid(zz))).astype(o_ref.dtype)


def deltanet_mix(lay, qkv, p, ab, a_log, dt_bias, norm_g):
    c = DN_CHUNK * DN_STEP_CHUNKS
    n = lay.n_chunks(c)
    n_tok = qkv.shape[0]
    nh = DN_V_HEADS
    neg_a = -jnp.exp(a_log.astype(F32))
    na = jnp.concatenate([neg_a, jnp.zeros_like(neg_a)], axis=1).reshape(1, 4 * nh)
    dt = jnp.concatenate([dt_bias.astype(F32), jnp.zeros_like(neg_a)], axis=1).reshape(1, 4 * nh)
    small = lambda a: pl.BlockSpec(a.shape, lambda b, i: (0, 0))
    consts = (na, dt)
    state = [pltpu.VMEM((nh, DN_HEAD_DIM, DN_HEAD_DIM), F32)]

    def specs(backward):
        rb = lambda b, i: lay.row_block(c, b, lay.seq_chunk(c, backward, i))
        return lambda w, off: pl.BlockSpec((c, w), lambda b, i: (rb(b, i), off))

    common = lambda blk: [blk(DN_QK_W, 0), blk(DN_QK_W, 1), blk(DN_V_W, 1), blk(4 * nh, 0)] + [small(a) for a in consts]
    blk = specs(False)
    o_f = pl.pallas_call(
        functools.partial(_dn_scan_kernel, False), grid=(lay.batch, n),
        in_specs=common(blk), out_specs=blk(DN_V_W, 0),
        out_shape=jax.ShapeDtypeStruct((n_tok, DN_V_W), F32),
        scratch_shapes=state, compiler_params=_params("arbitrary", "arbitrary"), name="dn_scan_fwd",
    )(qkv, qkv, qkv, ab, *consts)
    blk = specs(True)
    return pl.pallas_call(
        functools.partial(_dn_scan_kernel, True), grid=(lay.batch, n),
        in_specs=common(blk) + [blk(DN_V_W, 0), blk(DN_V_W, 2), small(norm_g.reshape(1, DN_HEAD_DIM))],
        out_specs=blk(DN_V_W, 0),
        out_shape=jax.ShapeDtypeStruct((n_tok, DN_V_W), BF16),
        scratch_shapes=state, compiler_params=_params("arbitrary", "arbitrary"), name="dn_scan_bwd",
    )(qkv, qkv, qkv, ab, *consts, o_f, p, norm_g.reshape(1, DN_HEAD_DIM))


def _unit_tri_inverse(nm, eye, heads):
    c = eye.shape[0]
    tm = [eye + nm[h] for h in heads]
    p = [_bdot(nm[h], nm[h]).astype(BF16) for h in heads]
    for _ in range(int(math.log2(c)) - 2):
        pt = [_bdot(jnp.concatenate([p[h], tm[h].astype(BF16)], axis=0), p[h]) for h in heads]
        tm = [tm[h] + pt[h][c:] for h in heads]
        p = [pt[h][:c].astype(BF16) for h in heads]
    return [tm[h] + _bdot(tm[h], p[h]) for h in heads]


def _rwkv_chunk_kernel(r_ref, v_ref, kk_ref, wl_ref, kd_ref, a_ref, o_ref, s_ref, *scratch):
    z = pl.program_id(0)

    @pl.when(pl.program_id(2) == 0)
    def _():
        s_ref[...] = jnp.zeros_like(s_ref)

    for j in range(RWKV_STEP_CHUNKS):
        sub = j + z * (RWKV_STEP_CHUNKS - 1 - 2 * j)
        rows = lambda r: r.at[pl.ds(pl.multiple_of(sub * RWKV_CHUNK, RWKV_CHUNK), RWKV_CHUNK)]
        _rwkv_chunk(z, rows(r_ref), rows(v_ref), rows(kk_ref), rows(wl_ref), rows(kd_ref), rows(a_ref),
                    rows(o_ref), s_ref, *scratch)


def _rwkv_chunk(z, r_ref, v_ref, kk_ref, wl_ref, kd_ref, a_ref, o_ref, s_ref, ar_s, bt_s, bk_s, uv_s, gc_s):
    c = r_ref.shape[0]
    row = lax.broadcasted_iota(jnp.int32, (c, c), 0)
    col = lax.broadcasted_iota(jnp.int32, (c, c), 1)
    lag = (row - col) * (1 - 2 * z)
    incl = lag >= 0
    strict = lag > 0

    wl = wl_ref[...]
    logw = -jnp.exp(-_softplus(-wl) - 0.5)
    cum = _hdot(incl.astype(F32), logw)
    c_last = jnp.sum(logw, axis=0, keepdims=True)
    kk = kk_ref[...].astype(F32)
    kb = kk * a_ref[...].astype(F32)
    kd = kd_ref[...].astype(F32)
    g_inv = jnp.exp(-cum)
    e_end = jnp.exp(c_last - cum)
    ar_s[0:c, :] = (-kk * jnp.exp(cum - logw)).astype(BF16)
    ar_s[c:, :] = (r_ref[...].astype(F32) * jnp.exp(cum)).astype(BF16)
    bt_s[0:c, :] = (kb * g_inv).astype(BF16)
    bt_s[c:, :] = (kd * g_inv).astype(BF16)
    bk_s[0:c, :] = (kb * e_end).astype(BF16)
    bk_s[c:, :] = (kd * e_end).astype(BF16)
    uv_s[c:, :] = v_ref[...].astype(BF16)
    gc_s[...] = jnp.exp(c_last)

    eye = (row == col).astype(F32)
    hh = range(RWKV_HEADS)
    sl = [slice(h * RWKV_HEAD, (h + 1) * RWKV_HEAD) for h in hh]
    lag2 = jnp.concatenate([lag, lag + 1], axis=0)
    mask4 = jnp.concatenate([lag2, lag2], axis=1) > 0
    x4 = [jnp.where(mask4, _bdot(ar_s[:, sl[h]], bt_s[:, sl[h]], _NT), 0.0) for h in hh]
    xb = [x4[h][:, :c] for h in hh]
    xk = [x4[h][:, c:].astype(BF16) for h in hh]
    tm = _unit_tri_inverse([xb[h][:c] for h in hh], eye, hh)
    s0 = [s_ref[h] for h in hh]
    xs = [_bdot(ar_s[:, sl[h]], s0[h], _NT) + _bdot(xk[h], uv_s[c:, sl[h]]) for h in hh]
    u = [_bdot(tm[h], xs[h][:c]).astype(BF16) for h in hh]
    for h in hh:
        uv_s[0:c, sl[h]] = u[h]
        o_ref[:, sl[h]] = xs[h][c:] + _bdot(xb[h][c:], u[h])
    for h in hh:
        s_ref[h] = s0[h] * gc_s[:, sl[h]] + _bdot(uv_s[:, sl[h]], bk_s[:, sl[h]], _TN)


def rwkv_scan(lay, r, v, kk, wl, kd, a):
    n_tok, d = r.shape
    step = RWKV_CHUNK * RWKV_STEP_CHUNKS
    n = lay.n_chunks(step)

    def rb(z, b, i):
        sc = jnp.where(z == 0, lay.seq_chunk(step, False, i), lay.seq_chunk(step, True, i))
        return lay.row_block(step, b, sc)

    shared = pl.BlockSpec((step, d), lambda z, b, i: (rb(z, b, i), 0))
    perdir = pl.BlockSpec((None, step, d), lambda z, b, i: (z, rb(z, b, i), 0))
    c = RWKV_CHUNK
    bf = lambda rows: pltpu.VMEM((rows, d), BF16)
    return pl.pallas_call(
        _rwkv_chunk_kernel,
        grid=(2, lay.batch, n),
        in_specs=[shared, shared, shared, perdir, perdir, perdir],
        out_specs=perdir,
        out_shape=jax.ShapeDtypeStruct((2, n_tok, d), F32),
        scratch_shapes=[pltpu.VMEM((RWKV_HEADS, RWKV_HEAD, RWKV_HEAD), F32),
                        bf(2 * c), bf(2 * c), bf(2 * c), bf(2 * c), pltpu.VMEM((1, d), F32)],
        compiler_params=_params("arbitrary", "arbitrary", "arbitrary"),
        name="rwkv_scan",
    )(r, v, kk, wl, kd, a)


def _rwkv_pre_kernel(lay, tm, prev_ref, h_ref, next_ref, mix_ref, wr_ref, wk_ref, wv_ref, w1_ref, a1_ref, g1_ref,
                     w2_ref, a2_ref, g2_ref, w0_ref, a0_ref, kk_ref, ka_ref, rk_ref, seg_ref,
                     r_out, v_out, kk_out, wl_out, kd_out, a_out, g_out, bonus_out, xe_s):
    j = pl.program_id(0)
    r0 = j * tm
    in_ctx = r0 < lay.ctx_tok
    seq_len = jnp.where(in_ctx, lay.n_ctx, lay.seq)
    off = jnp.where(in_ctx, r0, r0 - lay.ctx_tok) % seq_len
    hal = SEQ_HALO
    xe_s[0:hal, :] = jnp.where(off == 0, 0.0, prev_ref[...].astype(F32))
    xe_s[hal:hal + tm, :] = h_ref[...].astype(F32)
    xe_s[hal + tm:, :] = jnp.where(off + tm == seq_len, 0.0, next_ref[...].astype(F32))
    h = xe_s[pl.ds(hal, tm), :]
    xx = 0.5 * (xe_s[pl.ds(hal - 1, tm), :] + xe_s[pl.ds(hal + 1, tm), :]) - h
    xm = lambda i: (h + xx * mix_ref[i:i + 1, :]).astype(BF16)
    r = _bdot(xm(0), wr_ref[...])
    k = _bdot(xm(1), wk_ref[...])
    v = _bdot(xm(2), wv_ref[...])
    hw = jnp.tanh(_bdot(xm(3), w1_ref[...]))
    ha = _bdot(xm(4), a1_ref[...])
    hg = _sigmoid(_bdot(xm(5), g1_ref[...]))
    seg = seg_ref[...]
    kx = k * kk_ref[...]
    r_out[...] = r.astype(r_out.dtype)
    v_out[...] = v.astype(v_out.dtype)
    kk_out[...] = (kx * lax.rsqrt(_bdot(kx * kx, seg) + L2_EPS)).astype(kk_out.dtype)
    g_out[...] = _bdot(hg, g2_ref[...]).astype(g_out.dtype)
    lw = w2_ref.shape[1]
    rr = r * rk_ref[...]
    bsum = None
    for z in range(2):
        wl_out[z] = w0_ref[z:z + 1, :] + _bdot(hw[:, z * lw:(z + 1) * lw], w2_ref[z])
        a = _sigmoid(a0_ref[z:z + 1, :] + _bdot(ha[:, z * lw:(z + 1) * lw], a2_ref[z]))
        kd = k * (1.0 + (a - 1.0) * ka_ref[...])
        a_out[z] = a.astype(a_out.dtype)
        kd_out[z] = kd.astype(kd_out.dtype)
        bsum = rr * kd if bsum is None else bsum + rr * kd
    bonus_out[...] = (_bdot(bsum, seg) * v).astype(bonus_out.dtype)


def _rwkv_post_kernel(o_ref, g_ref, bonus_ref, lnx_ref, seg_ref, a_out):
    seg = seg_ref[...]
    o = o_ref[0] + o_ref[1]
    hi = o.astype(BF16)
    lo = o - hi.astype(F32)
    inv_n = 1.0 / RWKV_HEAD
    oc = o - (_bdot(hi, seg) + _bdot(lo, seg)) * inv_n
    var = _bdot(oc * oc, seg) * inv_n
    on = oc * lax.rsqrt(var + LNX_EPS) * lnx_ref[...] + bonus_ref[...].astype(F32)
    a_out[...] = (on * g_ref[...].astype(F32)).astype(a_out.dtype)


def rwkv7_mix(lay, h, mix, w_rkv, w0, w1, w2, a0, a1, a2, g1, g2, k_k, k_a, r_k, lnx_g):
    n_tok, d = h.shape
    bw = lambda w: w.astype(BF16)
    tm = math.gcd(256, lay.n_ctx, lay.seq)
    hb = tm // SEQ_HALO
    last_hb = n_tok // SEQ_HALO - 1
    head_of = jnp.arange(d) // RWKV_HEAD
    seg = (head_of[:, None] == head_of[None, :]).astype(BF16)
    full = lambda a: pl.BlockSpec(a.shape, lambda j: (0,) * a.ndim)
    row = lambda a: a.reshape(1, d)
    consts = (mix, bw(w_rkv[0]), bw(w_rkv[1]), bw(w_rkv[2]), bw(jnp.concatenate([w1[0], w1[1]], -1)),
              bw(jnp.concatenate([a1[0], a1[1]], -1)), bw(g1), bw(w2), bw(a2), bw(g2), w0, a0,
              row(k_k), row(k_a), row(r_k), seg)
    tok = pl.BlockSpec((tm, d), lambda j: (j, 0))
    tok2 = pl.BlockSpec((2, tm, d), lambda j: (0, j, 0))
    one = lambda dt: jax.ShapeDtypeStruct((n_tok, d), dt)
    two = lambda dt: jax.ShapeDtypeStruct((2, n_tok, d), dt)
    r, v, kk, wl, kd, a, g, bonus = pl.pallas_call(
        functools.partial(_rwkv_pre_kernel, lay, tm),
        grid=(n_tok // tm,),
        in_specs=[pl.BlockSpec((SEQ_HALO, d), lambda j: (jnp.maximum(j * hb - 1, 0), 0)), tok,
                  pl.BlockSpec((SEQ_HALO, d), lambda j: (jnp.minimum((j + 1) * hb, last_hb), 0))]
                 + [full(a) for a in consts],
        out_specs=[tok, tok, tok, tok2, tok2, tok2, tok, tok],
        out_shape=[one(BF16), one(BF16), one(BF16), two(F32), two(BF16), two(BF16), one(BF16), one(BF16)],
        scratch_shapes=[pltpu.VMEM((tm + 2 * SEQ_HALO, d), F32)],
        compiler_params=_params("arbitrary"), name="rwkv_pre",
    )(h, h, h, *consts)
    o = rwkv_scan(lay, r, v, kk, wl, kd, a)
    return pl.pallas_call(
        _rwkv_post_kernel,
        grid=(n_tok // tm,),
        in_specs=[tok2, tok, tok, full(row(lnx_g)), full(seg)],
        out_specs=tok,
        out_shape=one(BF16),
        compiler_params=_params("arbitrary"), name="rwkv_post",
    )(o, g, bonus, row(lnx_g), seg)


def _moe_kernel(be_ref, nb_ref, x_ref, wg_ref, wu_ref, wd_ref, gate_ref, o_ref, acc_s):
    j = pl.program_id(0)
    f = pl.program_id(1)

    @pl.when(f == 0)
    def _():
        acc_s[...] = jnp.zeros_like(acc_s)

    @pl.when(j < nb_ref[0])
    def _():
        x = x_ref[...]
        g = _bdot(x, wg_ref[...])
        u = _bdot(x, wu_ref[...])
        acc_s[...] += _bdot(g * _sigmoid(g) * u, wd_ref[...])

    @pl.when(f == pl.num_programs(1) - 1)
    def _():
        o_ref[...] = (acc_s[...] * gate_ref[...]).astype(o_ref.dtype)


def moe_experts(xb, block_e, n_used, w_gu, w_down, layer, slot_gate):
    n_slots, d = xb.shape
    bm = MOE_ROWS
    nf = MOE_F_CHUNKS
    tf = w_down.shape[2] // nf
    return pl.pallas_call(
        _moe_kernel,
        grid_spec=pltpu.PrefetchScalarGridSpec(
            num_scalar_prefetch=2,
            grid=(n_slots // bm, nf),
            in_specs=[pl.BlockSpec((bm, d), lambda j, f, be, nb: (j, 0)),
                      pl.BlockSpec((None, None, d, tf), lambda j, f, be, nb: (layer, be[j], 0, f)),
                      pl.BlockSpec((None, None, d, tf), lambda j, f, be, nb: (layer, be[j], 0, f + nf)),
                      pl.BlockSpec((None, None, tf, d), lambda j, f, be, nb: (layer, be[j], f, 0)),
                      pl.BlockSpec((bm, 1), lambda j, f, be, nb: (j, 0))],
            out_specs=pl.BlockSpec((bm, d), lambda j, f, be, nb: (j, 0)),
            scratch_shapes=[pltpu.VMEM((bm, d), F32)]),
        out_shape=jax.ShapeDtypeStruct((n_slots, d), BF16),
        compiler_params=_params("arbitrary", "arbitrary"),
        name="moe_experts",
    )(block_e, n_used, xb, w_gu, w_gu, w_down, slot_gate)


def moe_swiglu(h, w_router, w_gu, w_down, layer):
    n, d = h.shape
    logits = proj(h, w_router.astype(BF16), F32, N_EXPERTS)
    top_logit, top_e = lax.top_k(logits, TOP_K)
    gate = jax.nn.softmax(top_logit, axis=-1)
    flat_e = top_e.reshape(-1).astype(jnp.int32)
    order = jnp.argsort(flat_e).astype(jnp.int32)
    onehot = (flat_e[:, None] == jnp.arange(N_EXPERTS, dtype=jnp.int32)).astype(jnp.int32)
    seen = jnp.cumsum(onehot, axis=0)
    counts = seen[-1]
    rank = jnp.sum(seen * onehot, axis=1) - 1
    padded = (counts + MOE_ROWS - 1) // MOE_ROWS * MOE_ROWS
    start = jnp.cumsum(counts) - counts
    pend = jnp.cumsum(padded)
    pstart = pend - padded
    n_slots = (n * TOP_K + MOE_ROWS - 1) // MOE_ROWS * MOE_ROWS + N_EXPERTS * MOE_ROWS
    n_blocks = n_slots // MOE_ROWS
    block_e = jnp.minimum(jnp.sum(jnp.arange(n_blocks)[:, None] * MOE_ROWS >= pend[None, :], axis=1),
                          N_EXPERTS - 1).astype(jnp.int32)
    n_used = (pend[-1] // MOE_ROWS).astype(jnp.int32).reshape(1)
    slot_e = jnp.repeat(block_e, MOE_ROWS)
    slot_off = jnp.arange(n_slots, dtype=jnp.int32) - pstart[slot_e]
    slot_valid = slot_off < counts[slot_e]
    slot_asg = order[jnp.clip(start[slot_e] + slot_off, 0, n * TOP_K - 1)]
    slot_tok = jnp.where(slot_valid, slot_asg // TOP_K, 0)
    slot_gate = jnp.where(slot_valid, gate.reshape(-1)[slot_asg], 0.0)
    yb = moe_experts(h[slot_tok], block_e, n_used, w_gu, w_down, layer, slot_gate[:, None])
    tok_slot = (pstart[flat_e] + rank).reshape(n, TOP_K)
    return yb[tok_slot[:, 0]], yb[tok_slot[:, 1]]


def kernel(x, c, ctx, c_ctx, mod_w, mod_b, ln_g, ln_b,
           ret_w_in, ret_decay, ret_gn_g, ret_w_out,
           dn_w_in, dn_conv_w, dn_a_log, dn_dt_bias, dn_norm_g, dn_w_out,
           rk_mix, rk_w_rkv, rk_w0, rk_w1, rk_w2, rk_a0, rk_a1, rk_a2, rk_g1, rk_g2,
           rk_k_k, rk_k_a, rk_r_k, rk_lnx_g, rk_w_out,
           ffn_w_gu, ffn_w_down, moe_router, moe_w_gu, moe_w_down):
    bsz, t, d = x.shape
    n_ctx = ctx.shape[1]
    lay = Layout(bsz, n_ctx, t)
    bw = lambda w: w.astype(BF16)
    s_rows = jax.nn.silu(jnp.concatenate([c_ctx[None], c], 0))
    s_pad = jnp.zeros((8, d), F32).at[:1 + bsz].set(s_rows)
    mods = modulation_rows(s_pad, mod_w, mod_b)[:, :1 + bsz].reshape(DEPTH, 1 + bsz, 6, 1, d)
    mod = lambda i, k: mods[i, :, k]
    cos_t, sin_t = rope_tables(lay)
    xs = jnp.concatenate([ctx.reshape(-1, d), x.reshape(-1, d)], 0)
    h = modulate(lay, xs, mod(0, 1), mod(0, 0))
    for i in range(DEPTH):
        last = i == DEPTH - 1
        kind, j = i % N_MIXERS, i // N_MIXERS
        if kind == 0:
            p = proj(h, bw(ret_w_in[j]), BF16, PROJ_COLS)
            log_gamma = jax.nn.log_sigmoid(ret_decay[j].astype(F32))
            a = retention_mix(lay, p, log_gamma, ret_gn_g[j], cos_t, sin_t)
            w_out = ret_w_out[j]
        elif kind == 1:
            n_main = 2 * DN_QK_W + 2 * DN_V_W
            p = proj(h, bw(dn_w_in[j][:, :n_main]), BF16, PROJ_COLS)
            w_ab = bw(dn_w_in[j][:, n_main:])
            ab = proj(h, w_ab, F32, w_ab.shape[1])
            qkv = dn_prep(lay, p, dn_conv_w[j])
            a = deltanet_mix(lay, qkv, p, ab, dn_a_log[j], dn_dt_bias[j], dn_norm_g[j])
            w_out = dn_w_out[j]
        else:
            a = rwkv7_mix(lay, h, rk_mix[j], rk_w_rkv[j], rk_w0[j], rk_w1[j], rk_w2[j], rk_a0[j], rk_a1[j],
                          rk_a2[j], rk_g1[j], rk_g2[j], rk_k_k[j], rk_k_a[j], rk_r_k[j], rk_lnx_g[j])
            w_out = rk_w_out[j]
        xs, h = out_ln(lay, a, bw(w_out), xs, mod(i, 2), ln_g[i, 0], ln_b[i, 0], mod(i, 4), mod(i, 3))
        nxt = (i + 1) % DEPTH
        if i % 2 == 0:
            hm = swiglu_in(h, bw(ffn_w_gu[i // 2]), FFN_COLS)
            xs, h = out_ln(lay, hm, bw(ffn_w_down[i // 2]), xs, mod(i, 5), ln_g[i, 1], ln_b[i, 1],
                           mod(nxt, 1), mod(nxt, 0))
        else:
            f = moe_swiglu(h, moe_router[i // 2], moe_w_gu, moe_w_down, i // 2)
            xs, h = out_ln(lay, f, None, xs, mod(i, 5), ln_g[i, 1], ln_b[i, 1], mod(nxt, 1), mod(nxt, 0))
    return xs[lay.ctx_tok:].reshape(bsz, t, d)
```

```python
import math, functools
import jax
import jax.numpy as jnp
from jax import lax
import numpy as np
from jax.experimental import pallas as pl
from jax.experimental.pallas import tpu as pltpu

D_MODEL = 1024
DEPTH = 4
GRID_W = 64
N_MIXERS = 3
ALPHA = (2 * DEPTH) ** 0.25
LN_EPS = 1e-5
GN_EPS = 1e-5
RMS_EPS = 1e-6
LNX_EPS = 64e-5
L2_EPS = 1e-6

RET_HEADS = 4
RET_DK = D_MODEL // RET_HEADS
RET_DV = 2 * RET_DK
RET_CHUNK = 256
ROPE_BASE = 10000.0

DN_QK_HEADS = 8
DN_V_HEADS = 16
DN_HEAD_DIM = 128
DN_CHUNK = 64
DN_STEP_CHUNKS = 4
DN_CONV = 5
DN_QK_W = DN_QK_HEADS * DN_HEAD_DIM
DN_V_W = DN_V_HEADS * DN_HEAD_DIM
SEQ_HALO = 16

RWKV_HEAD = 64
RWKV_HEADS = D_MODEL // RWKV_HEAD
RWKV_CHUNK = 64
RWKV_STEP_CHUNKS = 4

FFN_DIM = 2816
N_EXPERTS = 8
TOP_K = 2
EXPERT_DIM = 3584
MOE_ROWS = 1024
MOE_F_CHUNKS = 7
PROJ_COLS = 2048
FFN_COLS = FFN_DIM // 2

ROW_TILE = 512
VMEM_LIMIT = 48 * 1024 * 1024

BF16 = jnp.bfloat16
F32 = jnp.float32
_NT = (((1,), (1,)), ((), ()))
_TN = (((0,), (0,)), ((), ()))


def _bdot(x, y, dims=None):
    x = x.astype(BF16)
    y = y.astype(BF16)
    if dims is None:
        return jnp.dot(x, y, preferred_element_type=F32)
    return lax.dot_general(x, y, dims, preferred_element_type=F32)


def _hdot(x, y):
    return jnp.dot(x, y, precision=lax.Precision.HIGHEST, preferred_element_type=F32)


def _sigmoid(x):
    return 1.0 / (1.0 + jnp.exp(-x))


def _softplus(x):
    return jnp.maximum(x, 0.0) + jnp.log(1.0 + jnp.exp(-jnp.abs(x)))


def _params(*sem):
    return pltpu.CompilerParams(dimension_semantics=sem, vmem_limit_bytes=VMEM_LIMIT)


class Layout:
    def __init__(self, batch, n_ctx, seq):
        self.batch, self.n_ctx, self.seq = batch, n_ctx, seq
        self.ctx_tok = batch * n_ctx
        self.n_tok = self.ctx_tok + batch * seq
        self.row_tile = math.gcd(ROW_TILE, n_ctx * batch, seq)

    def mod_index(self, tile, j):
        r0 = j * tile
        return jnp.where(r0 < self.ctx_tok, 0, 1 + (r0 - self.ctx_tok) // self.seq)

    def seq_chunk(self, chunk, backward, i):
        nc, nl = self.n_ctx // chunk, self.seq // chunk
        if not backward:
            return i
        return jnp.where(i < nc, nc - 1 - i, 2 * nc + nl - 1 - i)

    def row_block(self, chunk, b, sc):
        nc, nl = self.n_ctx // chunk, self.seq // chunk
        return jnp.where(sc < nc, b * nc + sc, self.batch * nc + b * nl + sc - nc)

    def n_chunks(self, chunk):
        return (self.n_ctx + self.seq) // chunk


def _proj_kernel(h_ref, w_ref, o_ref):
    o_ref[...] = _bdot(h_ref[...], w_ref[...]).astype(o_ref.dtype)


def proj(h, w, out_dtype, tn):
    n_tok, k = h.shape
    n = w.shape[1]
    tm = math.gcd(1024, n_tok)
    return pl.pallas_call(
        _proj_kernel,
        grid=(n // tn, n_tok // tm),
        in_specs=[pl.BlockSpec((tm, k), lambda c, j: (j, 0)), pl.BlockSpec((k, tn), lambda c, j: (0, c))],
        out_specs=pl.BlockSpec((tm, tn), lambda c, j: (j, c)),
        out_shape=jax.ShapeDtypeStruct((n_tok, n), out_dtype),
        compiler_params=_params("arbitrary", "arbitrary"),
        name="proj",
    )(h, w)


def _swiglu_in_kernel(h_ref, wg_ref, wu_ref, o_ref):
    h = h_ref[...]
    g = _bdot(h, wg_ref[...])
    u = _bdot(h, wu_ref[...])
    o_ref[...] = (g * _sigmoid(g) * u).astype(o_ref.dtype)


def swiglu_in(h, w_gu, tn):
    n_tok, k = h.shape
    f = w_gu.shape[1] // 2
    tm = math.gcd(1024, n_tok)
    nf = f // tn
    return pl.pallas_call(
        _swiglu_in_kernel,
        grid=(nf, n_tok // tm),
        in_specs=[pl.BlockSpec((tm, k), lambda c, j: (j, 0)),
                  pl.BlockSpec((k, tn), lambda c, j: (0, c)),
                  pl.BlockSpec((k, tn), lambda c, j: (0, c + nf))],
        out_specs=pl.BlockSpec((tm, tn), lambda c, j: (j, c)),
        out_shape=jax.ShapeDtypeStruct((n_tok, f), BF16),
        compiler_params=_params("arbitrary", "arbitrary"),
        name="swiglu_in",
    )(h, w_gu, w_gu)


def _deepnorm_epilogue(x, f, ga_ref, g_ref, b_ref, sc_ref, sh_ref, x_out, h_out):
    y = ALPHA * x + (1.0 + ga_ref[...]) * f
    mu = jnp.mean(y, -1, keepdims=True)
    yc = y - mu
    var = jnp.mean(yc * yc, -1, keepdims=True)
    xn = yc * lax.rsqrt(var + LN_EPS) * g_ref[...] + b_ref[...]
    x_out[...] = xn
    h_out[...] = (xn * (1.0 + sc_ref[...]) + sh_ref[...]).astype(h_out.dtype)


def _out_ln_kernel(a_ref, w_ref, x_ref, ga_ref, g_ref, b_ref, sc_ref, sh_ref, x_out, h_out):
    _deepnorm_epilogue(x_ref[...], _bdot(a_ref[...], w_ref[...]), ga_ref, g_ref, b_ref, sc_ref, sh_ref, x_out, h_out)


def _resid_ln_kernel(f1_ref, f2_ref, x_ref, ga_ref, g_ref, b_ref, sc_ref, sh_ref, x_out, h_out):
    f = f1_ref[...].astype(F32) + f2_ref[...].astype(F32)
    _deepnorm_epilogue(x_ref[...], f, ga_ref, g_ref, b_ref, sc_ref, sh_ref, x_out, h_out)


def out_ln(lay, a, w, x, gate, ln_g, ln_b, sc_next, sh_next):
    n_tok, d = x.shape
    tm = lay.row_tile
    row = lambda j: (j, 0)
    mod = pl.BlockSpec((None, 1, d), lambda j: (lay.mod_index(tm, j), 0, 0))
    vec = pl.BlockSpec((1, d), lambda j: (0, 0))
    tok = pl.BlockSpec((tm, d), row)
    if w is None:
        body, lhs, lhs_specs = _resid_ln_kernel, tuple(a), [tok, tok]
    else:
        k = a.shape[1]
        body, lhs = _out_ln_kernel, (a, w)
        lhs_specs = [pl.BlockSpec((tm, k), row), pl.BlockSpec((k, d), lambda j: (0, 0))]
    return pl.pallas_call(
        body,
        grid=(n_tok // tm,),
        in_specs=lhs_specs + [tok, mod, vec, vec, mod, mod],
        out_specs=[tok, tok],
        out_shape=[jax.ShapeDtypeStruct((n_tok, d), F32), jax.ShapeDtypeStruct((n_tok, d), BF16)],
        compiler_params=_params("arbitrary"),
        name="out_ln",
    )(*lhs, x, gate, ln_g.reshape(1, d), ln_b.reshape(1, d), sc_next, sh_next)


def _modulate_kernel(x_ref, sc_ref, sh_ref, h_out):
    h_out[...] = (x_ref[...] * (1.0 + sc_ref[...]) + sh_ref[...]).astype(h_out.dtype)


def modulate(lay, x, sc, sh):
    n_tok, d = x.shape
    tm = lay.row_tile
    mod = pl.BlockSpec((None, 1, d), lambda j: (lay.mod_index(tm, j), 0, 0))
    tok = pl.BlockSpec((tm, d), lambda j: (j, 0))
    return pl.pallas_call(
        _modulate_kernel, grid=(n_tok // tm,), in_specs=[tok, mod, mod], out_specs=tok,
        out_shape=jax.ShapeDtypeStruct((n_tok, d), BF16), compiler_params=_params("arbitrary"), name="modulate",
    )(x, sc, sh)


def _mod_kernel(s_ref, w_ref, b_ref, o_ref):
    o_ref[...] = _bdot(s_ref[...], w_ref[...]) + b_ref[...]


def modulation_rows(s, mod_w, mod_b):
    r, d = s.shape
    depth, _, n = mod_w.shape
    tn = 1024
    return pl.pallas_call(
        _mod_kernel,
        grid=(depth, n // tn),
        in_specs=[pl.BlockSpec((r, d), lambda i, c: (0, 0)),
                  pl.BlockSpec((None, d, tn), lambda i, c: (i, 0, c)),
                  pl.BlockSpec((None, 1, tn), lambda i, c: (i, 0, c))],
        out_specs=pl.BlockSpec((None, r, tn), lambda i, c: (i, 0, c)),
        out_shape=jax.ShapeDtypeStruct((depth, r, n), F32),
        compiler_params=_params("arbitrary", "arbitrary"),
        name="modulation_rows",
    )(s, mod_w, mod_b.reshape(depth, 1, n))


def _rope(x, cos, sin):
    half = x.shape[1] // 2
    parts = []
    for p in range(2):
        xs = x[:, p * half:(p + 1) * half]
        parts.append(xs * cos[:, p * half:(p + 1) * half]
                     + pltpu.roll(xs, half // 2, axis=1) * sin[:, p * half:(p + 1) * half])
    return jnp.concatenate(parts, axis=1)


def _ret_heads(q_ref, k_ref, v_ref, cos_ref, sin_ref):
    cos, sin = cos_ref[...], sin_ref[...]
    hs = range(RET_HEADS)
    q = [_rope(q_ref[:, h * RET_DK:(h + 1) * RET_DK].astype(F32), cos, sin) for h in hs]
    k = [_rope(k_ref[:, h * RET_DK:(h + 1) * RET_DK].astype(F32), cos, sin) for h in hs]
    v = [v_ref[:, h * RET_DV:(h + 1) * RET_DV] for h in hs]
    return hs, q, k, v


def _ret_fwd_kernel(cd_ref, q_ref, k_ref, v_ref, cos_ref, sin_ref, dec_ref, rd_ref, o_ref, s_ref):
    i = pl.program_id(1)

    @pl.when(i == 0)
    def _():
        s_ref[...] = jnp.zeros_like(s_ref)

    hs, q, k, v = _ret_heads(q_ref, k_ref, v_ref, cos_ref, sin_ref)
    scores = [(_bdot(q[h], k[h], _NT) * dec_ref[h]).astype(BF16) for h in hs]
    s = [s_ref[h] for h in hs]
    qd = [(q[h] * rd_ref[0, h]).astype(BF16) for h in hs]
    kd = [(k[h] * rd_ref[1, h]).astype(BF16) for h in hs]
    for h in hs:
        o_ref[:, h * RET_DV:(h + 1) * RET_DV] = _bdot(scores[h], v[h]) + _bdot(qd[h], s[h])
    for h in hs:
        s_ref[h] = s[h] * cd_ref[0, h] + _bdot(kd[h], v[h], _TN)


def _ret_bwd_kernel(cd_ref, q_ref, k_ref, v_ref, g_ref, cos_ref, sin_ref, rd_ref, op_ref, gn_ref, o_ref, s_ref):
    i = pl.program_id(1)

    @pl.when(i == 0)
    def _():
        s_ref[...] = jnp.zeros_like(s_ref)

    hs, q, k, v = _ret_heads(q_ref, k_ref, v_ref, cos_ref, sin_ref)
    s = [s_ref[h] for h in hs]
    qd = [(q[h] * rd_ref[0, h]).astype(BF16) for h in hs]
    kd = [(k[h] * rd_ref[1, h]).astype(BF16) for h in hs]
    o = [op_ref[:, h * RET_DV:(h + 1) * RET_DV] + _bdot(qd[h], s[h]) for h in hs]
    for h in hs:
        s_ref[h] = s[h] * cd_ref[1, h] + _bdot(kd[h], v[h], _TN)
    for h in hs:
        hv = slice(h * RET_DV, (h + 1) * RET_DV)
        mu = jnp.mean(o[h], -1, keepdims=True)
        oc = o[h] - mu
        var = jnp.mean(oc * oc, -1, keepdims=True)
        g = g_ref[:, hv].astype(F32)
        o_ref[:, hv] = (g * _sigmoid(g) * (oc * lax.rsqrt(var + GN_EPS) * gn_ref[:, hv])).astype(o_ref.dtype)


def retention_mix(lay, p, log_gamma, gn_g, cos_t, sin_t):
    c = RET_CHUNK
    n = lay.n_chunks(c)
    n_tok = p.shape[0]
    hk = RET_HEADS * RET_DK
    hv = RET_HEADS * RET_DV

    def specs(backward):
        sc = lambda i: lay.seq_chunk(c, backward, i)
        blk = lambda w, off: pl.BlockSpec((c, w), lambda b, i: (lay.row_block(c, b, sc(i)), off))
        tab = pl.BlockSpec((c, RET_DK), lambda b, i: (sc(i), 0))
        return blk, tab

    pos = jnp.arange(c, dtype=F32)
    lag = pos[:, None] - pos[None, :]
    lg_f, lg_b = log_gamma[0][:, None, None], log_gamma[1][:, None, None]
    k_scale = RET_DK ** -0.5
    dec = (jnp.where(lag >= 0, jnp.exp(jnp.maximum(lag, 0.0) * lg_f), 0.0)
           + jnp.where(lag <= 0, jnp.exp(jnp.maximum(-lag, 0.0) * lg_b), 0.0)) * k_scale
    rows = lambda e, lg: jnp.exp(e[None, :] * lg[:, None])[..., None]
    rd_f = jnp.stack([rows(pos + 1.0, log_gamma[0]), rows(c - 1.0 - pos, log_gamma[0]) * k_scale])
    rd_b = jnp.stack([rows(c - pos, log_gamma[1]), rows(pos, log_gamma[1]) * k_scale])
    cd = jnp.exp(c * log_gamma)

    smem = pl.BlockSpec(memory_space=pltpu.SMEM)
    full = lambda a: pl.BlockSpec(a.shape, lambda b, i: (0,) * a.ndim)
    grid = (lay.batch, n)
    state = [pltpu.VMEM((RET_HEADS, RET_DK, RET_DV), F32)]
    blk, tab = specs(False)
    o_part = pl.pallas_call(
        _ret_fwd_kernel, grid=grid,
        in_specs=[smem, blk(hk, 0), blk(hk, 1), blk(hv, 1), tab, tab, full(dec), full(rd_f)],
        out_specs=blk(hv, 0),
        out_shape=jax.ShapeDtypeStruct((n_tok, hv), F32),
        scratch_shapes=state, compiler_params=_params("arbitrary", "arbitrary"), name="ret_fwd",
    )(cd, p, p, p, cos_t, sin_t, dec, rd_f)
    blk, tab = specs(True)
    return pl.pallas_call(
        _ret_bwd_kernel, grid=grid,
        in_specs=[smem, blk(hk, 0), blk(hk, 1), blk(hv, 1), blk(hv, 2), tab, tab, full(rd_b),
                  blk(hv, 0), pl.BlockSpec((1, hv), lambda b, i: (0, 0))],
        out_specs=blk(hv, 0),
        out_shape=jax.ShapeDtypeStruct((n_tok, hv), BF16),
        scratch_shapes=state, compiler_params=_params("arbitrary", "arbitrary"), name="ret_bwd",
    )(cd, p, p, p, p, cos_t, sin_t, rd_b, o_part, gn_g.reshape(1, hv))


def rope_tables(lay):
    quarter = RET_DK // 4
    t = jnp.arange(lay.seq)
    inv = ROPE_BASE ** (-jnp.arange(quarter, dtype=F32) / quarter)
    ang_r = (t // GRID_W).astype(F32)[:, None] * inv
    ang_c = (t % GRID_W).astype(F32)[:, None] * inv
    cos = jnp.concatenate([jnp.cos(ang_r)] * 2 + [jnp.cos(ang_c)] * 2, -1)
    sin = jnp.concatenate([-jnp.sin(ang_r), jnp.sin(ang_r), -jnp.sin(ang_c), jnp.sin(ang_c)], -1)
    cos = jnp.concatenate([jnp.ones((lay.n_ctx, RET_DK), F32), cos], 0)
    sin = jnp.concatenate([jnp.zeros((lay.n_ctx, RET_DK), F32), sin], 0)
    return cos, sin


def _dn_prep_kernel(lay, tm, prev_ref, x_ref, next_ref, w_ref, o_ref, xe_s):
    j = pl.program_id(0)
    ct = pl.program_id(1)
    r0 = j * tm
    in_ctx = r0 < lay.ctx_tok
    seq_len = jnp.where(in_ctx, lay.n_ctx, lay.seq)
    off = jnp.where(in_ctx, r0, r0 - lay.ctx_tok) % seq_len
    first = off == 0
    last = off + tm == seq_len
    hal = SEQ_HALO
    xe_s[0:hal, :] = jnp.where(first, 0.0, prev_ref[...].astype(F32))
    xe_s[hal:hal + tm, :] = x_ref[...].astype(F32)
    xe_s[hal + tm:, :] = jnp.where(last, 0.0, next_ref[...].astype(F32))
    pad = (DN_CONV - 1) // 2
    acc = xe_s[pl.ds(hal - pad, tm), :] * w_ref[0:1, :]
    for d in range(1, DN_CONV):
        acc = acc + xe_s[pl.ds(hal - pad + d, tm), :] * w_ref[d:d + 1, :]
    y = acc * _sigmoid(acc)
    n_qk_tiles = 2 * DN_QK_W // x_ref.shape[1]

    @pl.when(ct >= n_qk_tiles)
    def _():
        o_ref[...] = y.astype(o_ref.dtype)

    @pl.when(ct < n_qk_tiles)
    def _():
        for s in range(x_ref.shape[1] // DN_HEAD_DIM):
            is_q = ct * x_ref.shape[1] + s * DN_HEAD_DIM < DN_QK_W
            scale = jnp.where(is_q, DN_HEAD_DIM ** -0.5, 1.0)
            ys = y[:, s * DN_HEAD_DIM:(s + 1) * DN_HEAD_DIM]
            inv = lax.rsqrt(jnp.sum(ys * ys, -1, keepdims=True) + L2_EPS) * scale
            o_ref[:, s * DN_HEAD_DIM:(s + 1) * DN_HEAD_DIM] = (ys * inv).astype(o_ref.dtype)


def dn_prep(lay, p, conv_w):
    n_tok = p.shape[0]
    w = 2 * DN_QK_W + DN_V_W
    tm = math.gcd(256, lay.n_ctx, lay.seq)
    tc = 2 * DN_QK_W
    hb = tm // SEQ_HALO
    last_hb = n_tok // SEQ_HALO - 1
    return pl.pallas_call(
        functools.partial(_dn_prep_kernel, lay, tm),
        grid=(n_tok // tm, w // tc),
        in_specs=[pl.BlockSpec((SEQ_HALO, tc), lambda j, c: (jnp.maximum(j * hb - 1, 0), c)),
                  pl.BlockSpec((tm, tc), lambda j, c: (j, c)),
                  pl.BlockSpec((SEQ_HALO, tc), lambda j, c: (jnp.minimum((j + 1) * hb, last_hb), c)),
                  pl.BlockSpec((DN_CONV, tc), lambda j, c: (0, c))],
        out_specs=pl.BlockSpec((tm, tc), lambda j, c: (j, c)),
        out_shape=jax.ShapeDtypeStruct((n_tok, w), BF16),
        scratch_shapes=[pltpu.VMEM((tm + 2 * SEQ_HALO, tc), F32)],
        compiler_params=_params("arbitrary", "arbitrary"), name="dn_prep",
    )(p, p, p, conv_w)


def _dn_scan_kernel(backward, q_ref, k_ref, v_ref, ab_ref, na_row, dt_row, *rest):
    s_ref = rest[-1]

    @pl.when(pl.program_id(1) == 0)
    def _():
        s_ref[...] = jnp.zeros_like(s_ref)

    for j in range(DN_STEP_CHUNKS):
        sub = DN_STEP_CHUNKS - 1 - j if backward else j
        rows = lambda r: r.at[pl.ds(sub * DN_CHUNK, DN_CHUNK)]
        if backward:
            of_ref, z_ref, ng_ref, o_ref, _ = rest
            tail = (rows(of_ref), rows(z_ref), ng_ref, rows(o_ref), s_ref)
        else:
            tail = (rows(rest[0]), s_ref)
        _dn_chunk(backward, rows(q_ref), rows(k_ref), rows(v_ref), rows(ab_ref), na_row, dt_row, *tail)


def _dn_chunk(backward, q_ref, k_ref, v_ref, ab_ref, na_row, dt_row, *rest):
    if backward:
        of_ref, z_ref, ng_ref, o_ref, s_ref = rest
    else:
        o_ref, s_ref = rest
    c = q_ref.shape[0]
    z = 1 if backward else 0
    nh = DN_V_HEADS
    hd = DN_HEAD_DIM
    rep = DN_V_HEADS // DN_QK_HEADS

    row = lax.broadcasted_iota(jnp.int32, (c, c), 0)
    col = lax.broadcasted_iota(jnp.int32, (c, c), 1)
    lag = (col - row) if backward else (row - col)
    incl = lag >= 0
    strict = lag > 0
    eye = (row == col).astype(F32)
    ab = ab_ref[...]
    g_cols = na_row[...] * _softplus(ab + dt_row[...])
    beta_cols = _sigmoid(ab)
    gc_cols = _hdot(incl.astype(F32), g_cols)
    gc_rows = lax.dot_general(gc_cols, eye, _TN, precision=lax.Precision.HIGHEST,
                              preferred_element_type=F32)
    gend = jnp.sum(g_cols, axis=0, keepdims=True)
    heads = range(nh)
    gi = [z * 2 * nh + h for h in heads]
    bi = [z * 2 * nh + nh + h for h in heads]
    gcc = [gc_cols[:, gi[h]:gi[h] + 1] for h in heads]
    dec = [jnp.where(incl, jnp.exp(jnp.minimum(gcc[h] - gc_rows[gi[h]:gi[h] + 1, :], 0.0)), 0.0) for h in heads]
    beta = [beta_cols[:, bi[h]:bi[h] + 1] for h in heads]
    egc = [jnp.exp(gcc[h]) for h in heads]
    eend = [jnp.exp(gend[:, gi[h]:gi[h] + 1] - gcc[h]) for h in heads]
    tail = [jnp.exp(gend[:, gi[h]:gi[h] + 1]) for h in heads]
    qs = [q_ref[:, j * hd:(j + 1) * hd] for j in range(DN_QK_HEADS)]
    ks = [k_ref[:, j * hd:(j + 1) * hd] for j in range(DN_QK_HEADS)]
    vs = [v_ref[:, h * hd:(h + 1) * hd].astype(F32) for h in heads]
    kq = [_bdot(jnp.concatenate([ks[j], qs[j]], axis=0), ks[j], _NT) for j in range(DN_QK_HEADS)]
    neg_a = [jnp.where(strict, -(kq[h // rep][:c] * beta[h] * dec[h]), 0.0) for h in heads]
    qkd = [(kq[h // rep][c:] * dec[h]).astype(BF16) for h in heads]
    tm = _unit_tri_inverse(neg_a, eye, heads)
    kf = [ks[h // rep].astype(F32) for h in heads]
    rhs = [jnp.concatenate([vs[h] * beta[h], kf[h] * (beta[h] * egc[h])], axis=1) for h in heads]
    uw = [_bdot(tm[h], rhs[h]) for h in heads]
    s0 = [s_ref[h] for h in heads]
    lhs = [jnp.concatenate([uw[h][:, hd:], qs[h // rep].astype(F32) * egc[h]], axis=0) for h in heads]
    ws_qs = [_bdot(lhs[h], s0[h]) for h in heads]
    v_new = [(uw[h][:, :hd] - ws_qs[h][:c]).astype(BF16) for h in heads]
    o = [ws_qs[h][c:] + _bdot(qkd[h], v_new[h]) for h in heads]
    for h in heads:
        s_ref[h] = s0[h] * tail[h] + _bdot(kf[h] * eend[h], v_new[h], _TN)
    if not backward:
        for h in heads:
            o_ref[:, h * hd:(h + 1) * hd] = o[h]
    else:
        for h in heads:
            ot = o[h] + of_ref[:, h * hd:(h + 1) * hd]
            on = ot * lax.rsqrt(jnp.mean(ot * ot, -1, keepdims=True) + RMS_EPS) * ng_ref[...]
            zz = z_ref[:, h * hd:(h + 1) * hd].astype(F32)
            o_ref[:, h * hd:(h + 1) * hd] = (on * (zz * _sigmoid(zz))).astype(o_ref.dtype)


def deltanet_mix(lay, qkv, p, ab, a_log, dt_bias, norm_g):
    c = DN_CHUNK * DN_STEP_CHUNKS
    n = lay.n_chunks(c)
    n_tok = qkv.shape[0]
    nh = DN_V_HEADS
    neg_a = -jnp.exp(a_log.astype(F32))
    na = jnp.concatenate([neg_a, jnp.zeros_like(neg_a)], axis=1).reshape(1, 4 * nh)
    dt = jnp.concatenate([dt_bias.astype(F32), jnp.zeros_like(neg_a)], axis=1).reshape(1, 4 * nh)
    small = lambda a: pl.BlockSpec(a.shape, lambda b, i: (0, 0))
    consts = (na, dt)
    state = [pltpu.VMEM((nh, DN_HEAD_DIM, DN_HEAD_DIM), F32)]

    def specs(backward):
        rb = lambda b, i: lay.row_block(c, b, lay.seq_chunk(c, backward, i))
        return lambda w, off: pl.BlockSpec((c, w), lambda b, i: (rb(b, i), off))

    common = lambda blk: [blk(DN_QK_W, 0), blk(DN_QK_W, 1), blk(DN_V_W, 1), blk(4 * nh, 0)] + [small(a) for a in consts]
    blk = specs(False)
    o_f = pl.pallas_call(
        functools.partial(_dn_scan_kernel, False), grid=(lay.batch, n),
        in_specs=common(blk), out_specs=blk(DN_V_W, 0),
        out_shape=jax.ShapeDtypeStruct((n_tok, DN_V_W), F32),
        scratch_shapes=state, compiler_params=_params("arbitrary", "arbitrary"), name="dn_scan_fwd",
    )(qkv, qkv, qkv, ab, *consts)
    blk = specs(True)
    return pl.pallas_call(
        functools.partial(_dn_scan_kernel, True), grid=(lay.batch, n),
        in_specs=common(blk) + [blk(DN_V_W, 0), blk(DN_V_W, 2), small(norm_g.reshape(1, DN_HEAD_DIM))],
        out_specs=blk(DN_V_W, 0),
        out_shape=jax.ShapeDtypeStruct((n_tok, DN_V_W), BF16),
        scratch_shapes=state, compiler_params=_params("arbitrary", "arbitrary"), name="dn_scan_bwd",
    )(qkv, qkv, qkv, ab, *consts, o_f, p, norm_g.reshape(1, DN_HEAD_DIM))


def _unit_tri_inverse(nm, eye, heads):
    c = eye.shape[0]
    tm = [eye + nm[h] for h in heads]
    p = [_bdot(nm[h], nm[h]).astype(BF16) for h in heads]
    for _ in range(int(math.log2(c)) - 2):
        pt = [_bdot(jnp.concatenate([p[h], tm[h].astype(BF16)], axis=0), p[h]) for h in heads]
        tm = [tm[h] + pt[h][c:] for h in heads]
        p = [pt[h][:c].astype(BF16) for h in heads]
    return [tm[h] + _bdot(tm[h], p[h]) for h in heads]


def _rwkv_chunk_kernel(r_ref, v_ref, kk_ref, wl_ref, kd_ref, a_ref, o_ref, s_ref, *scratch):
    z = pl.program_id(0)

    @pl.when(pl.program_id(2) == 0)
    def _():
        s_ref[...] = jnp.zeros_like(s_ref)

    for j in range(RWKV_STEP_CHUNKS):
        sub = j + z * (RWKV_STEP_CHUNKS - 1 - 2 * j)
        rows = lambda r: r.at[pl.ds(pl.multiple_of(sub * RWKV_CHUNK, RWKV_CHUNK), RWKV_CHUNK)]
        _rwkv_chunk(z, rows(r_ref), rows(v_ref), rows(kk_ref), rows(wl_ref), rows(kd_ref), rows(a_ref),
                    rows(o_ref), s_ref, *scratch)


def _rwkv_chunk(z, r_ref, v_ref, kk_ref, wl_ref, kd_ref, a_ref, o_ref, s_ref, ar_s, bt_s, bk_s, uv_s, gc_s):
    c = r_ref.shape[0]
    row = lax.broadcasted_iota(jnp.int32, (c, c), 0)
    col = lax.broadcasted_iota(jnp.int32, (c, c), 1)
    lag = (row - col) * (1 - 2 * z)
    incl = lag >= 0
    strict = lag > 0

    wl = wl_ref[...]
    logw = -jnp.exp(-_softplus(-wl) - 0.5)
    cum = _hdot(incl.astype(F32), logw)
    c_last = jnp.sum(logw, axis=0, keepdims=True)
    kk = kk_ref[...].astype(F32)
    kb = kk * a_ref[...].astype(F32)
    kd = kd_ref[...].astype(F32)
    g_inv = jnp.exp(-cum)
    e_end = jnp.exp(c_last - cum)
    ar_s[0:c, :] = (-kk * jnp.exp(cum - logw)).astype(BF16)
    ar_s[c:, :] = (r_ref[...].astype(F32) * jnp.exp(cum)).astype(BF16)
    bt_s[0:c, :] = (kb * g_inv).astype(BF16)
    bt_s[c:, :] = (kd * g_inv).astype(BF16)
    bk_s[0:c, :] = (kb * e_end).astype(BF16)
    bk_s[c:, :] = (kd * e_end).astype(BF16)
    uv_s[c:, :] = v_ref[...].astype(BF16)
    gc_s[...] = jnp.exp(c_last)

    eye = (row == col).astype(F32)
    hh = range(RWKV_HEADS)
    sl = [slice(h * RWKV_HEAD, (h + 1) * RWKV_HEAD) for h in hh]
    lag2 = jnp.concatenate([lag, lag + 1], axis=0)
    mask4 = jnp.concatenate([lag2, lag2], axis=1) > 0
    x4 = [jnp.where(mask4, _bdot(ar_s[:, sl[h]], bt_s[:, sl[h]], _NT), 0.0) for h in hh]
    xb = [x4[h][:, :c] for h in hh]
    xk = [x4[h][:, c:].astype(BF16) for h in hh]
    tm = _unit_tri_inverse([xb[h][:c] for h in hh], eye, hh)
    s0 = [s_ref[h] for h in hh]
    xs = [_bdot(ar_s[:, sl[h]], s0[h], _NT) + _bdot(xk[h], uv_s[c:, sl[h]]) for h in hh]
    u = [_bdot(tm[h], xs[h][:c]).astype(BF16) for h in hh]
    for h in hh:
        uv_s[0:c, sl[h]] = u[h]
        o_ref[:, sl[h]] = xs[h][c:] + _bdot(xb[h][c:], u[h])
    for h in hh:
        s_ref[h] = s0[h] * gc_s[:, sl[h]] + _bdot(uv_s[:, sl[h]], bk_s[:, sl[h]], _TN)


def rwkv_scan(lay, r, v, kk, wl, kd, a):
    n_tok, d = r.shape
    step = RWKV_CHUNK * RWKV_STEP_CHUNKS
    n = lay.n_chunks(step)

    def rb(z, b, i):
        sc = jnp.where(z == 0, lay.seq_chunk(step, False, i), lay.seq_chunk(step, True, i))
        return lay.row_block(step, b, sc)

    shared = pl.BlockSpec((step, d), lambda z, b, i: (rb(z, b, i), 0))
    perdir = pl.BlockSpec((None, step, d), lambda z, b, i: (z, rb(z, b, i), 0))
    c = RWKV_CHUNK
    bf = lambda rows: pltpu.VMEM((rows, d), BF16)
    return pl.pallas_call(
        _rwkv_chunk_kernel,
        grid=(2, lay.batch, n),
        in_specs=[shared, shared, shared, perdir, perdir, perdir],
        out_specs=perdir,
        out_shape=jax.ShapeDtypeStruct((2, n_tok, d), F32),
        scratch_shapes=[pltpu.VMEM((RWKV_HEADS, RWKV_HEAD, RWKV_HEAD), F32),
                        bf(2 * c), bf(2 * c), bf(2 * c), bf(2 * c), pltpu.VMEM((1, d), F32)],
        compiler_params=_params("arbitrary", "arbitrary", "arbitrary"),
        name="rwkv_scan",
    )(r, v, kk, wl, kd, a)


def _rwkv_pre_kernel(lay, tm, prev_ref, h_ref, next_ref, mix_ref, wr_ref, wk_ref, wv_ref, w1_ref, a1_ref, g1_ref,
                     w2_ref, a2_ref, g2_ref, w0_ref, a0_ref, kk_ref, ka_ref, rk_ref, seg_ref,
                     r_out, v_out, kk_out, wl_out, kd_out, a_out, g_out, bonus_out, xe_s):
    j = pl.program_id(0)
    r0 = j * tm
    in_ctx = r0 < lay.ctx_tok
    seq_len = jnp.where(in_ctx, lay.n_ctx, lay.seq)
    off = jnp.where(in_ctx, r0, r0 - lay.ctx_tok) % seq_len
    hal = SEQ_HALO
    xe_s[0:hal, :] = jnp.where(off == 0, 0.0, prev_ref[...].astype(F32))
    xe_s[hal:hal + tm, :] = h_ref[...].astype(F32)
    xe_s[hal + tm:, :] = jnp.where(off + tm == seq_len, 0.0, next_ref[...].astype(F32))
    h = xe_s[pl.ds(hal, tm), :]
    xx = 0.5 * (xe_s[pl.ds(hal - 1, tm), :] + xe_s[pl.ds(hal + 1, tm), :]) - h
    xm = lambda i: (h + xx * mix_ref[i:i + 1, :]).astype(BF16)
    r = _bdot(xm(0), wr_ref[...])
    k = _bdot(xm(1), wk_ref[...])
    v = _bdot(xm(2), wv_ref[...])
    hw = jnp.tanh(_bdot(xm(3), w1_ref[...]))
    ha = _bdot(xm(4), a1_ref[...])
    hg = _sigmoid(_bdot(xm(5), g1_ref[...]))
    seg = seg_ref[...]
    kx = k * kk_ref[...]
    r_out[...] = r.astype(r_out.dtype)
    v_out[...] = v.astype(v_out.dtype)
    kk_out[...] = (kx * lax.rsqrt(_bdot(kx * kx, seg) + L2_EPS)).astype(kk_out.dtype)
    g_out[...] = _bdot(hg, g2_ref[...]).astype(g_out.dtype)
    lw = w2_ref.shape[1]
    rr = r * rk_ref[...]
    bsum = None
    for z in range(2):
        wl_out[z] = w0_ref[z:z + 1, :] + _bdot(hw[:, z * lw:(z + 1) * lw], w2_ref[z])
        a = _sigmoid(a0_ref[z:z + 1, :] + _bdot(ha[:, z * lw:(z + 1) * lw], a2_ref[z]))
        kd = k * (1.0 + (a - 1.0) * ka_ref[...])
        a_out[z] = a.astype(a_out.dtype)
        kd_out[z] = kd.astype(kd_out.dtype)
        bsum = rr * kd if bsum is None else bsum + rr * kd
    bonus_out[...] = (_bdot(bsum, seg) * v).astype(bonus_out.dtype)


def _rwkv_post_kernel(o_ref, g_ref, bonus_ref, lnx_ref, seg_ref, a_out):
    seg = seg_ref[...]
    o = o_ref[0] + o_ref[1]
    hi = o.astype(BF16)
    lo = o - hi.astype(F32)
    inv_n = 1.0 / RWKV_HEAD
    oc = o - (_bdot(hi, seg) + _bdot(lo, seg)) * inv_n
    var = _bdot(oc * oc, seg) * inv_n
    on = oc * lax.rsqrt(var + LNX_EPS) * lnx_ref[...] + bonus_ref[...].astype(F32)
    a_out[...] = (on * g_ref[...].astype(F32)).astype(a_out.dtype)


def rwkv7_mix(lay, h, mix, w_rkv, w0, w1, w2, a0, a1, a2, g1, g2, k_k, k_a, r_k, lnx_g):
    n_tok, d = h.shape
    bw = lambda w: w.astype(BF16)
    tm = math.gcd(256, lay.n_ctx, lay.seq)
    hb = tm // SEQ_HALO
    last_hb = n_tok // SEQ_HALO - 1
    head_of = jnp.arange(d) // RWKV_HEAD
    seg = (head_of[:, None] == head_of[None, :]).astype(BF16)
    full = lambda a: pl.BlockSpec(a.shape, lambda j: (0,) * a.ndim)
    row = lambda a: a.reshape(1, d)
    consts = (mix, bw(w_rkv[0]), bw(w_rkv[1]), bw(w_rkv[2]), bw(jnp.concatenate([w1[0], w1[1]], -1)),
              bw(jnp.concatenate([a1[0], a1[1]], -1)), bw(g1), bw(w2), bw(a2), bw(g2), w0, a0,
              row(k_k), row(k_a), row(r_k), seg)
    tok = pl.BlockSpec((tm, d), lambda j: (j, 0))
    tok2 = pl.BlockSpec((2, tm, d), lambda j: (0, j, 0))
    one = lambda dt: jax.ShapeDtypeStruct((n_tok, d), dt)
    two = lambda dt: jax.ShapeDtypeStruct((2, n_tok, d), dt)
    r, v, kk, wl, kd, a, g, bonus = pl.pallas_call(
        functools.partial(_rwkv_pre_kernel, lay, tm),
        grid=(n_tok // tm,),
        in_specs=[pl.BlockSpec((SEQ_HALO, d), lambda j: (jnp.maximum(j * hb - 1, 0), 0)), tok,
                  pl.BlockSpec((SEQ_HALO, d), lambda j: (jnp.minimum((j + 1) * hb, last_hb), 0))]
                 + [full(a) for a in consts],
        out_specs=[tok, tok, tok, tok2, tok2, tok2, tok, tok],
        out_shape=[one(BF16), one(BF16), one(BF16), two(F32), two(BF16), two(BF16), one(BF16), one(BF16)],
        scratch_shapes=[pltpu.VMEM((tm + 2 * SEQ_HALO, d), F32)],
        compiler_params=_params("arbitrary"), name="rwkv_pre",
    )(h, h, h, *consts)
    o = rwkv_scan(lay, r, v, kk, wl, kd, a)
    return pl.pallas_call(
        _rwkv_post_kernel,
        grid=(n_tok // tm,),
        in_specs=[tok2, tok, tok, full(row(lnx_g)), full(seg)],
        out_specs=tok,
        out_shape=one(BF16),
        compiler_params=_params("arbitrary"), name="rwkv_post",
    )(o, g, bonus, row(lnx_g), seg)


def _moe_kernel(be_ref, nb_ref, x_ref, wg_ref, wu_ref, wd_ref, gate_ref, o_ref, acc_s):
    j = pl.program_id(0)
    f = pl.program_id(1)

    @pl.when(f == 0)
    def _():
        acc_s[...] = jnp.zeros_like(acc_s)

    @pl.when(j < nb_ref[0])
    def _():
        x = x_ref[...]
        g = _bdot(x, wg_ref[...])
        u = _bdot(x, wu_ref[...])
        acc_s[...] += _bdot(g * _sigmoid(g) * u, wd_ref[...])

    @pl.when(f == pl.num_programs(1) - 1)
    def _():
        o_ref[...] = (acc_s[...] * gate_ref[...]).astype(o_ref.dtype)


def moe_experts(xb, block_e, n_used, w_gu, w_down, layer, slot_gate):
    n_slots, d = xb.shape
    bm = MOE_ROWS
    nf = MOE_F_CHUNKS
    tf = w_down.shape[2] // nf
    return pl.pallas_call(
        _moe_kernel,
        grid_spec=pltpu.PrefetchScalarGridSpec(
            num_scalar_prefetch=2,
            grid=(n_slots // bm, nf),
            in_specs=[pl.BlockSpec((bm, d), lambda j, f, be, nb: (j, 0)),
                      pl.BlockSpec((None, None, d, tf), lambda j, f, be, nb: (layer, be[j], 0, f)),
                      pl.BlockSpec((None, None, d, tf), lambda j, f, be, nb: (layer, be[j], 0, f + nf)),
                      pl.BlockSpec((None, None, tf, d), lambda j, f, be, nb: (layer, be[j], f, 0)),
                      pl.BlockSpec((bm, 1), lambda j, f, be, nb: (j, 0))],
            out_specs=pl.BlockSpec((bm, d), lambda j, f, be, nb: (j, 0)),
            scratch_shapes=[pltpu.VMEM((bm, d), F32)]),
        out_shape=jax.ShapeDtypeStruct((n_slots, d), BF16),
        compiler_params=_params("arbitrary", "arbitrary"),
        name="moe_experts",
    )(block_e, n_used, xb, w_gu, w_gu, w_down, slot_gate)


def moe_swiglu(h, w_router, w_gu, w_down, layer):
    n, d = h.shape
    logits = proj(h, w_router.astype(BF16), F32, N_EXPERTS)
    lanes = jnp.arange(N_EXPERTS, dtype=jnp.int32)
    e1 = jnp.argmax(logits, axis=-1).astype(jnp.int32)
    rest = jnp.where(lanes == e1[:, None], -jnp.inf, logits)
    e2 = jnp.argmax(rest, axis=-1).astype(jnp.int32)
    top_logit = jnp.stack([jnp.max(logits, axis=-1), jnp.max(rest, axis=-1)], axis=-1)
    top_e = jnp.stack([e1, e2], axis=-1)
    gate = jax.nn.softmax(top_logit, axis=-1)
    flat_e = top_e.reshape(-1).astype(jnp.int32)
    order = jnp.argsort(flat_e).astype(jnp.int32)
    onehot = (flat_e[:, None] == jnp.arange(N_EXPERTS, dtype=jnp.int32)).astype(jnp.int32)
    seen = jnp.cumsum(onehot, axis=0)
    counts = seen[-1]
    rank = jnp.sum(seen * onehot, axis=1) - 1
    padded = (counts + MOE_ROWS - 1) // MOE_ROWS * MOE_ROWS
    start = jnp.cumsum(counts) - counts
    pend = jnp.cumsum(padded)
    pstart = pend - padded
    n_slots = (n * TOP_K + MOE_ROWS - 1) // MOE_ROWS * MOE_ROWS + N_EXPERTS * MOE_ROWS
    n_blocks = n_slots // MOE_ROWS
    block_e = jnp.minimum(jnp.sum(jnp.arange(n_blocks)[:, None] * MOE_ROWS >= pend[None, :], axis=1),
                          N_EXPERTS - 1).astype(jnp.int32)
    n_used = (pend[-1] // MOE_ROWS).astype(jnp.int32).reshape(1)
    slot_e = jnp.repeat(block_e, MOE_ROWS)
    slot_off = jnp.arange(n_slots, dtype=jnp.int32) - pstart[slot_e]
    slot_valid = slot_off < counts[slot_e]
    slot_asg = order[jnp.clip(start[slot_e] + slot_off, 0, n * TOP_K - 1)]
    slot_tok = jnp.where(slot_valid, slot_asg // TOP_K, 0)
    slot_gate = jnp.where(slot_valid, gate.reshape(-1)[slot_asg], 0.0)
    yb = moe_experts(h[slot_tok], block_e, n_used, w_gu, w_down, layer, slot_gate[:, None])
    tok_slot = (pstart[flat_e] + rank).reshape(n, TOP_K)
    return yb[tok_slot[:, 0]], yb[tok_slot[:, 1]]


def kernel(x, c, ctx, c_ctx, mod_w, mod_b, ln_g, ln_b,
           ret_w_in, ret_decay, ret_gn_g, ret_w_out,
           dn_w_in, dn_conv_w, dn_a_log, dn_dt_bias, dn_norm_g, dn_w_out,
           rk_mix, rk_w_rkv, rk_w0, rk_w1, rk_w2, rk_a0, rk_a1, rk_a2, rk_g1, rk_g2,
           rk_k_k, rk_k_a, rk_r_k, rk_lnx_g, rk_w_out,
           ffn_w_gu, ffn_w_down, moe_router, moe_w_gu, moe_w_down):
    bsz, t, d = x.shape
    n_ctx = ctx.shape[1]
    lay = Layout(bsz, n_ctx, t)
    bw = lambda w: w.astype(BF16)
    s_rows = jax.nn.silu(jnp.concatenate([c_ctx[None], c], 0))
    s_pad = jnp.zeros((8, d), F32).at[:1 + bsz].set(s_rows)
    mods = modulation_rows(s_pad, mod_w, mod_b)[:, :1 + bsz].reshape(DEPTH, 1 + bsz, 6, 1, d)
    mod = lambda i, k: mods[i, :, k]
    cos_t, sin_t = rope_tables(lay)
    xs = jnp.concatenate([ctx.reshape(-1, d), x.reshape(-1, d)], 0)
    h = modulate(lay, xs, mod(0, 1), mod(0, 0))
    for i in range(DEPTH):
        last = i == DEPTH - 1
        kind, j = i % N_MIXERS, i // N_MIXERS
        if kind == 0:
            p = proj(h, bw(ret_w_in[j]), BF16, PROJ_COLS)
            log_gamma = jax.nn.log_sigmoid(ret_decay[j].astype(F32))
            a = retention_mix(lay, p, log_gamma, ret_gn_g[j], cos_t, sin_t)
            w_out = ret_w_out[j]
        elif kind == 1:
            n_main = 2 * DN_QK_W + 2 * DN_V_W
            p = proj(h, bw(dn_w_in[j][:, :n_main]), BF16, PROJ_COLS)
            w_ab = bw(dn_w_in[j][:, n_main:])
            ab = proj(h, w_ab, F32, w_ab.shape[1])
            qkv = dn_prep(lay, p, dn_conv_w[j])
            a = deltanet_mix(lay, qkv, p, ab, dn_a_log[j], dn_dt_bias[j], dn_norm_g[j])
            w_out = dn_w_out[j]
        else:
            a = rwkv7_mix(lay, h, rk_mix[j], rk_w_rkv[j], rk_w0[j], rk_w1[j], rk_w2[j], rk_a0[j], rk_a1[j],
                          rk_a2[j], rk_g1[j], rk_g2[j], rk_k_k[j], rk_k_a[j], rk_r_k[j], rk_lnx_g[j])
            w_out = rk_w_out[j]
        xs, h = out_ln(lay, a, bw(w_out), xs, mod(i, 2), ln_g[i, 0], ln_b[i, 0], mod(i, 4), mod(i, 3))
        nxt = (i + 1) % DEPTH
        if i % 2 == 0:
            hm = swiglu_in(h, bw(ffn_w_gu[i // 2]), FFN_COLS)
            xs, h = out_ln(lay, hm, bw(ffn_w_down[i // 2]), xs, mod(i, 5), ln_g[i, 1], ln_b[i, 1],
                           mod(nxt, 1), mod(nxt, 0))
        else:
            f = moe_swiglu(h, moe_router[i // 2], moe_w_gu, moe_w_down, i // 2)
            xs, h = out_ln(lay, f, None, xs, mod(i, 5), ln_g[i, 1], ln_b[i, 1], mod(nxt, 1), mod(nxt, 0))
    return xs[lay.ctx_tok:].reshape(bsz, t, d)
```

```python
import math, functools
import jax
import jax.numpy as jnp
from jax import lax
import numpy as np
from jax.experimental import pallas as pl
from jax.experimental.pallas import tpu as pltpu

D_MODEL = 1024
DEPTH = 4
GRID_W = 64
N_MIXERS = 3
ALPHA = (2 * DEPTH) ** 0.25
LN_EPS = 1e-5
GN_EPS = 1e-5
RMS_EPS = 1e-6
LNX_EPS = 64e-5
L2_EPS = 1e-6

RET_HEADS = 4
RET_DK = D_MODEL // RET_HEADS
RET_DV = 2 * RET_DK
RET_CHUNK = 256
ROPE_BASE = 10000.0

DN_QK_HEADS = 8
DN_V_HEADS = 16
DN_HEAD_DIM = 128
DN_CHUNK = 64
DN_STEP_CHUNKS = 4
DN_CONV = 5
DN_QK_W = DN_QK_HEADS * DN_HEAD_DIM
DN_V_W = DN_V_HEADS * DN_HEAD_DIM
SEQ_HALO = 16

RWKV_HEAD = 64
RWKV_HEADS = D_MODEL // RWKV_HEAD
RWKV_CHUNK = 64
RWKV_STEP_CHUNKS = 4

FFN_DIM = 2816
N_EXPERTS = 8
TOP_K = 2
EXPERT_DIM = 3584
MOE_ROWS = 1024
MOE_F_CHUNKS = 7
PROJ_COLS = 2048
FFN_COLS = FFN_DIM // 2

ROW_TILE = 512
VMEM_LIMIT = 48 * 1024 * 1024

BF16 = jnp.bfloat16
F32 = jnp.float32
_NT = (((1,), (1,)), ((), ()))
_TN = (((0,), (0,)), ((), ()))


def _bdot(x, y, dims=None):
    x = x.astype(BF16)
    y = y.astype(BF16)
    if dims is None:
        return jnp.dot(x, y, preferred_element_type=F32)
    return lax.dot_general(x, y, dims, preferred_element_type=F32)


def _hdot(x, y):
    return jnp.dot(x, y, precision=lax.Precision.HIGHEST, preferred_element_type=F32)


def _sigmoid(x):
    return 1.0 / (1.0 + jnp.exp(-x))


def _softplus(x):
    return jnp.maximum(x, 0.0) + jnp.log(1.0 + jnp.exp(-jnp.abs(x)))


def _params(*sem):
    return pltpu.CompilerParams(dimension_semantics=sem, vmem_limit_bytes=VMEM_LIMIT)


class Layout:
    def __init__(self, batch, n_ctx, seq):
        self.batch, self.n_ctx, self.seq = batch, n_ctx, seq
        self.ctx_tok = batch * n_ctx
        self.n_tok = self.ctx_tok + batch * seq
        self.row_tile = math.gcd(ROW_TILE, n_ctx * batch, seq)

    def mod_index(self, tile, j):
        r0 = j * tile
        return jnp.where(r0 < self.ctx_tok, 0, 1 + (r0 - self.ctx_tok) // self.seq)

    def seq_chunk(self, chunk, backward, i):
        nc, nl = self.n_ctx // chunk, self.seq // chunk
        if not backward:
            return i
        return jnp.where(i < nc, nc - 1 - i, 2 * nc + nl - 1 - i)

    def row_block(self, chunk, b, sc):
        nc, nl = self.n_ctx // chunk, self.seq // chunk
        return jnp.where(sc < nc, b * nc + sc, self.batch * nc + b * nl + sc - nc)

    def n_chunks(self, chunk):
        return (self.n_ctx + self.seq) // chunk


def _proj_kernel(h_ref, w_ref, o_ref):
    o_ref[...] = _bdot(h_ref[...], w_ref[...]).astype(o_ref.dtype)


def proj(h, w, out_dtype, tn):
    n_tok, k = h.shape
    n = w.shape[1]
    tm = math.gcd(1024, n_tok)
    return pl.pallas_call(
        _proj_kernel,
        grid=(n // tn, n_tok // tm),
        in_specs=[pl.BlockSpec((tm, k), lambda c, j: (j, 0)), pl.BlockSpec((k, tn), lambda c, j: (0, c))],
        out_specs=pl.BlockSpec((tm, tn), lambda c, j: (j, c)),
        out_shape=jax.ShapeDtypeStruct((n_tok, n), out_dtype),
        compiler_params=_params("arbitrary", "arbitrary"),
        name="proj",
    )(h, w)


def _swiglu_in_kernel(h_ref, wg_ref, wu_ref, o_ref):
    h = h_ref[...]
    g = _bdot(h, wg_ref[...])
    u = _bdot(h, wu_ref[...])
    o_ref[...] = (g * _sigmoid(g) * u).astype(o_ref.dtype)


def swiglu_in(h, w_gu, tn):
    n_tok, k = h.shape
    f = w_gu.shape[1] // 2
    tm = math.gcd(1024, n_tok)
    nf = f // tn
    return pl.pallas_call(
        _swiglu_in_kernel,
        grid=(nf, n_tok // tm),
        in_specs=[pl.BlockSpec((tm, k), lambda c, j: (j, 0)),
                  pl.BlockSpec((k, tn), lambda c, j: (0, c)),
                  pl.BlockSpec((k, tn), lambda c, j: (0, c + nf))],
        out_specs=pl.BlockSpec((tm, tn), lambda c, j: (j, c)),
        out_shape=jax.ShapeDtypeStruct((n_tok, f), BF16),
        compiler_params=_params("arbitrary", "arbitrary"),
        name="swiglu_in",
    )(h, w_gu, w_gu)


def _deepnorm_epilogue(x, f, ga_ref, g_ref, b_ref, sc_ref, sh_ref, x_out, h_out):
    y = ALPHA * x + (1.0 + ga_ref[...]) * f
    mu = jnp.mean(y, -1, keepdims=True)
    yc = y - mu
    var = jnp.mean(yc * yc, -1, keepdims=True)
    xn = yc * lax.rsqrt(var + LN_EPS) * g_ref[...] + b_ref[...]
    x_out[...] = xn
    h_out[...] = (xn * (1.0 + sc_ref[...]) + sh_ref[...]).astype(h_out.dtype)


def _out_ln_kernel(a_ref, w_ref, x_ref, ga_ref, g_ref, b_ref, sc_ref, sh_ref, x_out, h_out):
    _deepnorm_epilogue(x_ref[...], _bdot(a_ref[...], w_ref[...]), ga_ref, g_ref, b_ref, sc_ref, sh_ref, x_out, h_out)


def _resid_ln_kernel(f1_ref, f2_ref, x_ref, ga_ref, g_ref, b_ref, sc_ref, sh_ref, x_out, h_out):
    f = f1_ref[...].astype(F32) + f2_ref[...].astype(F32)
    _deepnorm_epilogue(x_ref[...], f, ga_ref, g_ref, b_ref, sc_ref, sh_ref, x_out, h_out)


def out_ln(lay, a, w, x, gate, ln_g, ln_b, sc_next, sh_next, latents_only=False):
    n_tok, d = x.shape
    tm = lay.row_tile
    row = lambda j: (j, 0)
    ctx_tiles = lay.ctx_tok // tm if latents_only else 0
    x_out = pl.BlockSpec((tm, d), lambda j: (jnp.maximum(j - ctx_tiles, 0), 0))
    mod = pl.BlockSpec((None, 1, d), lambda j: (lay.mod_index(tm, j), 0, 0))
    vec = pl.BlockSpec((1, d), lambda j: (0, 0))
    tok = pl.BlockSpec((tm, d), row)
    if w is None:
        body, lhs, lhs_specs = _resid_ln_kernel, tuple(a), [tok, tok]
    else:
        k = a.shape[1]
        body, lhs = _out_ln_kernel, (a, w)
        lhs_specs = [pl.BlockSpec((tm, k), row), pl.BlockSpec((k, d), lambda j: (0, 0))]
    return pl.pallas_call(
        body,
        grid=(n_tok // tm,),
        in_specs=lhs_specs + [tok, mod, vec, vec, mod, mod],
        out_specs=[x_out, tok],
        out_shape=[jax.ShapeDtypeStruct((n_tok - ctx_tiles * tm, d), F32), jax.ShapeDtypeStruct((n_tok, d), BF16)],
        compiler_params=_params("arbitrary"),
        name="out_ln",
    )(*lhs, x, gate, ln_g.reshape(1, d), ln_b.reshape(1, d), sc_next, sh_next)


def _modulate_kernel(x_ref, sc_ref, sh_ref, h_out):
    h_out[...] = (x_ref[...] * (1.0 + sc_ref[...]) + sh_ref[...]).astype(h_out.dtype)


def modulate(lay, x, sc, sh):
    n_tok, d = x.shape
    tm = lay.row_tile
    mod = pl.BlockSpec((None, 1, d), lambda j: (lay.mod_index(tm, j), 0, 0))
    tok = pl.BlockSpec((tm, d), lambda j: (j, 0))
    return pl.pallas_call(
        _modulate_kernel, grid=(n_tok // tm,), in_specs=[tok, mod, mod], out_specs=tok,
        out_shape=jax.ShapeDtypeStruct((n_tok, d), BF16), compiler_params=_params("arbitrary"), name="modulate",
    )(x, sc, sh)


def _mod_kernel(s_ref, w_ref, b_ref, o_ref):
    o_ref[...] = _bdot(s_ref[...], w_ref[...]) + b_ref[...]


def modulation_rows(s, mod_w, mod_b):
    r, d = s.shape
    depth, _, n = mod_w.shape
    tn = 1024
    return pl.pallas_call(
        _mod_kernel,
        grid=(depth, n // tn),
        in_specs=[pl.BlockSpec((r, d), lambda i, c: (0, 0)),
                  pl.BlockSpec((None, d, tn), lambda i, c: (i, 0, c)),
                  pl.BlockSpec((None, 1, tn), lambda i, c: (i, 0, c))],
        out_specs=pl.BlockSpec((None, r, tn), lambda i, c: (i, 0, c)),
        out_shape=jax.ShapeDtypeStruct((depth, r, n), F32),
        compiler_params=_params("arbitrary", "arbitrary"),
        name="modulation_rows",
    )(s, mod_w, mod_b.reshape(depth, 1, n))


def _rope(x, cos, sin):
    half = x.shape[1] // 2
    parts = []
    for p in range(2):
        xs = x[:, p * half:(p + 1) * half]
        parts.append(xs * cos[:, p * half:(p + 1) * half]
                     + pltpu.roll(xs, half // 2, axis=1) * sin[:, p * half:(p + 1) * half])
    return jnp.concatenate(parts, axis=1)


def _ret_heads(q_ref, k_ref, v_ref, cos_ref, sin_ref):
    cos, sin = cos_ref[...], sin_ref[...]
    hs = range(RET_HEADS)
    q = [_rope(q_ref[:, h * RET_DK:(h + 1) * RET_DK].astype(F32), cos, sin) for h in hs]
    k = [_rope(k_ref[:, h * RET_DK:(h + 1) * RET_DK].astype(F32), cos, sin) for h in hs]
    v = [v_ref[:, h * RET_DV:(h + 1) * RET_DV] for h in hs]
    return hs, q, k, v


def _ret_fwd_kernel(cd_ref, q_ref, k_ref, v_ref, cos_ref, sin_ref, dec_ref, rd_ref, o_ref, s_ref):
    i = pl.program_id(1)

    @pl.when(i == 0)
    def _():
        s_ref[...] = jnp.zeros_like(s_ref)

    hs, q, k, v = _ret_heads(q_ref, k_ref, v_ref, cos_ref, sin_ref)
    scores = [(_bdot(q[h], k[h], _NT) * dec_ref[h]).astype(BF16) for h in hs]
    s = [s_ref[h] for h in hs]
    qd = [(q[h] * rd_ref[0, h]).astype(BF16) for h in hs]
    kd = [(k[h] * rd_ref[1, h]).astype(BF16) for h in hs]
    for h in hs:
        o_ref[:, h * RET_DV:(h + 1) * RET_DV] = _bdot(scores[h], v[h]) + _bdot(qd[h], s[h])
    for h in hs:
        s_ref[h] = s[h] * cd_ref[0, h] + _bdot(kd[h], v[h], _TN)


def _ret_bwd_kernel(cd_ref, q_ref, k_ref, v_ref, g_ref, cos_ref, sin_ref, rd_ref, op_ref, gn_ref, o_ref, s_ref):
    i = pl.program_id(1)

    @pl.when(i == 0)
    def _():
        s_ref[...] = jnp.zeros_like(s_ref)

    hs, q, k, v = _ret_heads(q_ref, k_ref, v_ref, cos_ref, sin_ref)
    s = [s_ref[h] for h in hs]
    qd = [(q[h] * rd_ref[0, h]).astype(BF16) for h in hs]
    kd = [(k[h] * rd_ref[1, h]).astype(BF16) for h in hs]
    o = [op_ref[:, h * RET_DV:(h + 1) * RET_DV] + _bdot(qd[h], s[h]) for h in hs]
    for h in hs:
        s_ref[h] = s[h] * cd_ref[1, h] + _bdot(kd[h], v[h], _TN)
    for h in hs:
        hv = slice(h * RET_DV, (h + 1) * RET_DV)
        mu = jnp.mean(o[h], -1, keepdims=True)
        oc = o[h] - mu
        var = jnp.mean(oc * oc, -1, keepdims=True)
        g = g_ref[:, hv].astype(F32)
        o_ref[:, hv] = (g * _sigmoid(g) * (oc * lax.rsqrt(var + GN_EPS) * gn_ref[:, hv])).astype(o_ref.dtype)


def retention_mix(lay, p, log_gamma, gn_g, cos_t, sin_t):
    c = RET_CHUNK
    n = lay.n_chunks(c)
    n_tok = p.shape[0]
    hk = RET_HEADS * RET_DK
    hv = RET_HEADS * RET_DV

    def specs(backward):
        sc = lambda i: lay.seq_chunk(c, backward, i)
        blk = lambda w, off: pl.BlockSpec((c, w), lambda b, i: (lay.row_block(c, b, sc(i)), off))
        tab = pl.BlockSpec((c, RET_DK), lambda b, i: (sc(i), 0))
        return blk, tab

    pos = jnp.arange(c, dtype=F32)
    lag = pos[:, None] - pos[None, :]
    lg_f, lg_b = log_gamma[0][:, None, None], log_gamma[1][:, None, None]
    k_scale = RET_DK ** -0.5
    dec = (jnp.where(lag >= 0, jnp.exp(jnp.maximum(lag, 0.0) * lg_f), 0.0)
           + jnp.where(lag <= 0, jnp.exp(jnp.maximum(-lag, 0.0) * lg_b), 0.0)) * k_scale
    rows = lambda e, lg: jnp.exp(e[None, :] * lg[:, None])[..., None]
    rd_f = jnp.stack([rows(pos + 1.0, log_gamma[0]), rows(c - 1.0 - pos, log_gamma[0]) * k_scale])
    rd_b = jnp.stack([rows(c - pos, log_gamma[1]), rows(pos, log_gamma[1]) * k_scale])
    cd = jnp.exp(c * log_gamma)

    smem = pl.BlockSpec(memory_space=pltpu.SMEM)
    full = lambda a: pl.BlockSpec(a.shape, lambda b, i: (0,) * a.ndim)
    grid = (lay.batch, n)
    state = [pltpu.VMEM((RET_HEADS, RET_DK, RET_DV), F32)]
    blk, tab = specs(False)
    o_part = pl.pallas_call(
        _ret_fwd_kernel, grid=grid,
        in_specs=[smem, blk(hk, 0), blk(hk, 1), blk(hv, 1), tab, tab, full(dec), full(rd_f)],
        out_specs=blk(hv, 0),
        out_shape=jax.ShapeDtypeStruct((n_tok, hv), F32),
        scratch_shapes=state, compiler_params=_params("arbitrary", "arbitrary"), name="ret_fwd",
    )(cd, p, p, p, cos_t, sin_t, dec, rd_f)
    blk, tab = specs(True)
    return pl.pallas_call(
        _ret_bwd_kernel, grid=grid,
        in_specs=[smem, blk(hk, 0), blk(hk, 1), blk(hv, 1), blk(hv, 2), tab, tab, full(rd_b),
                  blk(hv, 0), pl.BlockSpec((1, hv), lambda b, i: (0, 0))],
        out_specs=blk(hv, 0),
        out_shape=jax.ShapeDtypeStruct((n_tok, hv), BF16),
        scratch_shapes=state, compiler_params=_params("arbitrary", "arbitrary"), name="ret_bwd",
    )(cd, p, p, p, p, cos_t, sin_t, rd_b, o_part, gn_g.reshape(1, hv))


def rope_tables(lay):
    quarter = RET_DK // 4
    t = jnp.arange(lay.seq)
    inv = ROPE_BASE ** (-jnp.arange(quarter, dtype=F32) / quarter)
    ang_r = (t // GRID_W).astype(F32)[:, None] * inv
    ang_c = (t % GRID_W).astype(F32)[:, None] * inv
    cos = jnp.concatenate([jnp.cos(ang_r)] * 2 + [jnp.cos(ang_c)] * 2, -1)
    sin = jnp.concatenate([-jnp.sin(ang_r), jnp.sin(ang_r), -jnp.sin(ang_c), jnp.sin(ang_c)], -1)
    cos = jnp.concatenate([jnp.ones((lay.n_ctx, RET_DK), F32), cos], 0)
    sin = jnp.concatenate([jnp.zeros((lay.n_ctx, RET_DK), F32), sin], 0)
    return cos, sin


def _dn_prep_kernel(lay, tm, prev_ref, x_ref, next_ref, w_ref, o_ref, xe_s):
    j = pl.program_id(0)
    ct = pl.program_id(1)
    r0 = j * tm
    in_ctx = r0 < lay.ctx_tok
    seq_len = jnp.where(in_ctx, lay.n_ctx, lay.seq)
    off = jnp.where(in_ctx, r0, r0 - lay.ctx_tok) % seq_len
    first = off == 0
    last = off + tm == seq_len
    hal = SEQ_HALO
    xe_s[0:hal, :] = jnp.where(first, 0.0, prev_ref[...].astype(F32))
    xe_s[hal:hal + tm, :] = x_ref[...].astype(F32)
    xe_s[hal + tm:, :] = jnp.where(last, 0.0, next_ref[...].astype(F32))
    pad = (DN_CONV - 1) // 2
    acc = xe_s[pl.ds(hal - pad, tm), :] * w_ref[0:1, :]
    for d in range(1, DN_CONV):
        acc = acc + xe_s[pl.ds(hal - pad + d, tm), :] * w_ref[d:d + 1, :]
    y = acc * _sigmoid(acc)
    n_qk_tiles = 2 * DN_QK_W // x_ref.shape[1]

    @pl.when(ct >= n_qk_tiles)
    def _():
        o_ref[...] = y.astype(o_ref.dtype)

    @pl.when(ct < n_qk_tiles)
    def _():
        for s in range(x_ref.shape[1] // DN_HEAD_DIM):
            is_q = ct * x_ref.shape[1] + s * DN_HEAD_DIM < DN_QK_W
            scale = jnp.where(is_q, DN_HEAD_DIM ** -0.5, 1.0)
            ys = y[:, s * DN_HEAD_DIM:(s + 1) * DN_HEAD_DIM]
            inv = lax.rsqrt(jnp.sum(ys * ys, -1, keepdims=True) + L2_EPS) * scale
            o_ref[:, s * DN_HEAD_DIM:(s + 1) * DN_HEAD_DIM] = (ys * inv).astype(o_ref.dtype)


def dn_prep(lay, p, conv_w):
    n_tok = p.shape[0]
    w = 2 * DN_QK_W + DN_V_W
    tm = math.gcd(256, lay.n_ctx, lay.seq)
    tc = 2 * DN_QK_W
    hb = tm // SEQ_HALO
    last_hb = n_tok // SEQ_HALO - 1
    return pl.pallas_call(
        functools.partial(_dn_prep_kernel, lay, tm),
        grid=(n_tok // tm, w // tc),
        in_specs=[pl.BlockSpec((SEQ_HALO, tc), lambda j, c: (jnp.maximum(j * hb - 1, 0), c)),
                  pl.BlockSpec((tm, tc), lambda j, c: (j, c)),
                  pl.BlockSpec((SEQ_HALO, tc), lambda j, c: (jnp.minimum((j + 1) * hb, last_hb), c)),
                  pl.BlockSpec((DN_CONV, tc), lambda j, c: (0, c))],
        out_specs=pl.BlockSpec((tm, tc), lambda j, c: (j, c)),
        out_shape=jax.ShapeDtypeStruct((n_tok, w), BF16),
        scratch_shapes=[pltpu.VMEM((tm + 2 * SEQ_HALO, tc), F32)],
        compiler_params=_params("arbitrary", "arbitrary"), name="dn_prep",
    )(p, p, p, conv_w)


def _dn_scan_kernel(backward, q_ref, k_ref, v_ref, ab_ref, na_row, dt_row, *rest):
    s_ref = rest[-1]

    @pl.when(pl.program_id(1) == 0)
    def _():
        s_ref[...] = jnp.zeros_like(s_ref)

    for j in range(DN_STEP_CHUNKS):
        sub = DN_STEP_CHUNKS - 1 - j if backward else j
        rows = lambda r: r.at[pl.ds(sub * DN_CHUNK, DN_CHUNK)]
        if backward:
            of_ref, z_ref, ng_ref, o_ref, _ = rest
            tail = (rows(of_ref), rows(z_ref), ng_ref, rows(o_ref), s_ref)
        else:
            tail = (rows(rest[0]), s_ref)
        _dn_chunk(backward, rows(q_ref), rows(k_ref), rows(v_ref), rows(ab_ref), na_row, dt_row, *tail)


def _dn_chunk(backward, q_ref, k_ref, v_ref, ab_ref, na_row, dt_row, *rest):
    if backward:
        of_ref, z_ref, ng_ref, o_ref, s_ref = rest
    else:
        o_ref, s_ref = rest
    c = q_ref.shape[0]
    z = 1 if backward else 0
    nh = DN_V_HEADS
    hd = DN_HEAD_DIM
    rep = DN_V_HEADS // DN_QK_HEADS

    row = lax.broadcasted_iota(jnp.int32, (c, c), 0)
    col = lax.broadcasted_iota(jnp.int32, (c, c), 1)
    lag = (col - row) if backward else (row - col)
    incl = lag >= 0
    strict = lag > 0
    eye = (row == col).astype(F32)
    ab = ab_ref[...]
    g_cols = na_row[...] * _softplus(ab + dt_row[...])
    beta_cols = _sigmoid(ab)
    gc_cols = _hdot(incl.astype(F32), g_cols)
    gc_rows = lax.dot_general(gc_cols, eye, _TN, precision=lax.Precision.HIGHEST,
                              preferred_element_type=F32)
    gend = jnp.sum(g_cols, axis=0, keepdims=True)
    heads = range(nh)
    gi = [z * 2 * nh + h for h in heads]
    bi = [z * 2 * nh + nh + h for h in heads]
    gcc = [gc_cols[:, gi[h]:gi[h] + 1] for h in heads]
    dec = [jnp.where(incl, jnp.exp(jnp.minimum(gcc[h] - gc_rows[gi[h]:gi[h] + 1, :], 0.0)), 0.0) for h in heads]
    beta = [beta_cols[:, bi[h]:bi[h] + 1] for h in heads]
    egc = [jnp.exp(gcc[h]) for h in heads]
    eend = [jnp.exp(gend[:, gi[h]:gi[h] + 1] - gcc[h]) for h in heads]
    tail = [jnp.exp(gend[:, gi[h]:gi[h] + 1]) for h in heads]
    qs = [q_ref[:, j * hd:(j + 1) * hd] for j in range(DN_QK_HEADS)]
    ks = [k_ref[:, j * hd:(j + 1) * hd] for j in range(DN_QK_HEADS)]
    vs = [v_ref[:, h * hd:(h + 1) * hd].astype(F32) for h in heads]
    kq = [_bdot(jnp.concatenate([ks[j], qs[j]], axis=0), ks[j], _NT) for j in range(DN_QK_HEADS)]
    neg_a = [jnp.where(strict, -(kq[h // rep][:c] * beta[h] * dec[h]), 0.0) for h in heads]
    qkd = [(kq[h // rep][c:] * dec[h]).astype(BF16) for h in heads]
    tm = _unit_tri_inverse(neg_a, eye, heads)
    kf = [ks[h // rep].astype(F32) for h in heads]
    rhs = [jnp.concatenate([vs[h] * beta[h], kf[h] * (beta[h] * egc[h])], axis=1) for h in heads]
    uw = [_bdot(tm[h], rhs[h]) for h in heads]
    s0 = [s_ref[h] for h in heads]
    lhs = [jnp.concatenate([uw[h][:, hd:], qs[h // rep].astype(F32) * egc[h]], axis=0) for h in heads]
    ws_qs = [_bdot(lhs[h], s0[h]) for h in heads]
    v_new = [(uw[h][:, :hd] - ws_qs[h][:c]).astype(BF16) for h in heads]
    o = [ws_qs[h][c:] + _bdot(qkd[h], v_new[h]) for h in heads]
    for h in heads:
        s_ref[h] = s0[h] * tail[h] + _bdot(kf[h] * eend[h], v_new[h], _TN)
    if not backward:
        for h in heads:
            o_ref[:, h * hd:(h + 1) * hd] = o[h]
    else:
        for h in heads:
            ot = o[h] + of_ref[:, h * hd:(h + 1) * hd]
            on = ot * lax.rsqrt(jnp.mean(ot * ot, -1, keepdims=True) + RMS_EPS) * ng_ref[...]
            zz = z_ref[:, h * hd:(h + 1) * hd].astype(F32)
            o_ref[:, h * hd:(h + 1) * hd] = (on * (zz * _sigmoid(zz))).astype(o_ref.dtype)


def deltanet_mix(lay, qkv, p, ab, a_log, dt_bias, norm_g):
    c = DN_CHUNK * DN_STEP_CHUNKS
    n = lay.n_chunks(c)
    n_tok = qkv.shape[0]
    nh = DN_V_HEADS
    neg_a = -jnp.exp(a_log.astype(F32))
    na = jnp.concatenate([neg_a, jnp.zeros_like(neg_a)], axis=1).reshape(1, 4 * nh)
    dt = jnp.concatenate([dt_bias.astype(F32), jnp.zeros_like(neg_a)], axis=1).reshape(1, 4 * nh)
    small = lambda a: pl.BlockSpec(a.shape, lambda b, i: (0, 0))
    consts = (na, dt)
    state = [pltpu.VMEM((nh, DN_HEAD_DIM, DN_HEAD_DIM), F32)]

    def specs(backward):
        rb = lambda b, i: lay.row_block(c, b, lay.seq_chunk(c, backward, i))
        return lambda w, off: pl.BlockSpec((c, w), lambda b, i: (rb(b, i), off))

    common = lambda blk: [blk(DN_QK_W, 0), blk(DN_QK_W, 1), blk(DN_V_W, 1), blk(4 * nh, 0)] + [small(a) for a in consts]
    blk = specs(False)
    o_f = pl.pallas_call(
        functools.partial(_dn_scan_kernel, False), grid=(lay.batch, n),
        in_specs=common(blk), out_specs=blk(DN_V_W, 0),
        out_shape=jax.ShapeDtypeStruct((n_tok, DN_V_W), F32),
        scratch_shapes=state, compiler_params=_params("arbitrary", "arbitrary"), name="dn_scan_fwd",
    )(qkv, qkv, qkv, ab, *consts)
    blk = specs(True)
    return pl.pallas_call(
        functools.partial(_dn_scan_kernel, True), grid=(lay.batch, n),
        in_specs=common(blk) + [blk(DN_V_W, 0), blk(DN_V_W, 2), small(norm_g.reshape(1, DN_HEAD_DIM))],
        out_specs=blk(DN_V_W, 0),
        out_shape=jax.ShapeDtypeStruct((n_tok, DN_V_W), BF16),
        scratch_shapes=state, compiler_params=_params("arbitrary", "arbitrary"), name="dn_scan_bwd",
    )(qkv, qkv, qkv, ab, *consts, o_f, p, norm_g.reshape(1, DN_HEAD_DIM))


def _unit_tri_inverse(nm, eye, heads):
    c = eye.shape[0]
    tm = [eye + nm[h] for h in heads]
    p = [_bdot(nm[h], nm[h]).astype(BF16) for h in heads]
    for _ in range(int(math.log2(c)) - 2):
        pt = [_bdot(jnp.concatenate([p[h], tm[h].astype(BF16)], axis=0), p[h]) for h in heads]
        tm = [tm[h] + pt[h][c:] for h in heads]
        p = [pt[h][:c].astype(BF16) for h in heads]
    return [tm[h] + _bdot(tm[h], p[h]) for h in heads]


def _rwkv_chunk_kernel(r_ref, v_ref, kk_ref, wl_ref, kd_ref, a_ref, o_ref, s_ref, *scratch):
    z = pl.program_id(0)

    @pl.when(pl.program_id(2) == 0)
    def _():
        s_ref[...] = jnp.zeros_like(s_ref)

    for j in range(RWKV_STEP_CHUNKS):
        sub = j + z * (RWKV_STEP_CHUNKS - 1 - 2 * j)
        rows = lambda r: r.at[pl.ds(pl.multiple_of(sub * RWKV_CHUNK, RWKV_CHUNK), RWKV_CHUNK)]
        _rwkv_chunk(z, rows(r_ref), rows(v_ref), rows(kk_ref), rows(wl_ref), rows(kd_ref), rows(a_ref),
                    rows(o_ref), s_ref, *scratch)


def _rwkv_chunk(z, r_ref, v_ref, kk_ref, wl_ref, kd_ref, a_ref, o_ref, s_ref, ar_s, bt_s, bk_s, uv_s, gc_s):
    c = r_ref.shape[0]
    row = lax.broadcasted_iota(jnp.int32, (c, c), 0)
    col = lax.broadcasted_iota(jnp.int32, (c, c), 1)
    lag = (row - col) * (1 - 2 * z)
    incl = lag >= 0
    strict = lag > 0

    wl = wl_ref[...]
    logw = -jnp.exp(-_softplus(-wl) - 0.5)
    cum = _hdot(incl.astype(F32), logw)
    c_last = jnp.sum(logw, axis=0, keepdims=True)
    kk = kk_ref[...].astype(F32)
    kb = kk * a_ref[...].astype(F32)
    kd = kd_ref[...].astype(F32)
    g_inv = jnp.exp(-cum)
    e_end = jnp.exp(c_last - cum)
    ar_s[0:c, :] = (-kk * jnp.exp(cum - logw)).astype(BF16)
    ar_s[c:, :] = (r_ref[...].astype(F32) * jnp.exp(cum)).astype(BF16)
    bt_s[0:c, :] = (kb * g_inv).astype(BF16)
    bt_s[c:, :] = (kd * g_inv).astype(BF16)
    bk_s[0:c, :] = (kb * e_end).astype(BF16)
    bk_s[c:, :] = (kd * e_end).astype(BF16)
    uv_s[c:, :] = v_ref[...].astype(BF16)
    gc_s[...] = jnp.exp(c_last)

    eye = (row == col).astype(F32)
    hh = range(RWKV_HEADS)
    sl = [slice(h * RWKV_HEAD, (h + 1) * RWKV_HEAD) for h in hh]
    lag2 = jnp.concatenate([lag, lag + 1], axis=0)
    mask4 = jnp.concatenate([lag2, lag2], axis=1) > 0
    x4 = [jnp.where(mask4, _bdot(ar_s[:, sl[h]], bt_s[:, sl[h]], _NT), 0.0) for h in hh]
    xb = [x4[h][:, :c] for h in hh]
    xk = [x4[h][:, c:].astype(BF16) for h in hh]
    tm = _unit_tri_inverse([xb[h][:c] for h in hh], eye, hh)
    s0 = [s_ref[h] for h in hh]
    xs = [_bdot(ar_s[:, sl[h]], s0[h], _NT) + _bdot(xk[h], uv_s[c:, sl[h]]) for h in hh]
    u = [_bdot(tm[h], xs[h][:c]).astype(BF16) for h in hh]
    for h in hh:
        uv_s[0:c, sl[h]] = u[h]
        o_ref[:, sl[h]] = xs[h][c:] + _bdot(xb[h][c:], u[h])
    for h in hh:
        s_ref[h] = s0[h] * gc_s[:, sl[h]] + _bdot(uv_s[:, sl[h]], bk_s[:, sl[h]], _TN)


def rwkv_scan(lay, r, v, kk, wl, kd, a):
    n_tok, d = r.shape
    step = RWKV_CHUNK * RWKV_STEP_CHUNKS
    n = lay.n_chunks(step)

    def rb(z, b, i):
        sc = jnp.where(z == 0, lay.seq_chunk(step, False, i), lay.seq_chunk(step, True, i))
        return lay.row_block(step, b, sc)

    shared = pl.BlockSpec((step, d), lambda z, b, i: (rb(z, b, i), 0))
    perdir = pl.BlockSpec((None, step, d), lambda z, b, i: (z, rb(z, b, i), 0))
    c = RWKV_CHUNK
    bf = lambda rows: pltpu.VMEM((rows, d), BF16)
    return pl.pallas_call(
        _rwkv_chunk_kernel,
        grid=(2, lay.batch, n),
        in_specs=[shared, shared, shared, perdir, perdir, perdir],
        out_specs=perdir,
        out_shape=jax.ShapeDtypeStruct((2, n_tok, d), F32),
        scratch_shapes=[pltpu.VMEM((RWKV_HEADS, RWKV_HEAD, RWKV_HEAD), F32),
                        bf(2 * c), bf(2 * c), bf(2 * c), bf(2 * c), pltpu.VMEM((1, d), F32)],
        compiler_params=_params("arbitrary", "arbitrary", "arbitrary"),
        name="rwkv_scan",
    )(r, v, kk, wl, kd, a)


def _rwkv_pre_kernel(lay, tm, prev_ref, h_ref, next_ref, mix_ref, wr_ref, wk_ref, wv_ref, w1_ref, a1_ref, g1_ref,
                     w2_ref, a2_ref, g2_ref, w0_ref, a0_ref, kk_ref, ka_ref, rk_ref, seg_ref,
                     r_out, v_out, kk_out, wl_out, kd_out, a_out, g_out, bonus_out, xe_s):
    j = pl.program_id(0)
    r0 = j * tm
    in_ctx = r0 < lay.ctx_tok
    seq_len = jnp.where(in_ctx, lay.n_ctx, lay.seq)
    off = jnp.where(in_ctx, r0, r0 - lay.ctx_tok) % seq_len
    hal = SEQ_HALO
    xe_s[0:hal, :] = jnp.where(off == 0, 0.0, prev_ref[...].astype(F32))
    xe_s[hal:hal + tm, :] = h_ref[...].astype(F32)
    xe_s[hal + tm:, :] = jnp.where(off + tm == seq_len, 0.0, next_ref[...].astype(F32))
    h = xe_s[pl.ds(hal, tm), :]
    xx = 0.5 * (xe_s[pl.ds(hal - 1, tm), :] + xe_s[pl.ds(hal + 1, tm), :]) - h
    xm = lambda i: (h + xx * mix_ref[i:i + 1, :]).astype(BF16)
    r = _bdot(xm(0), wr_ref[...])
    k = _bdot(xm(1), wk_ref[...])
    v = _bdot(xm(2), wv_ref[...])
    hw = jnp.tanh(_bdot(xm(3), w1_ref[...]))
    ha = _bdot(xm(4), a1_ref[...])
    hg = _sigmoid(_bdot(xm(5), g1_ref[...]))
    seg = seg_ref[...]
    kx = k * kk_ref[...]
    r_out[...] = r.astype(r_out.dtype)
    v_out[...] = v.astype(v_out.dtype)
    kk_out[...] = (kx * lax.rsqrt(_bdot(kx * kx, seg) + L2_EPS)).astype(kk_out.dtype)
    g_out[...] = _bdot(hg, g2_ref[...]).astype(g_out.dtype)
    lw = w2_ref.shape[1]
    rr = r * rk_ref[...]
    bsum = None
    for z in range(2):
        wl_out[z] = w0_ref[z:z + 1, :] + _bdot(hw[:, z * lw:(z + 1) * lw], w2_ref[z])
        a = _sigmoid(a0_ref[z:z + 1, :] + _bdot(ha[:, z * lw:(z + 1) * lw], a2_ref[z]))
        kd = k * (1.0 + (a - 1.0) * ka_ref[...])
        a_out[z] = a.astype(a_out.dtype)
        kd_out[z] = kd.astype(kd_out.dtype)
        bsum = rr * kd if bsum is None else bsum + rr * kd
    bonus_out[...] = (_bdot(bsum, seg) * v).astype(bonus_out.dtype)


def _rwkv_post_kernel(o_ref, g_ref, bonus_ref, lnx_ref, seg_ref, a_out):
    seg = seg_ref[...]
    o = o_ref[0] + o_ref[1]
    hi = o.astype(BF16)
    lo = o - hi.astype(F32)
    inv_n = 1.0 / RWKV_HEAD
    oc = o - (_bdot(hi, seg) + _bdot(lo, seg)) * inv_n
    var = _bdot(oc * oc, seg) * inv_n
    on = oc * lax.rsqrt(var + LNX_EPS) * lnx_ref[...] + bonus_ref[...].astype(F32)
    a_out[...] = (on * g_ref[...].astype(F32)).astype(a_out.dtype)


def rwkv7_mix(lay, h, mix, w_rkv, w0, w1, w2, a0, a1, a2, g1, g2, k_k, k_a, r_k, lnx_g):
    n_tok, d = h.shape
    bw = lambda w: w.astype(BF16)
    tm = math.gcd(256, lay.n_ctx, lay.seq)
    hb = tm // SEQ_HALO
    last_hb = n_tok // SEQ_HALO - 1
    head_of = jnp.arange(d) // RWKV_HEAD
    seg = (head_of[:, None] == head_of[None, :]).astype(BF16)
    full = lambda a: pl.BlockSpec(a.shape, lambda j: (0,) * a.ndim)
    row = lambda a: a.reshape(1, d)
    consts = (mix, bw(w_rkv[0]), bw(w_rkv[1]), bw(w_rkv[2]), bw(jnp.concatenate([w1[0], w1[1]], -1)),
              bw(jnp.concatenate([a1[0], a1[1]], -1)), bw(g1), bw(w2), bw(a2), bw(g2), w0, a0,
              row(k_k), row(k_a), row(r_k), seg)
    tok = pl.BlockSpec((tm, d), lambda j: (j, 0))
    tok2 = pl.BlockSpec((2, tm, d), lambda j: (0, j, 0))
    one = lambda dt: jax.ShapeDtypeStruct((n_tok, d), dt)
    two = lambda dt: jax.ShapeDtypeStruct((2, n_tok, d), dt)
    r, v, kk, wl, kd, a, g, bonus = pl.pallas_call(
        functools.partial(_rwkv_pre_kernel, lay, tm),
        grid=(n_tok // tm,),
        in_specs=[pl.BlockSpec((SEQ_HALO, d), lambda j: (jnp.maximum(j * hb - 1, 0), 0)), tok,
                  pl.BlockSpec((SEQ_HALO, d), lambda j: (jnp.minimum((j + 1) * hb, last_hb), 0))]
                 + [full(a) for a in consts],
        out_specs=[tok, tok, tok, tok2, tok2, tok2, tok, tok],
        out_shape=[one(BF16), one(BF16), one(BF16), two(F32), two(BF16), two(BF16), one(BF16), one(BF16)],
        scratch_shapes=[pltpu.VMEM((tm + 2 * SEQ_HALO, d), F32)],
        compiler_params=_params("arbitrary"), name="rwkv_pre",
    )(h, h, h, *consts)
    o = rwkv_scan(lay, r, v, kk, wl, kd, a)
    return pl.pallas_call(
        _rwkv_post_kernel,
        grid=(n_tok // tm,),
        in_specs=[tok2, tok, tok, full(row(lnx_g)), full(seg)],
        out_specs=tok,
        out_shape=one(BF16),
        compiler_params=_params("arbitrary"), name="rwkv_post",
    )(o, g, bonus, row(lnx_g), seg)


def _moe_kernel(be_ref, nb_ref, x_ref, wg_ref, wu_ref, wd_ref, gate_ref, o_ref, acc_s):
    j = pl.program_id(0)
    f = pl.program_id(1)

    @pl.when(f == 0)
    def _():
        acc_s[...] = jnp.zeros_like(acc_s)

    @pl.when(j < nb_ref[0])
    def _():
        x = x_ref[...]
        g = _bdot(x, wg_ref[...])
        u = _bdot(x, wu_ref[...])
        acc_s[...] += _bdot(g * _sigmoid(g) * u, wd_ref[...])

    @pl.when(f == pl.num_programs(1) - 1)
    def _():
        o_ref[...] = (acc_s[...] * gate_ref[...]).astype(o_ref.dtype)


def moe_experts(xb, block_e, n_used, w_gu, w_down, layer, slot_gate):
    n_slots, d = xb.shape
    bm = MOE_ROWS
    nf = MOE_F_CHUNKS
    tf = w_down.shape[2] // nf
    return pl.pallas_call(
        _moe_kernel,
        grid_spec=pltpu.PrefetchScalarGridSpec(
            num_scalar_prefetch=2,
            grid=(n_slots // bm, nf),
            in_specs=[pl.BlockSpec((bm, d), lambda j, f, be, nb: (j, 0)),
                      pl.BlockSpec((None, None, d, tf), lambda j, f, be, nb: (layer, be[j], 0, f)),
                      pl.BlockSpec((None, None, d, tf), lambda j, f, be, nb: (layer, be[j], 0, f + nf)),
                      pl.BlockSpec((None, None, tf, d), lambda j, f, be, nb: (layer, be[j], f, 0)),
                      pl.BlockSpec((bm, 1), lambda j, f, be, nb: (j, 0))],
            out_specs=pl.BlockSpec((bm, d), lambda j, f, be, nb: (j, 0)),
            scratch_shapes=[pltpu.VMEM((bm, d), F32)]),
        out_shape=jax.ShapeDtypeStruct((n_slots, d), BF16),
        compiler_params=_params("arbitrary", "arbitrary"),
        name="moe_experts",
    )(block_e, n_used, xb, w_gu, w_gu, w_down, slot_gate)


def moe_swiglu(h, w_router, w_gu, w_down, layer):
    n, d = h.shape
    logits = proj(h, w_router.astype(BF16), F32, N_EXPERTS)
    lanes = jnp.arange(N_EXPERTS, dtype=jnp.int32)
    e1 = jnp.argmax(logits, axis=-1).astype(jnp.int32)
    rest = jnp.where(lanes == e1[:, None], -jnp.inf, logits)
    e2 = jnp.argmax(rest, axis=-1).astype(jnp.int32)
    top_logit = jnp.stack([jnp.max(logits, axis=-1), jnp.max(rest, axis=-1)], axis=-1)
    top_e = jnp.stack([e1, e2], axis=-1)
    gate = jax.nn.softmax(top_logit, axis=-1)
    flat_e = top_e.reshape(-1).astype(jnp.int32)
    order = jnp.argsort(flat_e).astype(jnp.int32)
    onehot = (flat_e[:, None] == jnp.arange(N_EXPERTS, dtype=jnp.int32)).astype(jnp.int32)
    seen = jnp.cumsum(onehot, axis=0)
    counts = seen[-1]
    padded = (counts + MOE_ROWS - 1) // MOE_ROWS * MOE_ROWS
    start = jnp.cumsum(counts) - counts
    pend = jnp.cumsum(padded)
    pstart = pend - padded
    tok_slot = jnp.sum(onehot * (seen - 1 + pstart[None, :]), axis=1).reshape(n, TOP_K)
    n_slots = (n * TOP_K + MOE_ROWS - 1) // MOE_ROWS * MOE_ROWS + N_EXPERTS * MOE_ROWS
    n_blocks = n_slots // MOE_ROWS
    blocks = jnp.arange(n_blocks, dtype=jnp.int32)
    block_e = jnp.minimum(jnp.sum(blocks[:, None] * MOE_ROWS >= pend[None, :], axis=1),
                          N_EXPERTS - 1).astype(jnp.int32)
    n_used = (pend[-1] // MOE_ROWS).astype(jnp.int32).reshape(1)
    slot_off = (blocks * MOE_ROWS - pstart[block_e])[:, None] + jnp.arange(MOE_ROWS, dtype=jnp.int32)[None, :]
    slot_valid = (slot_off < counts[block_e][:, None]).reshape(-1)
    slot_asg = order[jnp.clip(start[block_e][:, None] + slot_off, 0, n * TOP_K - 1).reshape(-1)]
    slot_tok = jnp.where(slot_valid, slot_asg // TOP_K, 0)
    slot_gate = jnp.where(slot_valid, gate.reshape(-1)[slot_asg], 0.0)
    yb = moe_experts(h[slot_tok], block_e, n_used, w_gu, w_down, layer, slot_gate[:, None])
    return yb[tok_slot[:, 0]], yb[tok_slot[:, 1]]


def kernel(x, c, ctx, c_ctx, mod_w, mod_b, ln_g, ln_b,
           ret_w_in, ret_decay, ret_gn_g, ret_w_out,
           dn_w_in, dn_conv_w, dn_a_log, dn_dt_bias, dn_norm_g, dn_w_out,
           rk_mix, rk_w_rkv, rk_w0, rk_w1, rk_w2, rk_a0, rk_a1, rk_a2, rk_g1, rk_g2,
           rk_k_k, rk_k_a, rk_r_k, rk_lnx_g, rk_w_out,
           ffn_w_gu, ffn_w_down, moe_router, moe_w_gu, moe_w_down):
    bsz, t, d = x.shape
    n_ctx = ctx.shape[1]
    lay = Layout(bsz, n_ctx, t)
    bw = lambda w: w.astype(BF16)
    s_rows = jax.nn.silu(jnp.concatenate([c_ctx[None], c], 0))
    s_pad = jnp.zeros((8, d), F32).at[:1 + bsz].set(s_rows)
    mods = modulation_rows(s_pad, mod_w, mod_b)[:, :1 + bsz].reshape(DEPTH, 1 + bsz, 6, 1, d)
    mod = lambda i, k: mods[i, :, k]
    cos_t, sin_t = rope_tables(lay)
    xs = jnp.concatenate([ctx.reshape(-1, d), x.reshape(-1, d)], 0)
    h = modulate(lay, xs, mod(0, 1), mod(0, 0))
    for i in range(DEPTH):
        last = i == DEPTH - 1
        kind, j = i % N_MIXERS, i // N_MIXERS
        if kind == 0:
            p = proj(h, bw(ret_w_in[j]), BF16, PROJ_COLS)
            log_gamma = jax.nn.log_sigmoid(ret_decay[j].astype(F32))
            a = retention_mix(lay, p, log_gamma, ret_gn_g[j], cos_t, sin_t)
            w_out = ret_w_out[j]
        elif kind == 1:
            n_main = 2 * DN_QK_W + 2 * DN_V_W
            p = proj(h, bw(dn_w_in[j][:, :n_main]), BF16, PROJ_COLS)
            w_ab = bw(dn_w_in[j][:, n_main:])
            ab = proj(h, w_ab, F32, w_ab.shape[1])
            qkv = dn_prep(lay, p, dn_conv_w[j])
            a = deltanet_mix(lay, qkv, p, ab, dn_a_log[j], dn_dt_bias[j], dn_norm_g[j])
            w_out = dn_w_out[j]
        else:
            a = rwkv7_mix(lay, h, rk_mix[j], rk_w_rkv[j], rk_w0[j], rk_w1[j], rk_w2[j], rk_a0[j], rk_a1[j],
                          rk_a2[j], rk_g1[j], rk_g2[j], rk_k_k[j], rk_k_a[j], rk_r_k[j], rk_lnx_g[j])
            w_out = rk_w_out[j]
        xs, h = out_ln(lay, a, bw(w_out), xs, mod(i, 2), ln_g[i, 0], ln_b[i, 0], mod(i, 4), mod(i, 3))
        nxt = (i + 1) % DEPTH
        if i % 2 == 0:
            hm = swiglu_in(h, bw(ffn_w_gu[i // 2]), FFN_COLS)
            xs, h = out_ln(lay, hm, bw(ffn_w_down[i // 2]), xs, mod(i, 5), ln_g[i, 1], ln_b[i, 1],
                           mod(nxt, 1), mod(nxt, 0), latents_only=last)
        else:
            f = moe_swiglu(h, moe_router[i // 2], moe_w_gu, moe_w_down, i // 2)
            xs, h = out_ln(lay, f, None, xs, mod(i, 5), ln_g[i, 1], ln_b[i, 1], mod(nxt, 1), mod(nxt, 0),
                           latents_only=last)
    return xs.reshape(bsz, t, d)
```

```python
import math, functools
import jax
import jax.numpy as jnp
from jax import lax
import numpy as np
from jax.experimental import pallas as pl
from jax.experimental.pallas import tpu as pltpu

D_MODEL = 1024
DEPTH = 4
GRID_W = 64
N_MIXERS = 3
ALPHA = (2 * DEPTH) ** 0.25
LN_EPS = 1e-5
GN_EPS = 1e-5
RMS_EPS = 1e-6
LNX_EPS = 64e-5
L2_EPS = 1e-6

RET_HEADS = 4
RET_DK = D_MODEL // RET_HEADS
RET_DV = 2 * RET_DK
RET_CHUNK = 256
ROPE_BASE = 10000.0

DN_QK_HEADS = 8
DN_V_HEADS = 16
DN_HEAD_DIM = 128
DN_CHUNK = 64
DN_STEP_CHUNKS = 4
DN_CONV = 5
DN_QK_W = DN_QK_HEADS * DN_HEAD_DIM
DN_V_W = DN_V_HEADS * DN_HEAD_DIM
SEQ_HALO = 16

RWKV_HEAD = 64
RWKV_HEADS = D_MODEL // RWKV_HEAD
RWKV_CHUNK = 64
RWKV_STEP_CHUNKS = 4

FFN_DIM = 2816
N_EXPERTS = 8
TOP_K = 2
EXPERT_DIM = 3584
MOE_ROWS = 1024
MOE_F_CHUNKS = 7
PROJ_COLS = 2048
FFN_COLS = FFN_DIM // 2

ROW_TILE = 512
VMEM_LIMIT = 48 * 1024 * 1024

BF16 = jnp.bfloat16
F32 = jnp.float32
_NT = (((1,), (1,)), ((), ()))
_TN = (((0,), (0,)), ((), ()))


def _bdot(x, y, dims=None):
    x = x.astype(BF16)
    y = y.astype(BF16)
    if dims is None:
        return jnp.dot(x, y, preferred_element_type=F32)
    return lax.dot_general(x, y, dims, preferred_element_type=F32)


def _hdot(x, y):
    return jnp.dot(x, y, precision=lax.Precision.HIGHEST, preferred_element_type=F32)


def _sigmoid(x):
    return 1.0 / (1.0 + jnp.exp(-x))


def _softplus(x):
    return jnp.maximum(x, 0.0) + jnp.log(1.0 + jnp.exp(-jnp.abs(x)))


def _params(*sem):
    return pltpu.CompilerParams(dimension_semantics=sem, vmem_limit_bytes=VMEM_LIMIT)


class Layout:
    def __init__(self, batch, n_ctx, seq):
        self.batch, self.n_ctx, self.seq = batch, n_ctx, seq
        self.ctx_tok = batch * n_ctx
        self.n_tok = self.ctx_tok + batch * seq
        self.row_tile = math.gcd(ROW_TILE, n_ctx * batch, seq)

    def mod_index(self, tile, j):
        r0 = j * tile
        return jnp.where(r0 < self.ctx_tok, 0, 1 + (r0 - self.ctx_tok) // self.seq)

    def seq_chunk(self, chunk, backward, i):
        nc, nl = self.n_ctx // chunk, self.seq // chunk
        if not backward:
            return i
        return jnp.where(i < nc, nc - 1 - i, 2 * nc + nl - 1 - i)

    def row_block(self, chunk, b, sc):
        nc, nl = self.n_ctx // chunk, self.seq // chunk
        return jnp.where(sc < nc, b * nc + sc, self.batch * nc + b * nl + sc - nc)

    def n_chunks(self, chunk):
        return (self.n_ctx + self.seq) // chunk


def _proj_kernel(h_ref, w_ref, o_ref):
    o_ref[...] = _bdot(h_ref[...], w_ref[...]).astype(o_ref.dtype)


def proj(h, w, out_dtype, tn):
    n_tok, k = h.shape
    n = w.shape[1]
    tm = math.gcd(1024, n_tok)
    return pl.pallas_call(
        _proj_kernel,
        grid=(n // tn, n_tok // tm),
        in_specs=[pl.BlockSpec((tm, k), lambda c, j: (j, 0)), pl.BlockSpec((k, tn), lambda c, j: (0, c))],
        out_specs=pl.BlockSpec((tm, tn), lambda c, j: (j, c)),
        out_shape=jax.ShapeDtypeStruct((n_tok, n), out_dtype),
        compiler_params=_params("arbitrary", "arbitrary"),
        name="proj",
    )(h, w)


def _swiglu_in_kernel(h_ref, wg_ref, wu_ref, o_ref):
    h = h_ref[...]
    g = _bdot(h, wg_ref[...])
    u = _bdot(h, wu_ref[...])
    o_ref[...] = (g * _sigmoid(g) * u).astype(o_ref.dtype)


def swiglu_in(h, w_gu, tn):
    n_tok, k = h.shape
    f = w_gu.shape[1] // 2
    tm = math.gcd(1024, n_tok)
    nf = f // tn
    return pl.pallas_call(
        _swiglu_in_kernel,
        grid=(nf, n_tok // tm),
        in_specs=[pl.BlockSpec((tm, k), lambda c, j: (j, 0)),
                  pl.BlockSpec((k, tn), lambda c, j: (0, c)),
                  pl.BlockSpec((k, tn), lambda c, j: (0, c + nf))],
        out_specs=pl.BlockSpec((tm, tn), lambda c, j: (j, c)),
        out_shape=jax.ShapeDtypeStruct((n_tok, f), BF16),
        compiler_params=_params("arbitrary", "arbitrary"),
        name="swiglu_in",
    )(h, w_gu, w_gu)


def _deepnorm_epilogue(x, f, ga_ref, g_ref, b_ref, sc_ref, sh_ref, x_out, h_out):
    y = ALPHA * x + (1.0 + ga_ref[...]) * f
    mu = jnp.mean(y, -1, keepdims=True)
    yc = y - mu
    var = jnp.mean(yc * yc, -1, keepdims=True)
    xn = yc * lax.rsqrt(var + LN_EPS) * g_ref[...] + b_ref[...]
    x_out[...] = xn
    h = (xn * (1.0 + sc_ref[...]) + sh_ref[...]).astype(h_out.dtype)
    h_out[...] = h
    return h


def _out_ln_kernel(a_ref, w_ref, x_ref, ga_ref, g_ref, b_ref, sc_ref, sh_ref, x_out, h_out):
    _deepnorm_epilogue(x_ref[...], _bdot(a_ref[...], w_ref[...]), ga_ref, g_ref, b_ref, sc_ref, sh_ref, x_out, h_out)


def _out_ln_router_kernel(a_ref, w_ref, x_ref, ga_ref, g_ref, b_ref, sc_ref, sh_ref, wr_ref, x_out, h_out, lg_out):
    h = _deepnorm_epilogue(x_ref[...], _bdot(a_ref[...], w_ref[...]), ga_ref, g_ref, b_ref, sc_ref, sh_ref,
                           x_out, h_out)
    lg_out[...] = _bdot(h, wr_ref[...])


def _resid_ln_kernel(f1_ref, f2_ref, x_ref, ga_ref, g_ref, b_ref, sc_ref, sh_ref, x_out, h_out):
    f = f1_ref[...].astype(F32) + f2_ref[...].astype(F32)
    _deepnorm_epilogue(x_ref[...], f, ga_ref, g_ref, b_ref, sc_ref, sh_ref, x_out, h_out)


def out_ln(lay, a, w, x, gate, ln_g, ln_b, sc_next, sh_next, latents_only=False, router=None):
    n_tok, d = x.shape
    tm = lay.row_tile
    row = lambda j: (j, 0)
    ctx_tiles = lay.ctx_tok // tm if latents_only else 0
    x_out = pl.BlockSpec((tm, d), lambda j: (jnp.maximum(j - ctx_tiles, 0), 0))
    mod = pl.BlockSpec((None, 1, d), lambda j: (lay.mod_index(tm, j), 0, 0))
    vec = pl.BlockSpec((1, d), lambda j: (0, 0))
    tok = pl.BlockSpec((tm, d), row)
    extra, extra_specs, extra_out, extra_shape = (), [], [], []
    if w is None:
        body, lhs, lhs_specs = _resid_ln_kernel, tuple(a), [tok, tok]
    else:
        k = a.shape[1]
        body, lhs = _out_ln_kernel, (a, w)
        lhs_specs = [pl.BlockSpec((tm, k), row), pl.BlockSpec((k, d), lambda j: (0, 0))]
        if router is not None:
            ne = router.shape[1]
            body, extra = _out_ln_router_kernel, (router,)
            extra_specs = [pl.BlockSpec((d, ne), lambda j: (0, 0))]
            extra_out = [pl.BlockSpec((tm, ne), row)]
            extra_shape = [jax.ShapeDtypeStruct((n_tok, ne), F32)]
    return pl.pallas_call(
        body,
        grid=(n_tok // tm,),
        in_specs=lhs_specs + [tok, mod, vec, vec, mod, mod] + extra_specs,
        out_specs=[x_out, tok] + extra_out,
        out_shape=[jax.ShapeDtypeStruct((n_tok - ctx_tiles * tm, d), F32), jax.ShapeDtypeStruct((n_tok, d), BF16)]
                  + extra_shape,
        compiler_params=_params("arbitrary"),
        name="out_ln",
    )(*lhs, x, gate, ln_g.reshape(1, d), ln_b.reshape(1, d), sc_next, sh_next, *extra)


def _modulate_kernel(x_ref, sc_ref, sh_ref, h_out):
    h_out[...] = (x_ref[...] * (1.0 + sc_ref[...]) + sh_ref[...]).astype(h_out.dtype)


def modulate(lay, x, sc, sh):
    n_tok, d = x.shape
    tm = lay.row_tile
    mod = pl.BlockSpec((None, 1, d), lambda j: (lay.mod_index(tm, j), 0, 0))
    tok = pl.BlockSpec((tm, d), lambda j: (j, 0))
    return pl.pallas_call(
        _modulate_kernel, grid=(n_tok // tm,), in_specs=[tok, mod, mod], out_specs=tok,
        out_shape=jax.ShapeDtypeStruct((n_tok, d), BF16), compiler_params=_params("arbitrary"), name="modulate",
    )(x, sc, sh)


def _mod_kernel(s_ref, w_ref, b_ref, o_ref):
    o_ref[...] = _bdot(s_ref[...], w_ref[...]) + b_ref[...]


def modulation_rows(s, mod_w, mod_b):
    r, d = s.shape
    depth, _, n = mod_w.shape
    tn = 1024
    return pl.pallas_call(
        _mod_kernel,
        grid=(depth, n // tn),
        in_specs=[pl.BlockSpec((r, d), lambda i, c: (0, 0)),
                  pl.BlockSpec((None, d, tn), lambda i, c: (i, 0, c)),
                  pl.BlockSpec((None, 1, tn), lambda i, c: (i, 0, c))],
        out_specs=pl.BlockSpec((None, r, tn), lambda i, c: (i, 0, c)),
        out_shape=jax.ShapeDtypeStruct((depth, r, n), F32),
        compiler_params=_params("arbitrary", "arbitrary"),
        name="modulation_rows",
    )(s, mod_w, mod_b.reshape(depth, 1, n))


def _rope(x, cos, sin):
    half = x.shape[1] // 2
    parts = []
    for p in range(2):
        xs = x[:, p * half:(p + 1) * half]
        parts.append(xs * cos[:, p * half:(p + 1) * half]
                     + pltpu.roll(xs, half // 2, axis=1) * sin[:, p * half:(p + 1) * half])
    return jnp.concatenate(parts, axis=1)


def _ret_heads(q_ref, k_ref, v_ref, cos_ref, sin_ref):
    cos, sin = cos_ref[...], sin_ref[...]
    hs = range(RET_HEADS)
    q = [_rope(q_ref[:, h * RET_DK:(h + 1) * RET_DK].astype(F32), cos, sin) for h in hs]
    k = [_rope(k_ref[:, h * RET_DK:(h + 1) * RET_DK].astype(F32), cos, sin) for h in hs]
    v = [v_ref[:, h * RET_DV:(h + 1) * RET_DV] for h in hs]
    return hs, q, k, v


def _ret_fwd_kernel(cd_ref, q_ref, k_ref, v_ref, cos_ref, sin_ref, dec_ref, rd_ref, o_ref, s_ref):
    i = pl.program_id(1)

    @pl.when(i == 0)
    def _():
        s_ref[...] = jnp.zeros_like(s_ref)

    hs, q, k, v = _ret_heads(q_ref, k_ref, v_ref, cos_ref, sin_ref)
    scores = [(_bdot(q[h], k[h], _NT) * dec_ref[h]).astype(BF16) for h in hs]
    s = [s_ref[h] for h in hs]
    qd = [(q[h] * rd_ref[0, h]).astype(BF16) for h in hs]
    kd = [(k[h] * rd_ref[1, h]).astype(BF16) for h in hs]
    for h in hs:
        o_ref[:, h * RET_DV:(h + 1) * RET_DV] = _bdot(scores[h], v[h]) + _bdot(qd[h], s[h])
    for h in hs:
        s_ref[h] = s[h] * cd_ref[0, h] + _bdot(kd[h], v[h], _TN)


def _ret_bwd_kernel(cd_ref, q_ref, k_ref, v_ref, g_ref, cos_ref, sin_ref, rd_ref, op_ref, gn_ref, o_ref, s_ref):
    i = pl.program_id(1)

    @pl.when(i == 0)
    def _():
        s_ref[...] = jnp.zeros_like(s_ref)

    hs, q, k, v = _ret_heads(q_ref, k_ref, v_ref, cos_ref, sin_ref)
    s = [s_ref[h] for h in hs]
    qd = [(q[h] * rd_ref[0, h]).astype(BF16) for h in hs]
    kd = [(k[h] * rd_ref[1, h]).astype(BF16) for h in hs]
    o = [op_ref[:, h * RET_DV:(h + 1) * RET_DV] + _bdot(qd[h], s[h]) for h in hs]
    for h in hs:
        s_ref[h] = s[h] * cd_ref[1, h] + _bdot(kd[h], v[h], _TN)
    for h in hs:
        hv = slice(h * RET_DV, (h + 1) * RET_DV)
        mu = jnp.mean(o[h], -1, keepdims=True)
        oc = o[h] - mu
        var = jnp.mean(oc * oc, -1, keepdims=True)
        g = g_ref[:, hv].astype(F32)
        o_ref[:, hv] = (g * _sigmoid(g) * (oc * lax.rsqrt(var + GN_EPS) * gn_ref[:, hv])).astype(o_ref.dtype)


def retention_mix(lay, p, log_gamma, gn_g, cos_t, sin_t):
    c = RET_CHUNK
    n = lay.n_chunks(c)
    n_tok = p.shape[0]
    hk = RET_HEADS * RET_DK
    hv = RET_HEADS * RET_DV

    def specs(backward):
        sc = lambda i: lay.seq_chunk(c, backward, i)
        blk = lambda w, off: pl.BlockSpec((c, w), lambda b, i: (lay.row_block(c, b, sc(i)), off))
        tab = pl.BlockSpec((c, RET_DK), lambda b, i: (sc(i), 0))
        return blk, tab

    pos = jnp.arange(c, dtype=F32)
    lag = pos[:, None] - pos[None, :]
    lg_f, lg_b = log_gamma[0][:, None, None], log_gamma[1][:, None, None]
    k_scale = RET_DK ** -0.5
    dec = (jnp.where(lag >= 0, jnp.exp(jnp.maximum(lag, 0.0) * lg_f), 0.0)
           + jnp.where(lag <= 0, jnp.exp(jnp.maximum(-lag, 0.0) * lg_b), 0.0)) * k_scale
    rows = lambda e, lg: jnp.exp(e[None, :] * lg[:, None])[..., None]
    rd_f = jnp.stack([rows(pos + 1.0, log_gamma[0]), rows(c - 1.0 - pos, log_gamma[0]) * k_scale])
    rd_b = jnp.stack([rows(c - pos, log_gamma[1]), rows(pos, log_gamma[1]) * k_scale])
    cd = jnp.exp(c * log_gamma)

    smem = pl.BlockSpec(memory_space=pltpu.SMEM)
    full = lambda a: pl.BlockSpec(a.shape, lambda b, i: (0,) * a.ndim)
    grid = (lay.batch, n)
    state = [pltpu.VMEM((RET_HEADS, RET_DK, RET_DV), F32)]
    blk, tab = specs(False)
    o_part = pl.pallas_call(
        _ret_fwd_kernel, grid=grid,
        in_specs=[smem, blk(hk, 0), blk(hk, 1), blk(hv, 1), tab, tab, full(dec), full(rd_f)],
        out_specs=blk(hv, 0),
        out_shape=jax.ShapeDtypeStruct((n_tok, hv), F32),
        scratch_shapes=state, compiler_params=_params("arbitrary", "arbitrary"), name="ret_fwd",
    )(cd, p, p, p, cos_t, sin_t, dec, rd_f)
    blk, tab = specs(True)
    return pl.pallas_call(
        _ret_bwd_kernel, grid=grid,
        in_specs=[smem, blk(hk, 0), blk(hk, 1), blk(hv, 1), blk(hv, 2), tab, tab, full(rd_b),
                  blk(hv, 0), pl.BlockSpec((1, hv), lambda b, i: (0, 0))],
        out_specs=blk(hv, 0),
        out_shape=jax.ShapeDtypeStruct((n_tok, hv), BF16),
        scratch_shapes=state, compiler_params=_params("arbitrary", "arbitrary"), name="ret_bwd",
    )(cd, p, p, p, p, cos_t, sin_t, rd_b, o_part, gn_g.reshape(1, hv))


def rope_tables(lay):
    quarter = RET_DK // 4
    t = jnp.arange(lay.seq)
    inv = ROPE_BASE ** (-jnp.arange(quarter, dtype=F32) / quarter)
    ang_r = (t // GRID_W).astype(F32)[:, None] * inv
    ang_c = (t % GRID_W).astype(F32)[:, None] * inv
    cos = jnp.concatenate([jnp.cos(ang_r)] * 2 + [jnp.cos(ang_c)] * 2, -1)
    sin = jnp.concatenate([-jnp.sin(ang_r), jnp.sin(ang_r), -jnp.sin(ang_c), jnp.sin(ang_c)], -1)
    cos = jnp.concatenate([jnp.ones((lay.n_ctx, RET_DK), F32), cos], 0)
    sin = jnp.concatenate([jnp.zeros((lay.n_ctx, RET_DK), F32), sin], 0)
    return cos, sin


def _dn_prep_kernel(lay, tm, prev_ref, x_ref, next_ref, w_ref, o_ref, xe_s):
    j = pl.program_id(0)
    ct = pl.program_id(1)
    r0 = j * tm
    in_ctx = r0 < lay.ctx_tok
    seq_len = jnp.where(in_ctx, lay.n_ctx, lay.seq)
    off = jnp.where(in_ctx, r0, r0 - lay.ctx_tok) % seq_len
    first = off == 0
    last = off + tm == seq_len
    hal = SEQ_HALO
    xe_s[0:hal, :] = jnp.where(first, 0.0, prev_ref[...].astype(F32))
    xe_s[hal:hal + tm, :] = x_ref[...].astype(F32)
    xe_s[hal + tm:, :] = jnp.where(last, 0.0, next_ref[...].astype(F32))
    pad = (DN_CONV - 1) // 2
    acc = xe_s[pl.ds(hal - pad, tm), :] * w_ref[0:1, :]
    for d in range(1, DN_CONV):
        acc = acc + xe_s[pl.ds(hal - pad + d, tm), :] * w_ref[d:d + 1, :]
    y = acc * _sigmoid(acc)
    n_qk_tiles = 2 * DN_QK_W // x_ref.shape[1]

    @pl.when(ct >= n_qk_tiles)
    def _():
        o_ref[...] = y.astype(o_ref.dtype)

    @pl.when(ct < n_qk_tiles)
    def _():
        for s in range(x_ref.shape[1] // DN_HEAD_DIM):
            is_q = ct * x_ref.shape[1] + s * DN_HEAD_DIM < DN_QK_W
            scale = jnp.where(is_q, DN_HEAD_DIM ** -0.5, 1.0)
            ys = y[:, s * DN_HEAD_DIM:(s + 1) * DN_HEAD_DIM]
            inv = lax.rsqrt(jnp.sum(ys * ys, -1, keepdims=True) + L2_EPS) * scale
            o_ref[:, s * DN_HEAD_DIM:(s + 1) * DN_HEAD_DIM] = (ys * inv).astype(o_ref.dtype)


def dn_prep(lay, p, conv_w):
    n_tok = p.shape[0]
    w = 2 * DN_QK_W + DN_V_W
    tm = math.gcd(256, lay.n_ctx, lay.seq)
    tc = 2 * DN_QK_W
    hb = tm // SEQ_HALO
    last_hb = n_tok // SEQ_HALO - 1
    return pl.pallas_call(
        functools.partial(_dn_prep_kernel, lay, tm),
        grid=(n_tok // tm, w // tc),
        in_specs=[pl.BlockSpec((SEQ_HALO, tc), lambda j, c: (jnp.maximum(j * hb - 1, 0), c)),
                  pl.BlockSpec((tm, tc), lambda j, c: (j, c)),
                  pl.BlockSpec((SEQ_HALO, tc), lambda j, c: (jnp.minimum((j + 1) * hb, last_hb), c)),
                  pl.BlockSpec((DN_CONV, tc), lambda j, c: (0, c))],
        out_specs=pl.BlockSpec((tm, tc), lambda j, c: (j, c)),
        out_shape=jax.ShapeDtypeStruct((n_tok, w), BF16),
        scratch_shapes=[pltpu.VMEM((tm + 2 * SEQ_HALO, tc), F32)],
        compiler_params=_params("arbitrary", "arbitrary"), name="dn_prep",
    )(p, p, p, conv_w)


def _dn_scan_kernel(backward, q_ref, k_ref, v_ref, ab_ref, na_row, dt_row, *rest):
    s_ref = rest[-1]

    @pl.when(pl.program_id(1) == 0)
    def _():
        s_ref[...] = jnp.zeros_like(s_ref)

    for j in range(DN_STEP_CHUNKS):
        sub = DN_STEP_CHUNKS - 1 - j if backward else j
        rows = lambda r: r.at[pl.ds(sub * DN_CHUNK, DN_CHUNK)]
        if backward:
            of_ref, z_ref, ng_ref, o_ref, _ = rest
            tail = (rows(of_ref), rows(z_ref), ng_ref, rows(o_ref), s_ref)
        else:
            tail = (rows(rest[0]), s_ref)
        _dn_chunk(backward, rows(q_ref), rows(k_ref), rows(v_ref), rows(ab_ref), na_row, dt_row, *tail)


def _dn_chunk(backward, q_ref, k_ref, v_ref, ab_ref, na_row, dt_row, *rest):
    if backward:
        of_ref, z_ref, ng_ref, o_ref, s_ref = rest
    else:
        o_ref, s_ref = rest
    c = q_ref.shape[0]
    z = 1 if backward else 0
    nh = DN_V_HEADS
    hd = DN_HEAD_DIM
    rep = DN_V_HEADS // DN_QK_HEADS

    row = lax.broadcasted_iota(jnp.int32, (c, c), 0)
    col = lax.broadcasted_iota(jnp.int32, (c, c), 1)
    lag = (col - row) if backward else (row - col)
    incl = lag >= 0
    strict = lag > 0
    eye = (row == col).astype(F32)
    ab = ab_ref[...]
    g_cols = na_row[...] * _softplus(ab + dt_row[...])
    beta_cols = _sigmoid(ab)
    gc_cols = _hdot(incl.astype(F32), g_cols)
    gc_rows = lax.dot_general(gc_cols, eye, _TN, precision=lax.Precision.HIGHEST,
                              preferred_element_type=F32)
    gend = jnp.sum(g_cols, axis=0, keepdims=True)
    heads = range(nh)
    gi = [z * 2 * nh + h for h in heads]
    bi = [z * 2 * nh + nh + h for h in heads]
    gcc = [gc_cols[:, gi[h]:gi[h] + 1] for h in heads]
    dec = [jnp.where(incl, jnp.exp(jnp.minimum(gcc[h] - gc_rows[gi[h]:gi[h] + 1, :], 0.0)), 0.0) for h in heads]
    beta = [beta_cols[:, bi[h]:bi[h] + 1] for h in heads]
    egc = [jnp.exp(gcc[h]) for h in heads]
    eend = [jnp.exp(gend[:, gi[h]:gi[h] + 1] - gcc[h]) for h in heads]
    tail = [jnp.exp(gend[:, gi[h]:gi[h] + 1]) for h in heads]
    qs = [q_ref[:, j * hd:(j + 1) * hd] for j in range(DN_QK_HEADS)]
    ks = [k_ref[:, j * hd:(j + 1) * hd] for j in range(DN_QK_HEADS)]
    vs = [v_ref[:, h * hd:(h + 1) * hd].astype(F32) for h in heads]
    kq = [_bdot(jnp.concatenate([ks[j], qs[j]], axis=0), ks[j], _NT) for j in range(DN_QK_HEADS)]
    neg_a = [jnp.where(strict, -(kq[h // rep][:c] * beta[h] * dec[h]), 0.0) for h in heads]
    qkd = [(kq[h // rep][c:] * dec[h]).astype(BF16) for h in heads]
    tm = _unit_tri_inverse(neg_a, eye, heads)
    kf = [ks[h // rep].astype(F32) for h in heads]
    rhs = [jnp.concatenate([vs[h] * beta[h], kf[h] * (beta[h] * egc[h])], axis=1) for h in heads]
    uw = [_bdot(tm[h], rhs[h]) for h in heads]
    s0 = [s_ref[h] for h in heads]
    lhs = [jnp.concatenate([uw[h][:, hd:], qs[h // rep].astype(F32) * egc[h]], axis=0) for h in heads]
    ws_qs = [_bdot(lhs[h], s0[h]) for h in heads]
    v_new = [(uw[h][:, :hd] - ws_qs[h][:c]).astype(BF16) for h in heads]
    o = [ws_qs[h][c:] + _bdot(qkd[h], v_new[h]) for h in heads]
    for h in heads:
        s_ref[h] = s0[h] * tail[h] + _bdot(kf[h] * eend[h], v_new[h], _TN)
    if not backward:
        for h in heads:
            o_ref[:, h * hd:(h + 1) * hd] = o[h]
    else:
        for h in heads:
            ot = o[h] + of_ref[:, h * hd:(h + 1) * hd]
            on = ot * lax.rsqrt(jnp.mean(ot * ot, -1, keepdims=True) + RMS_EPS) * ng_ref[...]
            zz = z_ref[:, h * hd:(h + 1) * hd].astype(F32)
            o_ref[:, h * hd:(h + 1) * hd] = (on * (zz * _sigmoid(zz))).astype(o_ref.dtype)


def deltanet_mix(lay, qkv, p, ab, a_log, dt_bias, norm_g):
    c = DN_CHUNK * DN_STEP_CHUNKS
    n = lay.n_chunks(c)
    n_tok = qkv.shape[0]
    nh = DN_V_HEADS
    neg_a = -jnp.exp(a_log.astype(F32))
    na = jnp.concatenate([neg_a, jnp.zeros_like(neg_a)], axis=1).reshape(1, 4 * nh)
    dt = jnp.concatenate([dt_bias.astype(F32), jnp.zeros_like(neg_a)], axis=1).reshape(1, 4 * nh)
    small = lambda a: pl.BlockSpec(a.shape, lambda b, i: (0, 0))
    consts = (na, dt)
    state = [pltpu.VMEM((nh, DN_HEAD_DIM, DN_HEAD_DIM), F32)]

    def specs(backward):
        rb = lambda b, i: lay.row_block(c, b, lay.seq_chunk(c, backward, i))
        return lambda w, off: pl.BlockSpec((c, w), lambda b, i: (rb(b, i), off))

    common = lambda blk: [blk(DN_QK_W, 0), blk(DN_QK_W, 1), blk(DN_V_W, 1), blk(4 * nh, 0)] + [small(a) for a in consts]
    blk = specs(False)
    o_f = pl.pallas_call(
        functools.partial(_dn_scan_kernel, False), grid=(lay.batch, n),
        in_specs=common(blk), out_specs=blk(DN_V_W, 0),
        out_shape=jax.ShapeDtypeStruct((n_tok, DN_V_W), F32),
        scratch_shapes=state, compiler_params=_params("arbitrary", "arbitrary"), name="dn_scan_fwd",
    )(qkv, qkv, qkv, ab, *consts)
    blk = specs(True)
    return pl.pallas_call(
        functools.partial(_dn_scan_kernel, True), grid=(lay.batch, n),
        in_specs=common(blk) + [blk(DN_V_W, 0), blk(DN_V_W, 2), small(norm_g.reshape(1, DN_HEAD_DIM))],
        out_specs=blk(DN_V_W, 0),
        out_shape=jax.ShapeDtypeStruct((n_tok, DN_V_W), BF16),
        scratch_shapes=state, compiler_params=_params("arbitrary", "arbitrary"), name="dn_scan_bwd",
    )(qkv, qkv, qkv, ab, *consts, o_f, p, norm_g.reshape(1, DN_HEAD_DIM))


def _unit_tri_inverse(nm, eye, heads):
    c = eye.shape[0]
    tm = [eye + nm[h] for h in heads]
    p = [_bdot(nm[h], nm[h]).astype(BF16) for h in heads]
    for _ in range(int(math.log2(c)) - 2):
        pt = [_bdot(jnp.concatenate([p[h], tm[h].astype(BF16)], axis=0), p[h]) for h in heads]
        tm = [tm[h] + pt[h][c:] for h in heads]
        p = [pt[h][:c].astype(BF16) for h in heads]
    return [tm[h] + _bdot(tm[h], p[h]) for h in heads]


def _rwkv_chunk_kernel(r_ref, v_ref, kk_ref, wl_ref, kd_ref, a_ref, o_ref, s_ref, *scratch):
    z = pl.program_id(0)

    @pl.when(pl.program_id(2) == 0)
    def _():
        s_ref[...] = jnp.zeros_like(s_ref)

    for j in range(RWKV_STEP_CHUNKS):
        sub = j + z * (RWKV_STEP_CHUNKS - 1 - 2 * j)
        rows = lambda r: r.at[pl.ds(pl.multiple_of(sub * RWKV_CHUNK, RWKV_CHUNK), RWKV_CHUNK)]
        _rwkv_chunk(z, rows(r_ref), rows(v_ref), rows(kk_ref), rows(wl_ref), rows(kd_ref), rows(a_ref),
                    rows(o_ref), s_ref, *scratch)


def _rwkv_chunk(z, r_ref, v_ref, kk_ref, wl_ref, kd_ref, a_ref, o_ref, s_ref, ar_s, bt_s, bk_s, uv_s, gc_s):
    c = r_ref.shape[0]
    row = lax.broadcasted_iota(jnp.int32, (c, c), 0)
    col = lax.broadcasted_iota(jnp.int32, (c, c), 1)
    lag = (row - col) * (1 - 2 * z)
    incl = lag >= 0
    strict = lag > 0

    wl = wl_ref[...]
    logw = -jnp.exp(-_softplus(-wl) - 0.5)
    cum = _hdot(incl.astype(F32), logw)
    c_last = jnp.sum(logw, axis=0, keepdims=True)
    kk = kk_ref[...].astype(F32)
    kb = kk * a_ref[...].astype(F32)
    kd = kd_ref[...].astype(F32)
    g_inv = jnp.exp(-cum)
    e_end = jnp.exp(c_last - cum)
    ar_s[0:c, :] = (-kk * jnp.exp(cum - logw)).astype(BF16)
    ar_s[c:, :] = (r_ref[...].astype(F32) * jnp.exp(cum)).astype(BF16)
    bt_s[0:c, :] = (kb * g_inv).astype(BF16)
    bt_s[c:, :] = (kd * g_inv).astype(BF16)
    bk_s[0:c, :] = (kb * e_end).astype(BF16)
    bk_s[c:, :] = (kd * e_end).astype(BF16)
    uv_s[c:, :] = v_ref[...].astype(BF16)
    gc_s[...] = jnp.exp(c_last)

    eye = (row == col).astype(F32)
    hh = range(RWKV_HEADS)
    sl = [slice(h * RWKV_HEAD, (h + 1) * RWKV_HEAD) for h in hh]
    lag2 = jnp.concatenate([lag, lag + 1], axis=0)
    mask4 = jnp.concatenate([lag2, lag2], axis=1) > 0
    x4 = [jnp.where(mask4, _bdot(ar_s[:, sl[h]], bt_s[:, sl[h]], _NT), 0.0) for h in hh]
    xb = [x4[h][:, :c] for h in hh]
    xk = [x4[h][:, c:].astype(BF16) for h in hh]
    tm = _unit_tri_inverse([xb[h][:c] for h in hh], eye, hh)
    s0 = [s_ref[h] for h in hh]
    xs = [_bdot(ar_s[:, sl[h]], s0[h], _NT) + _bdot(xk[h], uv_s[c:, sl[h]]) for h in hh]
    u = [_bdot(tm[h], xs[h][:c]).astype(BF16) for h in hh]
    for h in hh:
        uv_s[0:c, sl[h]] = u[h]
        o_ref[:, sl[h]] = xs[h][c:] + _bdot(xb[h][c:], u[h])
    for h in hh:
        s_ref[h] = s0[h] * gc_s[:, sl[h]] + _bdot(uv_s[:, sl[h]], bk_s[:, sl[h]], _TN)


def rwkv_scan(lay, r, v, kk, wl, kd, a):
    n_tok, d = r.shape
    step = RWKV_CHUNK * RWKV_STEP_CHUNKS
    n = lay.n_chunks(step)

    def rb(z, b, i):
        sc = jnp.where(z == 0, lay.seq_chunk(step, False, i), lay.seq_chunk(step, True, i))
        return lay.row_block(step, b, sc)

    shared = pl.BlockSpec((step, d), lambda z, b, i: (rb(z, b, i), 0))
    perdir = pl.BlockSpec((None, step, d), lambda z, b, i: (z, rb(z, b, i), 0))
    c = RWKV_CHUNK
    bf = lambda rows: pltpu.VMEM((rows, d), BF16)
    return pl.pallas_call(
        _rwkv_chunk_kernel,
        grid=(2, lay.batch, n),
        in_specs=[shared, shared, shared, perdir, perdir, perdir],
        out_specs=perdir,
        out_shape=jax.ShapeDtypeStruct((2, n_tok, d), F32),
        scratch_shapes=[pltpu.VMEM((RWKV_HEADS, RWKV_HEAD, RWKV_HEAD), F32),
                        bf(2 * c), bf(2 * c), bf(2 * c), bf(2 * c), pltpu.VMEM((1, d), F32)],
        compiler_params=_params("arbitrary", "arbitrary", "arbitrary"),
        name="rwkv_scan",
    )(r, v, kk, wl, kd, a)


def _rwkv_pre_kernel(lay, tm, prev_ref, h_ref, next_ref, mix_ref, wr_ref, wk_ref, wv_ref, w1_ref, a1_ref, g1_ref,
                     w2_ref, a2_ref, g2_ref, w0_ref, a0_ref, kk_ref, ka_ref, rk_ref, seg_ref,
                     r_out, v_out, kk_out, wl_out, kd_out, a_out, g_out, bonus_out, xe_s):
    j = pl.program_id(0)
    r0 = j * tm
    in_ctx = r0 < lay.ctx_tok
    seq_len = jnp.where(in_ctx, lay.n_ctx, lay.seq)
    off = jnp.where(in_ctx, r0, r0 - lay.ctx_tok) % seq_len
    hal = SEQ_HALO
    xe_s[0:hal, :] = jnp.where(off == 0, 0.0, prev_ref[...].astype(F32))
    xe_s[hal:hal + tm, :] = h_ref[...].astype(F32)
    xe_s[hal + tm:, :] = jnp.where(off + tm == seq_len, 0.0, next_ref[...].astype(F32))
    h = xe_s[pl.ds(hal, tm), :]
    xx = 0.5 * (xe_s[pl.ds(hal - 1, tm), :] + xe_s[pl.ds(hal + 1, tm), :]) - h
    xm = lambda i: (h + xx * mix_ref[i:i + 1, :]).astype(BF16)
    r = _bdot(xm(0), wr_ref[...])
    k = _bdot(xm(1), wk_ref[...])
    v = _bdot(xm(2), wv_ref[...])
    hw = jnp.tanh(_bdot(xm(3), w1_ref[...]))
    ha = _bdot(xm(4), a1_ref[...])
    hg = _sigmoid(_bdot(xm(5), g1_ref[...]))
    seg = seg_ref[...]
    kx = k * kk_ref[...]
    r_out[...] = r.astype(r_out.dtype)
    v_out[...] = v.astype(v_out.dtype)
    kk_out[...] = (kx * lax.rsqrt(_bdot(kx * kx, seg) + L2_EPS)).astype(kk_out.dtype)
    g_out[...] = _bdot(hg, g2_ref[...]).astype(g_out.dtype)
    lw = w2_ref.shape[1]
    rr = r * rk_ref[...]
    bsum = None
    for z in range(2):
        wl_out[z] = w0_ref[z:z + 1, :] + _bdot(hw[:, z * lw:(z + 1) * lw], w2_ref[z])
        a = _sigmoid(a0_ref[z:z + 1, :] + _bdot(ha[:, z * lw:(z + 1) * lw], a2_ref[z]))
        kd = k * (1.0 + (a - 1.0) * ka_ref[...])
        a_out[z] = a.astype(a_out.dtype)
        kd_out[z] = kd.astype(kd_out.dtype)
        bsum = rr * kd if bsum is None else bsum + rr * kd
    bonus_out[...] = (_bdot(bsum, seg) * v).astype(bonus_out.dtype)


def _rwkv_post_kernel(o_ref, g_ref, bonus_ref, lnx_ref, seg_ref, a_out):
    seg = seg_ref[...]
    o = o_ref[0] + o_ref[1]
    hi = o.astype(BF16)
    lo = o - hi.astype(F32)
    inv_n = 1.0 / RWKV_HEAD
    oc = o - (_bdot(hi, seg) + _bdot(lo, seg)) * inv_n
    var = _bdot(oc * oc, seg) * inv_n
    on = oc * lax.rsqrt(var + LNX_EPS) * lnx_ref[...] + bonus_ref[...].astype(F32)
    a_out[...] = (on * g_ref[...].astype(F32)).astype(a_out.dtype)


def rwkv7_mix(lay, h, mix, w_rkv, w0, w1, w2, a0, a1, a2, g1, g2, k_k, k_a, r_k, lnx_g):
    n_tok, d = h.shape
    bw = lambda w: w.astype(BF16)
    tm = math.gcd(256, lay.n_ctx, lay.seq)
    hb = tm // SEQ_HALO
    last_hb = n_tok // SEQ_HALO - 1
    head_of = jnp.arange(d) // RWKV_HEAD
    seg = (head_of[:, None] == head_of[None, :]).astype(BF16)
    full = lambda a: pl.BlockSpec(a.shape, lambda j: (0,) * a.ndim)
    row = lambda a: a.reshape(1, d)
    consts = (mix, bw(w_rkv[0]), bw(w_rkv[1]), bw(w_rkv[2]), bw(jnp.concatenate([w1[0], w1[1]], -1)),
              bw(jnp.concatenate([a1[0], a1[1]], -1)), bw(g1), bw(w2), bw(a2), bw(g2), w0, a0,
              row(k_k), row(k_a), row(r_k), seg)
    tok = pl.BlockSpec((tm, d), lambda j: (j, 0))
    tok2 = pl.BlockSpec((2, tm, d), lambda j: (0, j, 0))
    one = lambda dt: jax.ShapeDtypeStruct((n_tok, d), dt)
    two = lambda dt: jax.ShapeDtypeStruct((2, n_tok, d), dt)
    r, v, kk, wl, kd, a, g, bonus = pl.pallas_call(
        functools.partial(_rwkv_pre_kernel, lay, tm),
        grid=(n_tok // tm,),
        in_specs=[pl.BlockSpec((SEQ_HALO, d), lambda j: (jnp.maximum(j * hb - 1, 0), 0)), tok,
                  pl.BlockSpec((SEQ_HALO, d), lambda j: (jnp.minimum((j + 1) * hb, last_hb), 0))]
                 + [full(a) for a in consts],
        out_specs=[tok, tok, tok, tok2, tok2, tok2, tok, tok],
        out_shape=[one(BF16), one(BF16), one(BF16), two(F32), two(BF16), two(BF16), one(BF16), one(BF16)],
        scratch_shapes=[pltpu.VMEM((tm + 2 * SEQ_HALO, d), F32)],
        compiler_params=_params("arbitrary"), name="rwkv_pre",
    )(h, h, h, *consts)
    o = rwkv_scan(lay, r, v, kk, wl, kd, a)
    return pl.pallas_call(
        _rwkv_post_kernel,
        grid=(n_tok // tm,),
        in_specs=[tok2, tok, tok, full(row(lnx_g)), full(seg)],
        out_specs=tok,
        out_shape=one(BF16),
        compiler_params=_params("arbitrary"), name="rwkv_post",
    )(o, g, bonus, row(lnx_g), seg)


def _moe_kernel(be_ref, rows_ref, x_ref, wg_ref, wu_ref, wd_ref, gate_ref, o_ref, acc_s):
    j = pl.program_id(0)
    f = pl.program_id(1)

    @pl.when(f == 0)
    def _():
        acc_s[...] = jnp.zeros_like(acc_s)

    @pl.when(rows_ref[j] > 0)
    def _():
        x = x_ref[...]
        g = _bdot(x, wg_ref[...])
        u = _bdot(x, wu_ref[...])
        acc_s[...] += _bdot(g * _sigmoid(g) * u, wd_ref[...])

    @pl.when(f == pl.num_programs(1) - 1)
    def _():
        o_ref[...] = (acc_s[...] * gate_ref[...]).astype(o_ref.dtype)


def moe_experts(xb, block_e, block_rows, w_gu, w_down, layer, slot_gate):
    n_slots, d = xb.shape
    bm = MOE_ROWS
    nf = MOE_F_CHUNKS
    tf = w_down.shape[2] // nf
    return pl.pallas_call(
        _moe_kernel,
        grid_spec=pltpu.PrefetchScalarGridSpec(
            num_scalar_prefetch=2,
            grid=(n_slots // bm, nf),
            in_specs=[pl.BlockSpec((bm, d), lambda j, f, be, nb: (j, 0)),
                      pl.BlockSpec((None, None, d, tf), lambda j, f, be, nb: (layer, be[j], 0, f)),
                      pl.BlockSpec((None, None, d, tf), lambda j, f, be, nb: (layer, be[j], 0, f + nf)),
                      pl.BlockSpec((None, None, tf, d), lambda j, f, be, nb: (layer, be[j], f, 0)),
                      pl.BlockSpec((bm, 1), lambda j, f, be, nb: (j, 0))],
            out_specs=pl.BlockSpec((bm, d), lambda j, f, be, nb: (j, 0)),
            scratch_shapes=[pltpu.VMEM((bm, d), F32)]),
        out_shape=jax.ShapeDtypeStruct((n_slots, d), BF16),
        compiler_params=_params("arbitrary", "arbitrary"),
        name="moe_experts",
    )(block_e, block_rows, xb, w_gu, w_gu, w_down, slot_gate)


def moe_swiglu(h, logits, w_gu, w_down, layer):
    n, d = h.shape
    lanes = jnp.arange(N_EXPERTS, dtype=jnp.int32)
    e1 = jnp.argmax(logits, axis=-1).astype(jnp.int32)
    rest = jnp.where(lanes == e1[:, None], -jnp.inf, logits)
    e2 = jnp.argmax(rest, axis=-1).astype(jnp.int32)
    top_logit = jnp.stack([jnp.max(logits, axis=-1), jnp.max(rest, axis=-1)], axis=-1)
    top_e = jnp.stack([e1, e2], axis=-1)
    gate = jax.nn.softmax(top_logit, axis=-1)
    flat_e = top_e.reshape(-1).astype(jnp.int32)
    order = jnp.argsort(flat_e).astype(jnp.int32)
    onehot = (flat_e[:, None] == jnp.arange(N_EXPERTS, dtype=jnp.int32)).astype(jnp.int32)
    seen = jnp.cumsum(onehot, axis=0)
    counts = seen[-1]
    padded = (counts + MOE_ROWS - 1) // MOE_ROWS * MOE_ROWS
    start = jnp.cumsum(counts) - counts
    pend = jnp.cumsum(padded)
    pstart = pend - padded
    tok_slot = jnp.sum(onehot * (seen - 1 + pstart[None, :]), axis=1).reshape(n, TOP_K)
    n_slots = (n * TOP_K + MOE_ROWS - 1) // MOE_ROWS * MOE_ROWS + N_EXPERTS * MOE_ROWS
    n_blocks = n_slots // MOE_ROWS
    blocks = jnp.arange(n_blocks, dtype=jnp.int32)
    block_e = jnp.minimum(jnp.sum(blocks[:, None] * MOE_ROWS >= pend[None, :], axis=1),
                          N_EXPERTS - 1).astype(jnp.int32)
    block_off = blocks * MOE_ROWS - pstart[block_e]
    block_rows = jnp.clip(counts[block_e] - block_off, 0, MOE_ROWS).astype(jnp.int32)
    slot_off = block_off[:, None] + jnp.arange(MOE_ROWS, dtype=jnp.int32)[None, :]
    slot_valid = (slot_off < counts[block_e][:, None]).reshape(-1)
    slot_asg = order[jnp.clip(start[block_e][:, None] + slot_off, 0, n * TOP_K - 1).reshape(-1)]
    slot_tok = jnp.where(slot_valid, slot_asg // TOP_K, jnp.arange(n_slots, dtype=jnp.int32) % n)
    slot_gate = jnp.where(slot_valid, gate.reshape(-1)[slot_asg], 0.0)
    yb = moe_experts(h[slot_tok], block_e, block_rows, w_gu, w_down, layer, slot_gate[:, None])
    return yb[tok_slot[:, 0]], yb[tok_slot[:, 1]]


def kernel(x, c, ctx, c_ctx, mod_w, mod_b, ln_g, ln_b,
           ret_w_in, ret_decay, ret_gn_g, ret_w_out,
           dn_w_in, dn_conv_w, dn_a_log, dn_dt_bias, dn_norm_g, dn_w_out,
           rk_mix, rk_w_rkv, rk_w0, rk_w1, rk_w2, rk_a0, rk_a1, rk_a2, rk_g1, rk_g2,
           rk_k_k, rk_k_a, rk_r_k, rk_lnx_g, rk_w_out,
           ffn_w_gu, ffn_w_down, moe_router, moe_w_gu, moe_w_down):
    bsz, t, d = x.shape
    n_ctx = ctx.shape[1]
    lay = Layout(bsz, n_ctx, t)
    bw = lambda w: w.astype(BF16)
    s_rows = jax.nn.silu(jnp.concatenate([c_ctx[None], c], 0))
    s_pad = jnp.zeros((8, d), F32).at[:1 + bsz].set(s_rows)
    mods = modulation_rows(s_pad, mod_w, mod_b)[:, :1 + bsz].reshape(DEPTH, 1 + bsz, 6, 1, d)
    mod = lambda i, k: mods[i, :, k]
    cos_t, sin_t = rope_tables(lay)
    xs = jnp.concatenate([ctx.reshape(-1, d), x.reshape(-1, d)], 0)
    h = modulate(lay, xs, mod(0, 1), mod(0, 0))
    for i in range(DEPTH):
        last = i == DEPTH - 1
        kind, j = i % N_MIXERS, i // N_MIXERS
        if kind == 0:
            p = proj(h, bw(ret_w_in[j]), BF16, PROJ_COLS)
            log_gamma = jax.nn.log_sigmoid(ret_decay[j].astype(F32))
            a = retention_mix(lay, p, log_gamma, ret_gn_g[j], cos_t, sin_t)
            w_out = ret_w_out[j]
        elif kind == 1:
            n_main = 2 * DN_QK_W + 2 * DN_V_W
            p = proj(h, bw(dn_w_in[j][:, :n_main]), BF16, PROJ_COLS)
            w_ab = bw(dn_w_in[j][:, n_main:])
            ab = proj(h, w_ab, F32, w_ab.shape[1])
            qkv = dn_prep(lay, p, dn_conv_w[j])
            a = deltanet_mix(lay, qkv, p, ab, dn_a_log[j], dn_dt_bias[j], dn_norm_g[j])
            w_out = dn_w_out[j]
        else:
            a = rwkv7_mix(lay, h, rk_mix[j], rk_w_rkv[j], rk_w0[j], rk_w1[j], rk_w2[j], rk_a0[j], rk_a1[j],
                          rk_a2[j], rk_g1[j], rk_g2[j], rk_k_k[j], rk_k_a[j], rk_r_k[j], rk_lnx_g[j])
            w_out = rk_w_out[j]
        router = bw(moe_router[i // 2]) if i % 2 == 1 else None
        xs, h, *logits = out_ln(lay, a, bw(w_out), xs, mod(i, 2), ln_g[i, 0], ln_b[i, 0], mod(i, 4), mod(i, 3),
                                router=router)
        nxt = (i + 1) % DEPTH
        if i % 2 == 0:
            hm = swiglu_in(h, bw(ffn_w_gu[i // 2]), FFN_COLS)
            xs, h = out_ln(lay, hm, bw(ffn_w_down[i // 2]), xs, mod(i, 5), ln_g[i, 1], ln_b[i, 1],
                           mod(nxt, 1), mod(nxt, 0), latents_only=last)
        else:
            f = moe_swiglu(h, logits[0], moe_w_gu, moe_w_down, i // 2)
            xs, h = out_ln(lay, f, None, xs, mod(i, 5), ln_g[i, 1], ln_b[i, 1], mod(nxt, 1), mod(nxt, 0),
                           latents_only=last)
    return xs.reshape(bsz, t, d)
```

```python
import math, functools
import jax
import jax.numpy as jnp
from jax import lax
import numpy as np
from jax.experimental import pallas as pl
from jax.experimental.pallas import tpu as pltpu

D_MODEL = 1024
DEPTH = 4
GRID_W = 64
N_MIXERS = 3
ALPHA = (2 * DEPTH) ** 0.25
LN_EPS = 1e-5
GN_EPS = 1e-5
RMS_EPS = 1e-6
LNX_EPS = 64e-5
L2_EPS = 1e-6

RET_HEADS = 4
RET_DK = D_MODEL // RET_HEADS
RET_DV = 2 * RET_DK
RET_CHUNK = 256
ROPE_BASE = 10000.0

DN_QK_HEADS = 8
DN_V_HEADS = 16
DN_HEAD_DIM = 128
DN_CHUNK = 64
DN_STEP_CHUNKS = 4
DN_CONV = 5
DN_QK_W = DN_QK_HEADS * DN_HEAD_DIM
DN_V_W = DN_V_HEADS * DN_HEAD_DIM
SEQ_HALO = 16

RWKV_HEAD = 64
RWKV_HEADS = D_MODEL // RWKV_HEAD
RWKV_CHUNK = 64
RWKV_STEP_CHUNKS = 4

FFN_DIM = 2816
N_EXPERTS = 8
TOP_K = 2
EXPERT_DIM = 3584
MOE_ROWS = 1024
MOE_F_CHUNKS = 7
PROJ_COLS = 2048
FFN_COLS = FFN_DIM // 2

ROW_TILE = 512
VMEM_LIMIT = 48 * 1024 * 1024

BF16 = jnp.bfloat16
F32 = jnp.float32
_NT = (((1,), (1,)), ((), ()))
_TN = (((0,), (0,)), ((), ()))


def _bdot(x, y, dims=None):
    x = x.astype(BF16)
    y = y.astype(BF16)
    if dims is None:
        return jnp.dot(x, y, preferred_element_type=F32)
    return lax.dot_general(x, y, dims, preferred_element_type=F32)


def _hdot(x, y):
    return jnp.dot(x, y, precision=lax.Precision.HIGHEST, preferred_element_type=F32)


def _sigmoid(x):
    return 1.0 / (1.0 + jnp.exp(-x))


def _softplus(x):
    return jnp.maximum(x, 0.0) + jnp.log(1.0 + jnp.exp(-jnp.abs(x)))


def _params(*sem):
    return pltpu.CompilerParams(dimension_semantics=sem, vmem_limit_bytes=VMEM_LIMIT)


class Layout:
    def __init__(self, batch, n_ctx, seq):
        self.batch, self.n_ctx, self.seq = batch, n_ctx, seq
        self.ctx_tok = batch * n_ctx
        self.n_tok = self.ctx_tok + batch * seq
        self.row_tile = math.gcd(ROW_TILE, n_ctx * batch, seq)

    def mod_index(self, tile, j):
        r0 = j * tile
        return jnp.where(r0 < self.ctx_tok, 0, 1 + (r0 - self.ctx_tok) // self.seq)

    def seq_chunk(self, chunk, backward, i):
        nc, nl = self.n_ctx // chunk, self.seq // chunk
        if not backward:
            return i
        return jnp.where(i < nc, nc - 1 - i, 2 * nc + nl - 1 - i)

    def row_block(self, chunk, b, sc):
        nc, nl = self.n_ctx // chunk, self.seq // chunk
        return jnp.where(sc < nc, b * nc + sc, self.batch * nc + b * nl + sc - nc)

    def n_chunks(self, chunk):
        return (self.n_ctx + self.seq) // chunk


def _proj_kernel(h_ref, w_ref, o_ref):
    o_ref[...] = _bdot(h_ref[...], w_ref[...]).astype(o_ref.dtype)


def proj(h, w, out_dtype, tn):
    n_tok, k = h.shape
    n = w.shape[1]
    tm = math.gcd(1024, n_tok)
    return pl.pallas_call(
        _proj_kernel,
        grid=(n // tn, n_tok // tm),
        in_specs=[pl.BlockSpec((tm, k), lambda c, j: (j, 0)), pl.BlockSpec((k, tn), lambda c, j: (0, c))],
        out_specs=pl.BlockSpec((tm, tn), lambda c, j: (j, c)),
        out_shape=jax.ShapeDtypeStruct((n_tok, n), out_dtype),
        compiler_params=_params("arbitrary", "arbitrary"),
        name="proj",
    )(h, w)


def _swiglu_in_kernel(h_ref, wg_ref, wu_ref, o_ref):
    h = h_ref[...]
    g = _bdot(h, wg_ref[...])
    u = _bdot(h, wu_ref[...])
    o_ref[...] = (g * _sigmoid(g) * u).astype(o_ref.dtype)


def swiglu_in(h, w_gu, tn):
    n_tok, k = h.shape
    f = w_gu.shape[1] // 2
    tm = math.gcd(1024, n_tok)
    nf = f // tn
    return pl.pallas_call(
        _swiglu_in_kernel,
        grid=(nf, n_tok // tm),
        in_specs=[pl.BlockSpec((tm, k), lambda c, j: (j, 0)),
                  pl.BlockSpec((k, tn), lambda c, j: (0, c)),
                  pl.BlockSpec((k, tn), lambda c, j: (0, c + nf))],
        out_specs=pl.BlockSpec((tm, tn), lambda c, j: (j, c)),
        out_shape=jax.ShapeDtypeStruct((n_tok, f), BF16),
        compiler_params=_params("arbitrary", "arbitrary"),
        name="swiglu_in",
    )(h, w_gu, w_gu)


def _deepnorm_epilogue(x, f, ga_ref, g_ref, b_ref, sc_ref, sh_ref, x_out, h_out):
    y = ALPHA * x + (1.0 + ga_ref[...]) * f
    mu = jnp.mean(y, -1, keepdims=True)
    yc = y - mu
    var = jnp.mean(yc * yc, -1, keepdims=True)
    xn = yc * lax.rsqrt(var + LN_EPS) * g_ref[...] + b_ref[...]
    x_out[...] = xn
    h = (xn * (1.0 + sc_ref[...]) + sh_ref[...]).astype(h_out.dtype)
    h_out[...] = h
    return h


def _out_ln_kernel(a_ref, w_ref, x_ref, ga_ref, g_ref, b_ref, sc_ref, sh_ref, x_out, h_out):
    _deepnorm_epilogue(x_ref[...], _bdot(a_ref[...], w_ref[...]), ga_ref, g_ref, b_ref, sc_ref, sh_ref, x_out, h_out)


def _out_ln_router_kernel(a_ref, w_ref, x_ref, ga_ref, g_ref, b_ref, sc_ref, sh_ref, wr_ref, x_out, h_out, lg_out):
    h = _deepnorm_epilogue(x_ref[...], _bdot(a_ref[...], w_ref[...]), ga_ref, g_ref, b_ref, sc_ref, sh_ref,
                           x_out, h_out)
    lg_out[...] = _bdot(h, wr_ref[...])


def _resid_ln_kernel(f1_ref, f2_ref, x_ref, ga_ref, g_ref, b_ref, sc_ref, sh_ref, x_out, h_out):
    f = f1_ref[...].astype(F32) + f2_ref[...].astype(F32)
    _deepnorm_epilogue(x_ref[...], f, ga_ref, g_ref, b_ref, sc_ref, sh_ref, x_out, h_out)


def out_ln(lay, a, w, x, gate, ln_g, ln_b, sc_next, sh_next, latents_only=False, router=None):
    n_tok, d = x.shape
    tm = lay.row_tile
    row = lambda j: (j, 0)
    ctx_tiles = lay.ctx_tok // tm if latents_only else 0
    x_out = pl.BlockSpec((tm, d), lambda j: (jnp.maximum(j - ctx_tiles, 0), 0))
    mod = pl.BlockSpec((None, 1, d), lambda j: (lay.mod_index(tm, j), 0, 0))
    vec = pl.BlockSpec((1, d), lambda j: (0, 0))
    tok = pl.BlockSpec((tm, d), row)
    extra, extra_specs, extra_out, extra_shape = (), [], [], []
    if w is None:
        body, lhs, lhs_specs = _resid_ln_kernel, tuple(a), [tok, tok]
    else:
        k = a.shape[1]
        body, lhs = _out_ln_kernel, (a, w)
        lhs_specs = [pl.BlockSpec((tm, k), row), pl.BlockSpec((k, d), lambda j: (0, 0))]
        if router is not None:
            ne = router.shape[1]
            body, extra = _out_ln_router_kernel, (router,)
            extra_specs = [pl.BlockSpec((d, ne), lambda j: (0, 0))]
            extra_out = [pl.BlockSpec((tm, ne), row)]
            extra_shape = [jax.ShapeDtypeStruct((n_tok, ne), F32)]
    return pl.pallas_call(
        body,
        grid=(n_tok // tm,),
        in_specs=lhs_specs + [tok, mod, vec, vec, mod, mod] + extra_specs,
        out_specs=[x_out, tok] + extra_out,
        out_shape=[jax.ShapeDtypeStruct((n_tok - ctx_tiles * tm, d), F32), jax.ShapeDtypeStruct((n_tok, d), BF16)]
                  + extra_shape,
        compiler_params=_params("arbitrary"),
        name="out_ln",
    )(*lhs, x, gate, ln_g.reshape(1, d), ln_b.reshape(1, d), sc_next, sh_next, *extra)


def _modulate_kernel(x_ref, sc_ref, sh_ref, h_out):
    h_out[...] = (x_ref[...] * (1.0 + sc_ref[...]) + sh_ref[...]).astype(h_out.dtype)


def modulate(lay, x, sc, sh):
    n_tok, d = x.shape
    tm = lay.row_tile
    mod = pl.BlockSpec((None, 1, d), lambda j: (lay.mod_index(tm, j), 0, 0))
    tok = pl.BlockSpec((tm, d), lambda j: (j, 0))
    return pl.pallas_call(
        _modulate_kernel, grid=(n_tok // tm,), in_specs=[tok, mod, mod], out_specs=tok,
        out_shape=jax.ShapeDtypeStruct((n_tok, d), BF16), compiler_params=_params("arbitrary"), name="modulate",
    )(x, sc, sh)


def _mod_kernel(s_ref, w_ref, b_ref, o_ref):
    o_ref[...] = _bdot(s_ref[...], w_ref[...]) + b_ref[...]


def modulation_rows(s, mod_w, mod_b):
    r, d = s.shape
    depth, _, n = mod_w.shape
    tn = 1024
    return pl.pallas_call(
        _mod_kernel,
        grid=(depth, n // tn),
        in_specs=[pl.BlockSpec((r, d), lambda i, c: (0, 0)),
                  pl.BlockSpec((None, d, tn), lambda i, c: (i, 0, c)),
                  pl.BlockSpec((None, 1, tn), lambda i, c: (i, 0, c))],
        out_specs=pl.BlockSpec((None, r, tn), lambda i, c: (i, 0, c)),
        out_shape=jax.ShapeDtypeStruct((depth, r, n), F32),
        compiler_params=_params("arbitrary", "arbitrary"),
        name="modulation_rows",
    )(s, mod_w, mod_b.reshape(depth, 1, n))


def _rope(x, cos, sin):
    half = x.shape[1] // 2
    parts = []
    for p in range(2):
        xs = x[:, p * half:(p + 1) * half]
        parts.append(xs * cos[:, p * half:(p + 1) * half]
                     + pltpu.roll(xs, half // 2, axis=1) * sin[:, p * half:(p + 1) * half])
    return jnp.concatenate(parts, axis=1)


def _ret_heads(q_ref, k_ref, v_ref, cos_ref, sin_ref):
    cos, sin = cos_ref[...], sin_ref[...]
    hs = range(RET_HEADS)
    q = [_rope(q_ref[:, h * RET_DK:(h + 1) * RET_DK].astype(F32), cos, sin) for h in hs]
    k = [_rope(k_ref[:, h * RET_DK:(h + 1) * RET_DK].astype(F32), cos, sin) for h in hs]
    v = [v_ref[:, h * RET_DV:(h + 1) * RET_DV] for h in hs]
    return hs, q, k, v


def _ret_fwd_kernel(cd_ref, q_ref, k_ref, v_ref, cos_ref, sin_ref, dec_ref, rd_ref, o_ref, s_ref):
    i = pl.program_id(1)

    @pl.when(i == 0)
    def _():
        s_ref[...] = jnp.zeros_like(s_ref)

    hs, q, k, v = _ret_heads(q_ref, k_ref, v_ref, cos_ref, sin_ref)
    scores = [(_bdot(q[h], k[h], _NT) * dec_ref[h]).astype(BF16) for h in hs]
    s = [s_ref[h] for h in hs]
    qd = [(q[h] * rd_ref[0, h]).astype(BF16) for h in hs]
    kd = [(k[h] * rd_ref[1, h]).astype(BF16) for h in hs]
    for h in hs:
        o_ref[:, h * RET_DV:(h + 1) * RET_DV] = _bdot(scores[h], v[h]) + _bdot(qd[h], s[h])
    for h in hs:
        s_ref[h] = s[h] * cd_ref[0, h] + _bdot(kd[h], v[h], _TN)


def _ret_bwd_kernel(cd_ref, q_ref, k_ref, v_ref, g_ref, cos_ref, sin_ref, rd_ref, op_ref, gn_ref, o_ref, s_ref):
    i = pl.program_id(1)

    @pl.when(i == 0)
    def _():
        s_ref[...] = jnp.zeros_like(s_ref)

    hs, q, k, v = _ret_heads(q_ref, k_ref, v_ref, cos_ref, sin_ref)
    s = [s_ref[h] for h in hs]
    qd = [(q[h] * rd_ref[0, h]).astype(BF16) for h in hs]
    kd = [(k[h] * rd_ref[1, h]).astype(BF16) for h in hs]
    o = [op_ref[:, h * RET_DV:(h + 1) * RET_DV] + _bdot(qd[h], s[h]) for h in hs]
    for h in hs:
        s_ref[h] = s[h] * cd_ref[1, h] + _bdot(kd[h], v[h], _TN)
    for h in hs:
        hv = slice(h * RET_DV, (h + 1) * RET_DV)
        mu = jnp.mean(o[h], -1, keepdims=True)
        oc = o[h] - mu
        var = jnp.mean(oc * oc, -1, keepdims=True)
        g = g_ref[:, hv].astype(F32)
        o_ref[:, hv] = (g * _sigmoid(g) * (oc * lax.rsqrt(var + GN_EPS) * gn_ref[:, hv])).astype(o_ref.dtype)


def retention_mix(lay, p, log_gamma, gn_g, cos_t, sin_t):
    c = RET_CHUNK
    n = lay.n_chunks(c)
    n_tok = p.shape[0]
    hk = RET_HEADS * RET_DK
    hv = RET_HEADS * RET_DV

    def specs(backward):
        sc = lambda i: lay.seq_chunk(c, backward, i)
        blk = lambda w, off: pl.BlockSpec((c, w), lambda b, i: (lay.row_block(c, b, sc(i)), off))
        tab = pl.BlockSpec((c, RET_DK), lambda b, i: (sc(i), 0))
        return blk, tab

    pos = jnp.arange(c, dtype=F32)
    lag = pos[:, None] - pos[None, :]
    lg_f, lg_b = log_gamma[0][:, None, None], log_gamma[1][:, None, None]
    k_scale = RET_DK ** -0.5
    dec = (jnp.where(lag >= 0, jnp.exp(jnp.maximum(lag, 0.0) * lg_f), 0.0)
           + jnp.where(lag <= 0, jnp.exp(jnp.maximum(-lag, 0.0) * lg_b), 0.0)) * k_scale
    rows = lambda e, lg: jnp.exp(e[None, :] * lg[:, None])[..., None]
    rd_f = jnp.stack([rows(pos + 1.0, log_gamma[0]), rows(c - 1.0 - pos, log_gamma[0]) * k_scale])
    rd_b = jnp.stack([rows(c - pos, log_gamma[1]), rows(pos, log_gamma[1]) * k_scale])
    cd = jnp.exp(c * log_gamma)

    smem = pl.BlockSpec(memory_space=pltpu.SMEM)
    full = lambda a: pl.BlockSpec(a.shape, lambda b, i: (0,) * a.ndim)
    grid = (lay.batch, n)
    state = [pltpu.VMEM((RET_HEADS, RET_DK, RET_DV), F32)]
    blk, tab = specs(False)
    o_part = pl.pallas_call(
        _ret_fwd_kernel, grid=grid,
        in_specs=[smem, blk(hk, 0), blk(hk, 1), blk(hv, 1), tab, tab, full(dec), full(rd_f)],
        out_specs=blk(hv, 0),
        out_shape=jax.ShapeDtypeStruct((n_tok, hv), F32),
        scratch_shapes=state, compiler_params=_params("arbitrary", "arbitrary"), name="ret_fwd",
    )(cd, p, p, p, cos_t, sin_t, dec, rd_f)
    blk, tab = specs(True)
    return pl.pallas_call(
        _ret_bwd_kernel, grid=grid,
        in_specs=[smem, blk(hk, 0), blk(hk, 1), blk(hv, 1), blk(hv, 2), tab, tab, full(rd_b),
                  blk(hv, 0), pl.BlockSpec((1, hv), lambda b, i: (0, 0))],
        out_specs=blk(hv, 0),
        out_shape=jax.ShapeDtypeStruct((n_tok, hv), BF16),
        scratch_shapes=state, compiler_params=_params("arbitrary", "arbitrary"), name="ret_bwd",
    )(cd, p, p, p, p, cos_t, sin_t, rd_b, o_part, gn_g.reshape(1, hv))


def rope_tables(lay):
    quarter = RET_DK // 4
    t = jnp.arange(lay.seq)
    inv = ROPE_BASE ** (-jnp.arange(quarter, dtype=F32) / quarter)
    ang_r = (t // GRID_W).astype(F32)[:, None] * inv
    ang_c = (t % GRID_W).astype(F32)[:, None] * inv
    cos = jnp.concatenate([jnp.cos(ang_r)] * 2 + [jnp.cos(ang_c)] * 2, -1)
    sin = jnp.concatenate([-jnp.sin(ang_r), jnp.sin(ang_r), -jnp.sin(ang_c), jnp.sin(ang_c)], -1)
    cos = jnp.concatenate([jnp.ones((lay.n_ctx, RET_DK), F32), cos], 0)
    sin = jnp.concatenate([jnp.zeros((lay.n_ctx, RET_DK), F32), sin], 0)
    return cos, sin


def _dn_prep_kernel(lay, tm, prev_ref, x_ref, next_ref, w_ref, o_ref, xe_s):
    j = pl.program_id(0)
    ct = pl.program_id(1)
    r0 = j * tm
    in_ctx = r0 < lay.ctx_tok
    seq_len = jnp.where(in_ctx, lay.n_ctx, lay.seq)
    off = jnp.where(in_ctx, r0, r0 - lay.ctx_tok) % seq_len
    first = off == 0
    last = off + tm == seq_len
    hal = SEQ_HALO
    xe_s[0:hal, :] = jnp.where(first, 0.0, prev_ref[...].astype(F32))
    xe_s[hal:hal + tm, :] = x_ref[...].astype(F32)
    xe_s[hal + tm:, :] = jnp.where(last, 0.0, next_ref[...].astype(F32))
    pad = (DN_CONV - 1) // 2
    acc = xe_s[pl.ds(hal - pad, tm), :] * w_ref[0:1, :]
    for d in range(1, DN_CONV):
        acc = acc + xe_s[pl.ds(hal - pad + d, tm), :] * w_ref[d:d + 1, :]
    y = acc * _sigmoid(acc)
    n_qk_tiles = 2 * DN_QK_W // x_ref.shape[1]

    @pl.when(ct >= n_qk_tiles)
    def _():
        o_ref[...] = y.astype(o_ref.dtype)

    @pl.when(ct < n_qk_tiles)
    def _():
        for s in range(x_ref.shape[1] // DN_HEAD_DIM):
            is_q = ct * x_ref.shape[1] + s * DN_HEAD_DIM < DN_QK_W
            scale = jnp.where(is_q, DN_HEAD_DIM ** -0.5, 1.0)
            ys = y[:, s * DN_HEAD_DIM:(s + 1) * DN_HEAD_DIM]
            inv = lax.rsqrt(jnp.sum(ys * ys, -1, keepdims=True) + L2_EPS) * scale
            o_ref[:, s * DN_HEAD_DIM:(s + 1) * DN_HEAD_DIM] = (ys * inv).astype(o_ref.dtype)


def dn_prep(lay, p, conv_w):
    n_tok = p.shape[0]
    w = 2 * DN_QK_W + DN_V_W
    tm = math.gcd(256, lay.n_ctx, lay.seq)
    tc = 2 * DN_QK_W
    hb = tm // SEQ_HALO
    last_hb = n_tok // SEQ_HALO - 1
    return pl.pallas_call(
        functools.partial(_dn_prep_kernel, lay, tm),
        grid=(n_tok // tm, w // tc),
        in_specs=[pl.BlockSpec((SEQ_HALO, tc), lambda j, c: (jnp.maximum(j * hb - 1, 0), c)),
                  pl.BlockSpec((tm, tc), lambda j, c: (j, c)),
                  pl.BlockSpec((SEQ_HALO, tc), lambda j, c: (jnp.minimum((j + 1) * hb, last_hb), c)),
                  pl.BlockSpec((DN_CONV, tc), lambda j, c: (0, c))],
        out_specs=pl.BlockSpec((tm, tc), lambda j, c: (j, c)),
        out_shape=jax.ShapeDtypeStruct((n_tok, w), BF16),
        scratch_shapes=[pltpu.VMEM((tm + 2 * SEQ_HALO, tc), F32)],
        compiler_params=_params("arbitrary", "arbitrary"), name="dn_prep",
    )(p, p, p, conv_w)


def _dn_scan_kernel(backward, q_ref, k_ref, v_ref, ab_ref, na_row, dt_row, *rest):
    s_ref = rest[-1]

    @pl.when(pl.program_id(1) == 0)
    def _():
        s_ref[...] = jnp.zeros_like(s_ref)

    for j in range(DN_STEP_CHUNKS):
        sub = DN_STEP_CHUNKS - 1 - j if backward else j
        rows = lambda r: r.at[pl.ds(sub * DN_CHUNK, DN_CHUNK)]
        if backward:
            of_ref, z_ref, ng_ref, o_ref, _ = rest
            tail = (rows(of_ref), rows(z_ref), ng_ref, rows(o_ref), s_ref)
        else:
            tail = (rows(rest[0]), s_ref)
        _dn_chunk(backward, rows(q_ref), rows(k_ref), rows(v_ref), rows(ab_ref), na_row, dt_row, *tail)


def _dn_chunk(backward, q_ref, k_ref, v_ref, ab_ref, na_row, dt_row, *rest):
    if backward:
        of_ref, z_ref, ng_ref, o_ref, s_ref = rest
    else:
        o_ref, s_ref = rest
    c = q_ref.shape[0]
    z = 1 if backward else 0
    nh = DN_V_HEADS
    hd = DN_HEAD_DIM
    rep = DN_V_HEADS // DN_QK_HEADS

    row = lax.broadcasted_iota(jnp.int32, (c, c), 0)
    col = lax.broadcasted_iota(jnp.int32, (c, c), 1)
    lag = (col - row) if backward else (row - col)
    incl = lag >= 0
    strict = lag > 0
    eye = (row == col).astype(F32)
    ab = ab_ref[...]
    g_cols = na_row[...] * _softplus(ab + dt_row[...])
    beta_cols = _sigmoid(ab)
    gc_cols = _hdot(incl.astype(F32), g_cols)
    gc_rows = lax.dot_general(gc_cols, eye, _TN, precision=lax.Precision.HIGHEST,
                              preferred_element_type=F32)
    gend = jnp.sum(g_cols, axis=0, keepdims=True)
    heads = range(nh)
    gi = [z * 2 * nh + h for h in heads]
    bi = [z * 2 * nh + nh + h for h in heads]
    gcc = [gc_cols[:, gi[h]:gi[h] + 1] for h in heads]
    dec = [jnp.where(incl, jnp.exp(jnp.minimum(gcc[h] - gc_rows[gi[h]:gi[h] + 1, :], 0.0)), 0.0) for h in heads]
    beta = [beta_cols[:, bi[h]:bi[h] + 1] for h in heads]
    egc = [jnp.exp(gcc[h]) for h in heads]
    eend = [jnp.exp(gend[:, gi[h]:gi[h] + 1] - gcc[h]) for h in heads]
    tail = [jnp.exp(gend[:, gi[h]:gi[h] + 1]) for h in heads]
    qs = [q_ref[:, j * hd:(j + 1) * hd] for j in range(DN_QK_HEADS)]
    ks = [k_ref[:, j * hd:(j + 1) * hd] for j in range(DN_QK_HEADS)]
    vs = [v_ref[:, h * hd:(h + 1) * hd].astype(F32) for h in heads]
    kq = [_bdot(jnp.concatenate([ks[j], qs[j]], axis=0), ks[j], _NT) for j in range(DN_QK_HEADS)]
    neg_a = [jnp.where(strict, -(kq[h // rep][:c] * beta[h] * dec[h]), 0.0) for h in heads]
    qkd = [(kq[h // rep][c:] * dec[h]).astype(BF16) for h in heads]
    tm = _unit_tri_inverse(neg_a, eye, heads)
    kf = [ks[h // rep].astype(F32) for h in heads]
    rhs = [jnp.concatenate([vs[h] * beta[h], kf[h] * (beta[h] * egc[h])], axis=1) for h in heads]
    uw = [_bdot(tm[h], rhs[h]) for h in heads]
    s0 = [s_ref[h] for h in heads]
    lhs = [jnp.concatenate([uw[h][:, hd:], qs[h // rep].astype(F32) * egc[h]], axis=0) for h in heads]
    ws_qs = [_bdot(lhs[h], s0[h]) for h in heads]
    v_new = [(uw[h][:, :hd] - ws_qs[h][:c]).astype(BF16) for h in heads]
    o = [ws_qs[h][c:] + _bdot(qkd[h], v_new[h]) for h in heads]
    for h in heads:
        s_ref[h] = s0[h] * tail[h] + _bdot(kf[h] * eend[h], v_new[h], _TN)
    if not backward:
        for h in heads:
            o_ref[:, h * hd:(h + 1) * hd] = o[h]
    else:
        for h in heads:
            ot = o[h] + of_ref[:, h * hd:(h + 1) * hd]
            on = ot * lax.rsqrt(jnp.mean(ot * ot, -1, keepdims=True) + RMS_EPS) * ng_ref[...]
            zz = z_ref[:, h * hd:(h + 1) * hd].astype(F32)
            o_ref[:, h * hd:(h + 1) * hd] = (on * (zz * _sigmoid(zz))).astype(o_ref.dtype)


def deltanet_mix(lay, qkv, p, ab, a_log, dt_bias, norm_g):
    c = DN_CHUNK * DN_STEP_CHUNKS
    n = lay.n_chunks(c)
    n_tok = qkv.shape[0]
    nh = DN_V_HEADS
    neg_a = -jnp.exp(a_log.astype(F32))
    na = jnp.concatenate([neg_a, jnp.zeros_like(neg_a)], axis=1).reshape(1, 4 * nh)
    dt = jnp.concatenate([dt_bias.astype(F32), jnp.zeros_like(neg_a)], axis=1).reshape(1, 4 * nh)
    small = lambda a: pl.BlockSpec(a.shape, lambda b, i: (0, 0))
    consts = (na, dt)
    state = [pltpu.VMEM((nh, DN_HEAD_DIM, DN_HEAD_DIM), F32)]

    def specs(backward):
        rb = lambda b, i: lay.row_block(c, b, lay.seq_chunk(c, backward, i))
        return lambda w, off: pl.BlockSpec((c, w), lambda b, i: (rb(b, i), off))

    common = lambda blk: [blk(DN_QK_W, 0), blk(DN_QK_W, 1), blk(DN_V_W, 1), blk(4 * nh, 0)] + [small(a) for a in consts]
    blk = specs(False)
    o_f = pl.pallas_call(
        functools.partial(_dn_scan_kernel, False), grid=(lay.batch, n),
        in_specs=common(blk), out_specs=blk(DN_V_W, 0),
        out_shape=jax.ShapeDtypeStruct((n_tok, DN_V_W), F32),
        scratch_shapes=state, compiler_params=_params("arbitrary", "arbitrary"), name="dn_scan_fwd",
    )(qkv, qkv, qkv, ab, *consts)
    blk = specs(True)
    return pl.pallas_call(
        functools.partial(_dn_scan_kernel, True), grid=(lay.batch, n),
        in_specs=common(blk) + [blk(DN_V_W, 0), blk(DN_V_W, 2), small(norm_g.reshape(1, DN_HEAD_DIM))],
        out_specs=blk(DN_V_W, 0),
        out_shape=jax.ShapeDtypeStruct((n_tok, DN_V_W), BF16),
        scratch_shapes=state, compiler_params=_params("arbitrary", "arbitrary"), name="dn_scan_bwd",
    )(qkv, qkv, qkv, ab, *consts, o_f, p, norm_g.reshape(1, DN_HEAD_DIM))


def _unit_tri_inverse(nm, eye, heads):
    c = eye.shape[0]
    tm = [eye + nm[h] for h in heads]
    p = [_bdot(nm[h], nm[h]).astype(BF16) for h in heads]
    for _ in range(int(math.log2(c)) - 2):
        pt = [_bdot(jnp.concatenate([p[h], tm[h].astype(BF16)], axis=0), p[h]) for h in heads]
        tm = [tm[h] + pt[h][c:] for h in heads]
        p = [pt[h][:c].astype(BF16) for h in heads]
    return [tm[h] + _bdot(tm[h], p[h]) for h in heads]


def _rwkv_chunk_kernel(r_ref, v_ref, kk_ref, wl_ref, kd_ref, a_ref, o_ref, s_ref, *scratch):
    z = pl.program_id(0)

    @pl.when(pl.program_id(2) == 0)
    def _():
        s_ref[...] = jnp.zeros_like(s_ref)

    for j in range(RWKV_STEP_CHUNKS):
        sub = j + z * (RWKV_STEP_CHUNKS - 1 - 2 * j)
        rows = lambda r: r.at[pl.ds(pl.multiple_of(sub * RWKV_CHUNK, RWKV_CHUNK), RWKV_CHUNK)]
        _rwkv_chunk(z, rows(r_ref), rows(v_ref), rows(kk_ref), rows(wl_ref), rows(kd_ref), rows(a_ref),
                    rows(o_ref), s_ref, *scratch)


def _rwkv_chunk(z, r_ref, v_ref, kk_ref, wl_ref, kd_ref, a_ref, o_ref, s_ref, ar_s, bt_s, bk_s, uv_s, gc_s):
    c = r_ref.shape[0]
    row = lax.broadcasted_iota(jnp.int32, (c, c), 0)
    col = lax.broadcasted_iota(jnp.int32, (c, c), 1)
    lag = (row - col) * (1 - 2 * z)
    incl = lag >= 0
    strict = lag > 0

    wl = wl_ref[...]
    logw = -jnp.exp(-_softplus(-wl) - 0.5)
    cum = _hdot(incl.astype(F32), logw)
    c_last = jnp.sum(logw, axis=0, keepdims=True)
    kk = kk_ref[...].astype(F32)
    kb = kk * a_ref[...].astype(F32)
    kd = kd_ref[...].astype(F32)
    g_inv = jnp.exp(-cum)
    e_end = jnp.exp(c_last - cum)
    ar_s[0:c, :] = (-kk * jnp.exp(cum - logw)).astype(BF16)
    ar_s[c:, :] = (r_ref[...].astype(F32) * jnp.exp(cum)).astype(BF16)
    bt_s[0:c, :] = (kb * g_inv).astype(BF16)
    bt_s[c:, :] = (kd * g_inv).astype(BF16)
    bk_s[0:c, :] = (kb * e_end).astype(BF16)
    bk_s[c:, :] = (kd * e_end).astype(BF16)
    uv_s[c:, :] = v_ref[...].astype(BF16)
    gc_s[...] = jnp.exp(c_last)

    eye = (row == col).astype(F32)
    hh = range(RWKV_HEADS)
    sl = [slice(h * RWKV_HEAD, (h + 1) * RWKV_HEAD) for h in hh]
    lag2 = jnp.concatenate([lag, lag + 1], axis=0)
    mask4 = jnp.concatenate([lag2, lag2], axis=1) > 0
    x4 = [jnp.where(mask4, _bdot(ar_s[:, sl[h]], bt_s[:, sl[h]], _NT), 0.0) for h in hh]
    xb = [x4[h][:, :c] for h in hh]
    xk = [x4[h][:, c:].astype(BF16) for h in hh]
    tm = _unit_tri_inverse([xb[h][:c] for h in hh], eye, hh)
    s0 = [s_ref[h] for h in hh]
    xs = [_bdot(ar_s[:, sl[h]], s0[h], _NT) + _bdot(xk[h], uv_s[c:, sl[h]]) for h in hh]
    u = [_bdot(tm[h], xs[h][:c]).astype(BF16) for h in hh]
    for h in hh:
        uv_s[0:c, sl[h]] = u[h]
        o_ref[:, sl[h]] = xs[h][c:] + _bdot(xb[h][c:], u[h])
    for h in hh:
        s_ref[h] = s0[h] * gc_s[:, sl[h]] + _bdot(uv_s[:, sl[h]], bk_s[:, sl[h]], _TN)


def rwkv_scan(lay, r, v, kk, wl, kd, a):
    n_tok, d = r.shape
    step = RWKV_CHUNK * RWKV_STEP_CHUNKS
    n = lay.n_chunks(step)

    def rb(z, b, i):
        sc = jnp.where(z == 0, lay.seq_chunk(step, False, i), lay.seq_chunk(step, True, i))
        return lay.row_block(step, b, sc)

    shared = pl.BlockSpec((step, d), lambda z, b, i: (rb(z, b, i), 0))
    perdir = pl.BlockSpec((None, step, d), lambda z, b, i: (z, rb(z, b, i), 0))
    c = RWKV_CHUNK
    bf = lambda rows: pltpu.VMEM((rows, d), BF16)
    return pl.pallas_call(
        _rwkv_chunk_kernel,
        grid=(2, lay.batch, n),
        in_specs=[shared, shared, shared, perdir, perdir, perdir],
        out_specs=perdir,
        out_shape=jax.ShapeDtypeStruct((2, n_tok, d), F32),
        scratch_shapes=[pltpu.VMEM((RWKV_HEADS, RWKV_HEAD, RWKV_HEAD), F32),
                        bf(2 * c), bf(2 * c), bf(2 * c), bf(2 * c), pltpu.VMEM((1, d), F32)],
        compiler_params=_params("arbitrary", "arbitrary", "arbitrary"),
        name="rwkv_scan",
    )(r, v, kk, wl, kd, a)


def _rwkv_pre_kernel(lay, tm, prev_ref, h_ref, next_ref, mix_ref, wr_ref, wk_ref, wv_ref, w1_ref, a1_ref, g1_ref,
                     w2_ref, a2_ref, g2_ref, w0_ref, a0_ref, kk_ref, ka_ref, rk_ref, seg_ref,
                     r_out, v_out, kk_out, wl_out, kd_out, a_out, g_out, bonus_out, xe_s):
    j = pl.program_id(0)
    r0 = j * tm
    in_ctx = r0 < lay.ctx_tok
    seq_len = jnp.where(in_ctx, lay.n_ctx, lay.seq)
    off = jnp.where(in_ctx, r0, r0 - lay.ctx_tok) % seq_len
    hal = SEQ_HALO
    xe_s[0:hal, :] = jnp.where(off == 0, 0.0, prev_ref[...].astype(F32))
    xe_s[hal:hal + tm, :] = h_ref[...].astype(F32)
    xe_s[hal + tm:, :] = jnp.where(off + tm == seq_len, 0.0, next_ref[...].astype(F32))
    h = xe_s[pl.ds(hal, tm), :]
    xx = 0.5 * (xe_s[pl.ds(hal - 1, tm), :] + xe_s[pl.ds(hal + 1, tm), :]) - h
    xm = lambda i: (h + xx * mix_ref[i:i + 1, :]).astype(BF16)
    r = _bdot(xm(0), wr_ref[...])
    k = _bdot(xm(1), wk_ref[...])
    v = _bdot(xm(2), wv_ref[...])
    hw = jnp.tanh(_bdot(xm(3), w1_ref[...]))
    ha = _bdot(xm(4), a1_ref[...])
    hg = _sigmoid(_bdot(xm(5), g1_ref[...]))
    seg = seg_ref[...]
    kx = k * kk_ref[...]
    r_out[...] = r.astype(r_out.dtype)
    v_out[...] = v.astype(v_out.dtype)
    kk_out[...] = (kx * lax.rsqrt(_bdot(kx * kx, seg) + L2_EPS)).astype(kk_out.dtype)
    g_out[...] = _bdot(hg, g2_ref[...]).astype(g_out.dtype)
    lw = w2_ref.shape[1]
    rr = r * rk_ref[...]
    bsum = None
    for z in range(2):
        wl_out[z] = w0_ref[z:z + 1, :] + _bdot(hw[:, z * lw:(z + 1) * lw], w2_ref[z])
        a = _sigmoid(a0_ref[z:z + 1, :] + _bdot(ha[:, z * lw:(z + 1) * lw], a2_ref[z]))
        kd = k * (1.0 + (a - 1.0) * ka_ref[...])
        a_out[z] = a.astype(a_out.dtype)
        kd_out[z] = kd.astype(kd_out.dtype)
        bsum = rr * kd if bsum is None else bsum + rr * kd
    bonus_out[...] = (_bdot(bsum, seg) * v).astype(bonus_out.dtype)


def _rwkv_post_kernel(o_ref, g_ref, bonus_ref, lnx_ref, seg_ref, a_out):
    seg = seg_ref[...]
    o = o_ref[0] + o_ref[1]
    hi = o.astype(BF16)
    lo = o - hi.astype(F32)
    inv_n = 1.0 / RWKV_HEAD
    oc = o - (_bdot(hi, seg) + _bdot(lo, seg)) * inv_n
    var = _bdot(oc * oc, seg) * inv_n
    on = oc * lax.rsqrt(var + LNX_EPS) * lnx_ref[...] + bonus_ref[...].astype(F32)
    a_out[...] = (on * g_ref[...].astype(F32)).astype(a_out.dtype)


def rwkv7_mix(lay, h, mix, w_rkv, w0, w1, w2, a0, a1, a2, g1, g2, k_k, k_a, r_k, lnx_g):
    n_tok, d = h.shape
    bw = lambda w: w.astype(BF16)
    tm = math.gcd(256, lay.n_ctx, lay.seq)
    hb = tm // SEQ_HALO
    last_hb = n_tok // SEQ_HALO - 1
    head_of = jnp.arange(d) // RWKV_HEAD
    seg = (head_of[:, None] == head_of[None, :]).astype(BF16)
    full = lambda a: pl.BlockSpec(a.shape, lambda j: (0,) * a.ndim)
    row = lambda a: a.reshape(1, d)
    consts = (mix, bw(w_rkv[0]), bw(w_rkv[1]), bw(w_rkv[2]), bw(jnp.concatenate([w1[0], w1[1]], -1)),
              bw(jnp.concatenate([a1[0], a1[1]], -1)), bw(g1), bw(w2), bw(a2), bw(g2), w0, a0,
              row(k_k), row(k_a), row(r_k), seg)
    tok = pl.BlockSpec((tm, d), lambda j: (j, 0))
    tok2 = pl.BlockSpec((2, tm, d), lambda j: (0, j, 0))
    one = lambda dt: jax.ShapeDtypeStruct((n_tok, d), dt)
    two = lambda dt: jax.ShapeDtypeStruct((2, n_tok, d), dt)
    r, v, kk, wl, kd, a, g, bonus = pl.pallas_call(
        functools.partial(_rwkv_pre_kernel, lay, tm),
        grid=(n_tok // tm,),
        in_specs=[pl.BlockSpec((SEQ_HALO, d), lambda j: (jnp.maximum(j * hb - 1, 0), 0)), tok,
                  pl.BlockSpec((SEQ_HALO, d), lambda j: (jnp.minimum((j + 1) * hb, last_hb), 0))]
                 + [full(a) for a in consts],
        out_specs=[tok, tok, tok, tok2, tok2, tok2, tok, tok],
        out_shape=[one(BF16), one(BF16), one(BF16), two(F32), two(BF16), two(BF16), one(BF16), one(BF16)],
        scratch_shapes=[pltpu.VMEM((tm + 2 * SEQ_HALO, d), F32)],
        compiler_params=_params("arbitrary"), name="rwkv_pre",
    )(h, h, h, *consts)
    o = rwkv_scan(lay, r, v, kk, wl, kd, a)
    return pl.pallas_call(
        _rwkv_post_kernel,
        grid=(n_tok // tm,),
        in_specs=[tok2, tok, tok, full(row(lnx_g)), full(seg)],
        out_specs=tok,
        out_shape=one(BF16),
        compiler_params=_params("arbitrary"), name="rwkv_post",
    )(o, g, bonus, row(lnx_g), seg)


def _moe_kernel(be_ref, rows_ref, x_ref, wg_ref, wu_ref, wd_ref, gate_ref, o_ref, acc_s):
    j = pl.program_id(0)
    f = pl.program_id(1)

    @pl.when(f == 0)
    def _():
        acc_s[...] = jnp.zeros_like(acc_s)

    half = MOE_ROWS // 2

    def ffn(rows):
        x = x_ref[rows, :]
        g = _bdot(x, wg_ref[...])
        u = _bdot(x, wu_ref[...])
        acc_s[rows, :] += _bdot(g * _sigmoid(g) * u, wd_ref[...])

    pl.when(rows_ref[j] > half)(lambda: ffn(slice(0, MOE_ROWS)))
    pl.when((rows_ref[j] > 0) & (rows_ref[j] <= half))(lambda: ffn(slice(0, half)))

    @pl.when(f == pl.num_programs(1) - 1)
    def _():
        o_ref[...] = (acc_s[...] * gate_ref[...]).astype(o_ref.dtype)


def moe_experts(xb, block_e, block_rows, w_gu, w_down, layer, slot_gate):
    n_slots, d = xb.shape
    bm = MOE_ROWS
    nf = MOE_F_CHUNKS
    tf = w_down.shape[2] // nf
    return pl.pallas_call(
        _moe_kernel,
        grid_spec=pltpu.PrefetchScalarGridSpec(
            num_scalar_prefetch=2,
            grid=(n_slots // bm, nf),
            in_specs=[pl.BlockSpec((bm, d), lambda j, f, be, nb: (j, 0)),
                      pl.BlockSpec((None, None, d, tf), lambda j, f, be, nb: (layer, be[j], 0, f)),
                      pl.BlockSpec((None, None, d, tf), lambda j, f, be, nb: (layer, be[j], 0, f + nf)),
                      pl.BlockSpec((None, None, tf, d), lambda j, f, be, nb: (layer, be[j], f, 0)),
                      pl.BlockSpec((bm, 1), lambda j, f, be, nb: (j, 0))],
            out_specs=pl.BlockSpec((bm, d), lambda j, f, be, nb: (j, 0)),
            scratch_shapes=[pltpu.VMEM((bm, d), F32)]),
        out_shape=jax.ShapeDtypeStruct((n_slots, d), BF16),
        compiler_params=_params("arbitrary", "arbitrary"),
        name="moe_experts",
    )(block_e, block_rows, xb, w_gu, w_gu, w_down, slot_gate)


def moe_swiglu(h, logits, w_gu, w_down, layer):
    n, d = h.shape
    lanes = jnp.arange(N_EXPERTS, dtype=jnp.int32)
    e1 = jnp.argmax(logits, axis=-1).astype(jnp.int32)
    rest = jnp.where(lanes == e1[:, None], -jnp.inf, logits)
    e2 = jnp.argmax(rest, axis=-1).astype(jnp.int32)
    top_logit = jnp.stack([jnp.max(logits, axis=-1), jnp.max(rest, axis=-1)], axis=-1)
    top_e = jnp.stack([e1, e2], axis=-1)
    gate = jax.nn.softmax(top_logit, axis=-1)
    flat_e = top_e.reshape(-1).astype(jnp.int32)
    order = jnp.argsort(flat_e).astype(jnp.int32)
    onehot = (flat_e[:, None] == jnp.arange(N_EXPERTS, dtype=jnp.int32)).astype(jnp.int32)
    seen = jnp.cumsum(onehot, axis=0)
    counts = seen[-1]
    padded = (counts + MOE_ROWS - 1) // MOE_ROWS * MOE_ROWS
    start = jnp.cumsum(counts) - counts
    pend = jnp.cumsum(padded)
    pstart = pend - padded
    tok_slot = jnp.sum(onehot * (seen - 1 + pstart[None, :]), axis=1).reshape(n, TOP_K)
    n_slots = (n * TOP_K + MOE_ROWS - 1) // MOE_ROWS * MOE_ROWS + N_EXPERTS * MOE_ROWS
    n_blocks = n_slots // MOE_ROWS
    blocks = jnp.arange(n_blocks, dtype=jnp.int32)
    block_e = jnp.minimum(jnp.sum(blocks[:, None] * MOE_ROWS >= pend[None, :], axis=1),
                          N_EXPERTS - 1).astype(jnp.int32)
    block_off = blocks * MOE_ROWS - pstart[block_e]
    block_rows = jnp.clip(counts[block_e] - block_off, 0, MOE_ROWS).astype(jnp.int32)
    slot_off = block_off[:, None] + jnp.arange(MOE_ROWS, dtype=jnp.int32)[None, :]
    slot_valid = (slot_off < counts[block_e][:, None]).reshape(-1)
    slot_asg = order[jnp.clip(start[block_e][:, None] + slot_off, 0, n * TOP_K - 1).reshape(-1)]
    slot_tok = jnp.where(slot_valid, slot_asg // TOP_K, jnp.arange(n_slots, dtype=jnp.int32) % n)
    slot_gate = jnp.where(slot_valid, gate.reshape(-1)[slot_asg], 0.0)
    yb = moe_experts(h[slot_tok], block_e, block_rows, w_gu, w_down, layer, slot_gate[:, None])
    return yb[tok_slot[:, 0]], yb[tok_slot[:, 1]]


def kernel(x, c, ctx, c_ctx, mod_w, mod_b, ln_g, ln_b,
           ret_w_in, ret_decay, ret_gn_g, ret_w_out,
           dn_w_in, dn_conv_w, dn_a_log, dn_dt_bias, dn_norm_g, dn_w_out,
           rk_mix, rk_w_rkv, rk_w0, rk_w1, rk_w2, rk_a0, rk_a1, rk_a2, rk_g1, rk_g2,
           rk_k_k, rk_k_a, rk_r_k, rk_lnx_g, rk_w_out,
           ffn_w_gu, ffn_w_down, moe_router, moe_w_gu, moe_w_down):
    bsz, t, d = x.shape
    n_ctx = ctx.shape[1]
    lay = Layout(bsz, n_ctx, t)
    bw = lambda w: w.astype(BF16)
    s_rows = jax.nn.silu(jnp.concatenate([c_ctx[None], c], 0))
    s_pad = jnp.zeros((8, d), F32).at[:1 + bsz].set(s_rows)
    mods = modulation_rows(s_pad, mod_w, mod_b)[:, :1 + bsz].reshape(DEPTH, 1 + bsz, 6, 1, d)
    mod = lambda i, k: mods[i, :, k]
    cos_t, sin_t = rope_tables(lay)
    xs = jnp.concatenate([ctx.reshape(-1, d), x.reshape(-1, d)], 0)
    h = modulate(lay, xs, mod(0, 1), mod(0, 0))
    for i in range(DEPTH):
        last = i == DEPTH - 1
        kind, j = i % N_MIXERS, i // N_MIXERS
        if kind == 0:
            p = proj(h, bw(ret_w_in[j]), BF16, PROJ_COLS)
            log_gamma = jax.nn.log_sigmoid(ret_decay[j].astype(F32))
            a = retention_mix(lay, p, log_gamma, ret_gn_g[j], cos_t, sin_t)
            w_out = ret_w_out[j]
        elif kind == 1:
            n_main = 2 * DN_QK_W + 2 * DN_V_W
            p = proj(h, bw(dn_w_in[j][:, :n_main]), BF16, PROJ_COLS)
            w_ab = bw(dn_w_in[j][:, n_main:])
            ab = proj(h, w_ab, F32, w_ab.shape[1])
            qkv = dn_prep(lay, p, dn_conv_w[j])
            a = deltanet_mix(lay, qkv, p, ab, dn_a_log[j], dn_dt_bias[j], dn_norm_g[j])
            w_out = dn_w_out[j]
        else:
            a = rwkv7_mix(lay, h, rk_mix[j], rk_w_rkv[j], rk_w0[j], rk_w1[j], rk_w2[j], rk_a0[j], rk_a1[j],
                          rk_a2[j], rk_g1[j], rk_g2[j], rk_k_k[j], rk_k_a[j], rk_r_k[j], rk_lnx_g[j])
            w_out = rk_w_out[j]
        router = bw(moe_router[i // 2]) if i % 2 == 1 else None
        xs, h, *logits = out_ln(lay, a, bw(w_out), xs, mod(i, 2), ln_g[i, 0], ln_b[i, 0], mod(i, 4), mod(i, 3),
                                router=router)
        nxt = (i + 1) % DEPTH
        if i % 2 == 0:
            hm = swiglu_in(h, bw(ffn_w_gu[i // 2]), FFN_COLS)
            xs, h = out_ln(lay, hm, bw(ffn_w_down[i // 2]), xs, mod(i, 5), ln_g[i, 1], ln_b[i, 1],
                           mod(nxt, 1), mod(nxt, 0), latents_only=last)
        else:
            f = moe_swiglu(h, logits[0], moe_w_gu, moe_w_down, i // 2)
            xs, h = out_ln(lay, f, None, xs, mod(i, 5), ln_g[i, 1], ln_b[i, 1], mod(nxt, 1), mod(nxt, 0),
                           latents_only=last)
    return xs.reshape(bsz, t, d)
```
